```python
import jax, jax.numpy as jnp
from jax import lax
import numpy as np

D_MODEL = 1024
BATCH = 1
SEQ = 16384
DEPTH = 1

HEAD_DIM = 64
N_Q_HEADS = 8
N_KV_HEADS = 2
GROUP = N_Q_HEADS // N_KV_HEADS
WINDOW = 128
BLOCK = 128
ATTN_WIDTH = N_Q_HEADS * HEAD_DIM
KV_WIDTH = N_KV_HEADS * HEAD_DIM
CONV_WIDTH_CH = 512
CONV_TAPS = 3
MIX_WIDTH = ATTN_WIDTH + CONV_WIDTH_CH
IN_WIDTH = ATTN_WIDTH + 2 * KV_WIDTH + 3 * CONV_WIDTH_CH
D_FF = 2816
EPS = 1e-6
NEG_INF = -1e30

kernel_name = "hybrid_swa_sink_shortconv_convffn"


def rms_norm(x, w):
    xf = x.astype(jnp.float32)
    y = xf * lax.rsqrt(jnp.mean(xf * xf, axis=-1, keepdims=True) + EPS)
    return (y * w.astype(jnp.float32)).astype(x.dtype)


def causal_dwconv3(x, w):
    s = x.shape[1]
    xp = jnp.pad(x, ((0, 0), (CONV_TAPS - 1, 0), (0, 0)))
    y = xp[:, 0:s] * w[0]
    for i in range(1, CONV_TAPS):
        y = y + xp[:, i:i + s] * w[i]
    return y


def sliding_window_attention(q, k, v, sinks):
    bsz, s = q.shape[0], q.shape[1]
    nb = s // BLOCK
    qb = q.reshape(bsz, nb, BLOCK, N_KV_HEADS, GROUP, HEAD_DIM)
    pad = ((0, 0), (BLOCK, 0), (0, 0), (0, 0))
    kp = jnp.pad(k, pad).reshape(bsz, nb + 1, BLOCK, N_KV_HEADS, HEAD_DIM)
    vp = jnp.pad(v, pad).reshape(bsz, nb + 1, BLOCK, N_KV_HEADS, HEAD_DIM)
    kw = jnp.concatenate([kp[:, :-1], kp[:, 1:]], axis=2)
    vw = jnp.concatenate([vp[:, :-1], vp[:, 1:]], axis=2)
    scale = HEAD_DIM ** -0.5
    sc = jnp.einsum('bnqkgd,bnskd->bnkgqs', qb, kw).astype(jnp.float32) * scale
    qi = jnp.arange(BLOCK)[:, None]
    sj = jnp.arange(2 * BLOCK)[None, :]
    rel = qi + BLOCK - sj
    band = (rel >= 0) & (rel < WINDOW)
    kpos = jnp.arange(nb)[:, None] * BLOCK - BLOCK + jnp.arange(2 * BLOCK)[None, :]
    mask = band[None, :, :] & (kpos >= 0)[:, None, :]
    sc = jnp.where(mask[None, :, None, None, :, :], sc, jnp.float32(NEG_INF))
    sink = sinks.astype(jnp.float32).reshape(N_KV_HEADS, GROUP)[None, None, :, :, None, None]
    m = jnp.maximum(jnp.max(sc, axis=-1, keepdims=True), sink)
    p = jnp.exp(sc - m)
    denom = jnp.sum(p, axis=-1, keepdims=True) + jnp.exp(sink - m)
    p = (p / denom).astype(v.dtype)
    o = jnp.einsum('bnkgqs,bnskd->bnqkgd', p, vw)
    return o.reshape(bsz, s, ATTN_WIDTH)


def setup_inputs(seed: int = 0) -> dict:
    key = jax.random.key(seed)
    ks = jax.random.split(key, 16)
    f = jnp.float32
    L = DEPTH
    def gain(k, n):
        return jnp.ones((L, n), f) + 0.05 * jax.random.normal(k, (L, n), f)
    return {
        "x": jax.random.normal(ks[0], (BATCH, SEQ, D_MODEL), f),
        "attn_norm_w": gain(ks[1], D_MODEL),
        "w_in": jax.random.normal(ks[2], (L, D_MODEL, IN_WIDTH), f) * D_MODEL ** -0.5,
        "q_norm_w": gain(ks[3], HEAD_DIM),
        "k_norm_w": gain(ks[4], HEAD_DIM),
        "sinks": 0.5 * jax.random.normal(ks[5], (L, N_Q_HEADS), f),
        "conv_mix_w": jax.random.normal(ks[6], (L, CONV_TAPS, CONV_WIDTH_CH), f) * CONV_TAPS ** -0.5,
        "attn_out_norm_w": gain(ks[7], ATTN_WIDTH),
        "conv_out_norm_w": gain(ks[8], CONV_WIDTH_CH),
        "w_out": jax.random.normal(ks[9], (L, MIX_WIDTH, D_MODEL), f) * MIX_WIDTH ** -0.5,
        "ffn_norm_w": gain(ks[10], D_MODEL),
        "w_up": jax.random.normal(ks[11], (L, D_MODEL, 2 * D_FF), f) * D_MODEL ** -0.5,
        "ffn_conv_w": jax.random.normal(ks[12], (L, CONV_TAPS, 2 * D_FF), f) * CONV_TAPS ** -0.5,
        "w_down": jax.random.normal(ks[13], (L, D_FF, D_MODEL), f) * D_FF ** -0.5,
    }


def reference(x, attn_norm_w, w_in, q_norm_w, k_norm_w, sinks, conv_mix_w,
              attn_out_norm_w, conv_out_norm_w, w_out, ffn_norm_w, w_up,
              ffn_conv_w, w_down):
    bsz, s = x.shape[0], x.shape[1]
    splits = [ATTN_WIDTH,
              ATTN_WIDTH + KV_WIDTH,
              ATTN_WIDTH + 2 * KV_WIDTH,
              ATTN_WIDTH + 2 * KV_WIDTH + CONV_WIDTH_CH,
              ATTN_WIDTH + 2 * KV_WIDTH + 2 * CONV_WIDTH_CH]
    for l in range(DEPTH):
        h = rms_norm(x, attn_norm_w[l])
        proj = jnp.einsum('bsd,de->bse', h, w_in[l])
        q, k, v, bg, cg, u = jnp.split(proj, splits, axis=-1)
        q = rms_norm(q.reshape(bsz, s, N_Q_HEADS, HEAD_DIM), q_norm_w[l])
        k = rms_norm(k.reshape(bsz, s, N_KV_HEADS, HEAD_DIM), k_norm_w[l])
        v = v.reshape(bsz, s, N_KV_HEADS, HEAD_DIM)
        attn = sliding_window_attention(q, k, v, sinks[l])
        conv = bg * causal_dwconv3(cg * u, conv_mix_w[l])
        mix = jnp.concatenate([rms_norm(attn, attn_out_norm_w[l]),
                               rms_norm(conv, conv_out_norm_w[l])], axis=-1)
        x = x + jnp.einsum('bse,ed->bsd', mix, w_out[l])
        h = rms_norm(x, ffn_norm_w[l])
        up = causal_dwconv3(jnp.einsum('bsd,df->bsf', h, w_up[l]), ffn_conv_w[l])
        g, uu = jnp.split(up, [D_FF], axis=-1)
        x = x + jnp.einsum('bsf,fd->bsd', jax.nn.silu(g) * uu, w_down[l])
    return x
```

```python
import functools

import jax
import jax.numpy as jnp
from jax import lax
from jax.experimental import pallas as pl
from jax.experimental.pallas import tpu as pltpu

HEAD_DIM = 64
N_Q_HEADS = 8
N_KV_HEADS = 2
WINDOW = 128
BLOCK = 128
ATTN_WIDTH = N_Q_HEADS * HEAD_DIM
KV_WIDTH = N_KV_HEADS * HEAD_DIM
CONV_WIDTH = 512
CONV_TAPS = 3
EPS = 1e-6
NEG_INF = -1e30

LANES = 128
SUBLANES = 8
MXU_DIM = 256
VMEM_LIMIT_BYTES = 56 * 1024 * 1024

TOKEN_TILE = 512
HALO = SUBLANES

F32 = jnp.float32
BF16 = jnp.bfloat16


def _rms(x, w):
    ms = jnp.mean(x * x, axis=-1, keepdims=True)
    return x * lax.rsqrt(ms + EPS) * w


def _causal_conv3(scr, cw, n):
    y = cw[0:1, :] * scr[pl.ds(HALO - 2, n), :]
    y = y + cw[1:2, :] * scr[pl.ds(HALO - 1, n), :]
    return y + cw[2:3, :] * scr[pl.ds(HALO, n), :]


def _in_proj_kernel(x_ref, nw_ref, wqkv_ref, wconv_ref, qkw_ref, bd_ref, cw_ref, cnw_ref,
                    q_ref, kk_ref, vv_ref, cn_ref, cu_scr):
    tm = x_ref.shape[0]
    h = _rms(x_ref[...], nw_ref[...]).astype(BF16)

    qkv = jnp.dot(h, wqkv_ref[...], preferred_element_type=F32)
    qk = qkv[:, :ATTN_WIDTH + KV_WIDTH]
    sq = (qk * qk).astype(BF16)
    bd = bd_ref[...]
    ssq = jnp.concatenate(
        [jnp.dot(sq[:, 0:MXU_DIM], bd, preferred_element_type=F32),
         jnp.dot(sq[:, MXU_DIM:2 * MXU_DIM], bd, preferred_element_type=F32),
         jnp.dot(sq[:, 2 * MXU_DIM:], bd[:KV_WIDTH, :KV_WIDTH], preferred_element_type=F32)],
        axis=1)
    qkn = qk * lax.rsqrt(ssq * (1.0 / HEAD_DIM) + EPS) * qkw_ref[...]
    q_ref[...] = qkn[:, :ATTN_WIDTH].astype(BF16)

    low = lax.broadcasted_iota(jnp.int32, (tm, KV_WIDTH), 1) < HEAD_DIM
    for src, dst in ((qkn[:, ATTN_WIDTH:], kk_ref), (qkv[:, ATTN_WIDTH + KV_WIDTH:], vv_ref)):
        swapped = pltpu.roll(src, HEAD_DIM, axis=1)
        dst[:, 0 * LANES:1 * LANES] = jnp.where(low, src, 0.0).astype(BF16)
        dst[:, 1 * LANES:2 * LANES] = jnp.where(low, 0.0, swapped).astype(BF16)
        dst[:, 2 * LANES:3 * LANES] = jnp.where(low, swapped, 0.0).astype(BF16)
        dst[:, 3 * LANES:4 * LANES] = jnp.where(low, 0.0, src).astype(BF16)

    pc = jnp.dot(h, wconv_ref[...], preferred_element_type=F32)
    bg = pc[:, :CONV_WIDTH]

    @pl.when(pl.program_id(1) == 0)
    def _():
        cu_scr[pl.ds(0, HALO), :] = jnp.zeros((HALO, CONV_WIDTH), F32)

    cu_scr[pl.ds(HALO, tm), :] = pc[:, CONV_WIDTH:2 * CONV_WIDTH] * pc[:, 2 * CONV_WIDTH:]
    conv = bg * _causal_conv3(cu_scr, cw_ref[...], tm)
    cn_ref[...] = _rms(conv, cnw_ref[...]).astype(BF16)
    cu_scr[pl.ds(0, HALO), :] = cu_scr[pl.ds(tm, HALO), :]


def _attn_kernel(sinks_ref, q_ref, kc_ref, kp_ref, vc_ref, vp_ref, cn_ref, x_ref, wo_ref, anw_ref,
                 o_ref, attn_scr):
    tq = q_ref.shape[0]
    first_tile = pl.program_id(1) == 0
    qi = lax.broadcasted_iota(jnp.int32, (BLOCK, 2 * BLOCK), 0)
    sj = lax.broadcasted_iota(jnp.int32, (BLOCK, 2 * BLOCK), 1)
    rel = qi + BLOCK - sj
    band = (rel >= 0) & (rel < WINDOW)
    band0 = band & ((sj >= BLOCK) | jnp.logical_not(first_tile))
    low = lax.broadcasted_iota(jnp.int32, (BLOCK, LANES), 1) < HEAD_DIM
    nt = (((1,), (1,)), ((), ()))

    for j in range(tq // BLOCK):
        mask = band0 if j == 0 else band
        rows = pl.ds(j * BLOCK, BLOCK)
        for p in range(N_Q_HEADS // 2):
            kv = (2 * p) // (N_Q_HEADS // N_KV_HEADS)
            q = q_ref[rows, pl.ds(p * LANES, LANES)]
            acc = None
            inv = []
            for e in range(2):
                cols = pl.ds((2 * kv + e) * LANES, LANES)
                if j == 0:
                    k = jnp.concatenate([kp_ref[:, cols], kc_ref[pl.ds(0, BLOCK), cols]], axis=0)
                    v = jnp.concatenate([vp_ref[:, cols], vc_ref[pl.ds(0, BLOCK), cols]], axis=0)
                else:
                    k = kc_ref[pl.ds((j - 1) * BLOCK, 2 * BLOCK), cols]
                    v = vc_ref[pl.ds((j - 1) * BLOCK, 2 * BLOCK), cols]
                s = lax.dot_general(q, k, nt, preferred_element_type=F32)
                s = jnp.where(mask, s, NEG_INF)
                sink = sinks_ref[2 * p + e]
                m = jnp.maximum(jnp.max(s, axis=-1, keepdims=True), sink)
                pe = jnp.exp(s - m)
                denom = jnp.sum(pe, axis=-1, keepdims=True) + jnp.exp(sink - m)
                inv.append(1.0 / denom)
                pv = jnp.dot(pe.astype(BF16), v, preferred_element_type=F32)
                acc = pv if acc is None else acc + pv
            attn_scr[rows, pl.ds(p * LANES, LANES)] = acc * jnp.where(low, inv[0], inv[1])

    an = _rms(attn_scr[...], anw_ref[...]).astype(BF16)
    y = jnp.dot(an, wo_ref[pl.ds(0, ATTN_WIDTH), :], preferred_element_type=F32)
    y = y + jnp.dot(cn_ref[...], wo_ref[pl.ds(ATTN_WIDTH, CONV_WIDTH), :], preferred_element_type=F32)
    o_ref[...] = x_ref[...] + y


def _ffn_kernel(chunks, x_ref, nw_ref, wg_ref, wu_ref, cwg_ref, cwu_ref, wd_ref, o_ref, *scr):
    tm = x_ref.shape[0]
    x = x_ref[...]
    h = _rms(x, nw_ref[...]).astype(BF16)

    @pl.when(pl.program_id(1) == 0)
    def _():
        for s in scr:
            s[pl.ds(0, HALO), :] = jnp.zeros((HALO, s.shape[1]), F32)

    acc = None
    for c, (lo, n) in enumerate(chunks):
        g_scr, u_scr = scr[2 * c], scr[2 * c + 1]
        cols = pl.ds(lo, n)
        g_scr[pl.ds(HALO, tm), :] = jnp.dot(h, wg_ref[:, cols], preferred_element_type=F32)
        u_scr[pl.ds(HALO, tm), :] = jnp.dot(h, wu_ref[:, cols], preferred_element_type=F32)
        g = _causal_conv3(g_scr, cwg_ref[:, cols], tm)
        u = _causal_conv3(u_scr, cwu_ref[:, cols], tm)
        a = (g / (1.0 + jnp.exp(-g)) * u).astype(BF16)
        d = jnp.dot(a, wd_ref[cols, :], preferred_element_type=F32)
        acc = d if acc is None else acc + d
        g_scr[pl.ds(0, HALO), :] = g_scr[pl.ds(tm, HALO), :]
        u_scr[pl.ds(0, HALO), :] = u_scr[pl.ds(tm, HALO), :]
    o_ref[...] = x + acc


def _resident(shape):
    return pl.BlockSpec(shape, lambda b, i: (0,) * len(shape), pipeline_mode=pl.Buffered(1))


def _tile(tm, width):
    return pl.BlockSpec((None, tm, width), lambda b, i: (b, i, 0))


def _params():
    return pltpu.CompilerParams(dimension_semantics=("arbitrary", "arbitrary"),
                                vmem_limit_bytes=VMEM_LIMIT_BYTES)


def _ffn_chunks(d_ff):
    tiles = d_ff // MXU_DIM
    first = (tiles + 1) // 2 * MXU_DIM
    return ((0, first), (first, d_ff - first))


def _layer(x, attn_norm_w, w_in, q_norm_w, k_norm_w, sinks, conv_mix_w, attn_out_norm_w,
           conv_out_norm_w, w_out, ffn_norm_w, w_up, ffn_conv_w, w_down):
    bsz, seq, d = x.shape
    d_ff = w_down.shape[0]
    tm = TOKEN_TILE
    assert seq % tm == 0 and tm % BLOCK == 0 and d_ff % MXU_DIM == 0
    grid = (bsz, seq // tm)
    qkv_w = ATTN_WIDTH + 2 * KV_WIDTH

    w_in = w_in.astype(BF16)
    w_qkv, w_conv = w_in[:, :qkv_w], w_in[:, qkv_w:]
    scale = HEAD_DIM ** -0.5
    qkw = jnp.concatenate([jnp.tile(q_norm_w, N_Q_HEADS) * scale, jnp.tile(k_norm_w, N_KV_HEADS)])[None, :]
    seg = jnp.arange(MXU_DIM) // HEAD_DIM
    bd = (seg[:, None] == seg[None, :]).astype(BF16)
    w_out = w_out.astype(BF16)
    w_up = w_up.astype(BF16)
    w_g, w_u = w_up[:, :d_ff], w_up[:, d_ff:]
    cw_g, cw_u = ffn_conv_w[:, :d_ff], ffn_conv_w[:, d_ff:]
    w_down = w_down.astype(BF16)

    act = jax.ShapeDtypeStruct((bsz, seq, 4 * LANES), BF16)
    qn, kk, vv, cn = pl.pallas_call(
        _in_proj_kernel,
        grid=grid,
        in_specs=[_tile(tm, d), _resident((1, d)), _resident((d, qkv_w)), _resident((d, 3 * CONV_WIDTH)),
                  _resident((1, ATTN_WIDTH + KV_WIDTH)), _resident((MXU_DIM, MXU_DIM)),
                  _resident((CONV_TAPS, CONV_WIDTH)), _resident((1, CONV_WIDTH))],
        out_specs=[_tile(tm, 4 * LANES)] * 4,
        out_shape=[act] * 4,
        scratch_shapes=[pltpu.VMEM((tm + HALO, CONV_WIDTH), F32)],
        compiler_params=_params(),
        name="in_proj",
    )(x, attn_norm_w[None, :], w_qkv, w_conv, qkw, bd, conv_mix_w, conv_out_norm_w[None, :])

    bpt = tm // BLOCK
    prev = pl.BlockSpec((None, BLOCK, 4 * LANES), lambda b, i: (b, jnp.maximum(i * bpt - 1, 0), 0))
    x = pl.pallas_call(
        _attn_kernel,
        grid=grid,
        in_specs=[pl.BlockSpec(memory_space=pltpu.SMEM),
                  _tile(tm, 4 * LANES), _tile(tm, 4 * LANES), prev, _tile(tm, 4 * LANES), prev,
                  _tile(tm, 4 * LANES), _tile(tm, d), _resident((ATTN_WIDTH + CONV_WIDTH, d)),
                  _resident((1, ATTN_WIDTH))],
        out_specs=_tile(tm, d),
        out_shape=jax.ShapeDtypeStruct(x.shape, x.dtype),
        scratch_shapes=[pltpu.VMEM((tm, ATTN_WIDTH), F32)],
        compiler_params=_params(),
        name="attn_out_proj",
    )(sinks, qn, kk, kk, vv, vv, cn, x, w_out, attn_out_norm_w[None, :])

    chunks = _ffn_chunks(d_ff)
    scratch = []
    for _, n in chunks:
        scratch += [pltpu.VMEM((tm + HALO, n), F32)] * 2
    x = pl.pallas_call(
        functools.partial(_ffn_kernel, chunks),
        grid=grid,
        in_specs=[_tile(tm, d), _resident((1, d)), _resident((d, d_ff)), _resident((d, d_ff)),
                  _resident((CONV_TAPS, d_ff)), _resident((CONV_TAPS, d_ff)), _resident((d_ff, d))],
        out_specs=_tile(tm, d),
        out_shape=jax.ShapeDtypeStruct(x.shape, x.dtype),
        scratch_shapes=scratch,
        compiler_params=_params(),
        name="conv_ffn",
    )(x, ffn_norm_w[None, :], w_g, w_u, cw_g, cw_u, w_down)
    return x


def kernel(x, attn_norm_w, w_in, q_norm_w, k_norm_w, sinks, conv_mix_w, attn_out_norm_w,
           conv_out_norm_w, w_out, ffn_norm_w, w_up, ffn_conv_w, w_down):
    for l in range(attn_norm_w.shape[0]):
        x = _layer(x, attn_norm_w[l], w_in[l], q_norm_w[l], k_norm_w[l], sinks[l], conv_mix_w[l],
                   attn_out_norm_w[l], conv_out_norm_w[l], w_out[l], ffn_norm_w[l], w_up[l],
                   ffn_conv_w[l], w_down[l])
    return x
```

```python
import functools

import jax
import jax.numpy as jnp
from jax import lax
from jax.experimental import pallas as pl
from jax.experimental.pallas import tpu as pltpu

HEAD_DIM = 64
N_Q_HEADS = 8
N_KV_HEADS = 2
WINDOW = 128
BLOCK = 128
ATTN_WIDTH = N_Q_HEADS * HEAD_DIM
KV_WIDTH = N_KV_HEADS * HEAD_DIM
CONV_WIDTH = 512
CONV_TAPS = 3
EPS = 1e-6
NEG_INF = -1e30

LANES = 128
SUBLANES = 8
MXU_DIM = 256
VMEM_LIMIT_BYTES = 56 * 1024 * 1024

TOKEN_TILE = 512
HALO = SUBLANES
SOFTMAX_ROWS = 16

F32 = jnp.float32
BF16 = jnp.bfloat16


def _rms(x, w):
    ms = jnp.mean(x * x, axis=-1, keepdims=True)
    return x * lax.rsqrt(ms + EPS) * w


def _causal_conv3(scr, cw, n):
    y = cw[0:1, :] * scr[pl.ds(HALO - 2, n), :]
    y = y + cw[1:2, :] * scr[pl.ds(HALO - 1, n), :]
    return y + cw[2:3, :] * scr[pl.ds(HALO, n), :]


def _in_proj_kernel(x_ref, nw_ref, wqkv_ref, wconv_ref, qkw_ref, bd_ref, cw_ref, cnw_ref,
                    q_ref, kk_ref, vv_ref, cn_ref, cu_scr):
    tm = x_ref.shape[0]
    h = _rms(x_ref[...], nw_ref[...]).astype(BF16)

    qkv = jnp.dot(h, wqkv_ref[...], preferred_element_type=F32)
    qk = qkv[:, :ATTN_WIDTH + KV_WIDTH]
    sq = (qk * qk).astype(BF16)
    bd = bd_ref[...]
    ssq = jnp.concatenate(
        [jnp.dot(sq[:, 0:MXU_DIM], bd, preferred_element_type=F32),
         jnp.dot(sq[:, MXU_DIM:2 * MXU_DIM], bd, preferred_element_type=F32),
         jnp.dot(sq[:, 2 * MXU_DIM:], bd[:KV_WIDTH, :KV_WIDTH], preferred_element_type=F32)],
        axis=1)
    qkn = qk * lax.rsqrt(ssq * (1.0 / HEAD_DIM) + EPS) * qkw_ref[...]
    q_ref[...] = qkn[:, :ATTN_WIDTH].astype(BF16)

    low = lax.broadcasted_iota(jnp.int32, (tm, KV_WIDTH), 1) < HEAD_DIM
    for src, dst in ((qkn[:, ATTN_WIDTH:], kk_ref), (qkv[:, ATTN_WIDTH + KV_WIDTH:], vv_ref)):
        swapped = pltpu.roll(src, HEAD_DIM, axis=1)
        dst[:, 0 * LANES:1 * LANES] = jnp.where(low, src, 0.0).astype(BF16)
        dst[:, 1 * LANES:2 * LANES] = jnp.where(low, 0.0, swapped).astype(BF16)
        dst[:, 2 * LANES:3 * LANES] = jnp.where(low, swapped, 0.0).astype(BF16)
        dst[:, 3 * LANES:4 * LANES] = jnp.where(low, 0.0, src).astype(BF16)

    pc = jnp.dot(h, wconv_ref[...], preferred_element_type=F32)
    bg = pc[:, :CONV_WIDTH]

    @pl.when(pl.program_id(1) == 0)
    def _():
        cu_scr[pl.ds(0, HALO), :] = jnp.zeros((HALO, CONV_WIDTH), F32)

    cu_scr[pl.ds(HALO, tm), :] = pc[:, CONV_WIDTH:2 * CONV_WIDTH] * pc[:, 2 * CONV_WIDTH:]
    conv = bg * _causal_conv3(cu_scr, cw_ref[...], tm)
    cn_ref[...] = _rms(conv, cnw_ref[...]).astype(BF16)
    cu_scr[pl.ds(0, HALO), :] = cu_scr[pl.ds(tm, HALO), :]


def _attn_kernel(sinks_ref, q_ref, kc_ref, kp_ref, vc_ref, vp_ref, cn_ref, x_ref, wo_ref, anw_ref,
                 o_ref, s_scr, p_scr, inv_scr, attn_scr, mask_scr):
    tq = q_ref.shape[0]
    nb = tq // BLOCK
    group_pairs = N_Q_HEADS // N_KV_HEADS // 2
    nt = (((1,), (1,)), ((), ()))

    qi = lax.broadcasted_iota(jnp.int32, (BLOCK, 2 * BLOCK), 0)
    sj = lax.broadcasted_iota(jnp.int32, (BLOCK, 2 * BLOCK), 1)
    rel = qi + BLOCK - sj
    band = (rel >= 0) & (rel < WINDOW)
    band0 = band & ((sj >= BLOCK) | (pl.program_id(1) > 0))
    mask_scr[0] = band0.astype(jnp.int32)
    mask_scr[1] = band.astype(jnp.int32)
    low = lax.broadcasted_iota(jnp.int32, (2 * BLOCK, LANES), 1) < HEAD_DIM

    def window(cur_ref, prev_ref, j, cols):
        if j == 0:
            return jnp.concatenate([prev_ref[:, cols], cur_ref[pl.ds(0, BLOCK), cols]], axis=0)
        return cur_ref[pl.ds((j - 1) * BLOCK, 2 * BLOCK), cols]

    def scores(j):
        rows = pl.ds(j * BLOCK, BLOCK)
        for c in range(N_KV_HEADS):
            qs = jnp.concatenate(
                [q_ref[rows, pl.ds((group_pairs * c + g) * LANES, LANES)] for g in range(group_pairs)], axis=0)
            for e in range(2):
                k = window(kc_ref, kp_ref, j, pl.ds((2 * c + e) * LANES, LANES))
                s_scr[j, 2 * c + e] = lax.dot_general(qs, k, nt, preferred_element_type=F32)

    def softmax(j):
        for ce in range(2 * N_KV_HEADS):
            c, e = divmod(ce, 2)
            for r in range(0, group_pairs * BLOCK, SOFTMAX_ROWS):
                rows = pl.ds(r, SOFTMAX_ROWS)
                keep = mask_scr[min(j, 1), pl.ds(r % BLOCK, SOFTMAX_ROWS), :] > 0
                s = jnp.where(keep, s_scr[j, ce, rows, :], NEG_INF)
                sink = sinks_ref[(N_Q_HEADS // N_KV_HEADS) * c + 2 * (r // BLOCK) + e]
                m = jnp.maximum(jnp.max(s, axis=-1, keepdims=True), sink)
                pe = jnp.exp(s - m)
                denom = jnp.sum(pe, axis=-1, keepdims=True) + jnp.exp(sink - m)
                p_scr[j, ce, rows, :] = pe.astype(BF16)
                inv_scr[j, ce, rows, :] = jnp.broadcast_to(1.0 / denom, (SOFTMAX_ROWS, LANES))

    def values(j):
        rows = pl.ds(j * BLOCK, BLOCK)
        for c in range(N_KV_HEADS):
            acc = None
            for e in range(2):
                v = window(vc_ref, vp_ref, j, pl.ds((2 * c + e) * LANES, LANES))
                pv = jnp.dot(p_scr[j, 2 * c + e], v, preferred_element_type=F32)
                acc = pv if acc is None else acc + pv
            o = acc * jnp.where(low, inv_scr[j, 2 * c], inv_scr[j, 2 * c + 1])
            for g in range(group_pairs):
                attn_scr[rows, pl.ds((group_pairs * c + g) * LANES, LANES)] = o[g * BLOCK:(g + 1) * BLOCK]

    for t in range(nb + 2):
        if t < nb:
            scores(t)
        if 0 <= t - 1 < nb:
            softmax(t - 1)
        if 0 <= t - 2 < nb:
            values(t - 2)

    an = _rms(attn_scr[...], anw_ref[...]).astype(BF16)
    y = jnp.dot(an, wo_ref[pl.ds(0, ATTN_WIDTH), :], preferred_element_type=F32)
    y = y + jnp.dot(cn_ref[...], wo_ref[pl.ds(ATTN_WIDTH, CONV_WIDTH), :], preferred_element_type=F32)
    o_ref[...] = x_ref[...] + y


def _ffn_kernel(chunks, x_ref, nw_ref, wg_ref, wu_ref, cwg_ref, cwu_ref, wd_ref, o_ref, *scr):
    tm = x_ref.shape[0]
    x = x_ref[...]
    h = _rms(x, nw_ref[...]).astype(BF16)

    @pl.when(pl.program_id(1) == 0)
    def _():
        for s in scr:
            s[pl.ds(0, HALO), :] = jnp.zeros((HALO, s.shape[1]), F32)

    acc = None
    for c, (lo, n) in enumerate(chunks):
        g_scr, u_scr = scr[2 * c], scr[2 * c + 1]
        cols = pl.ds(lo, n)
        g_scr[pl.ds(HALO, tm), :] = jnp.dot(h, wg_ref[:, cols], preferred_element_type=F32)
        u_scr[pl.ds(HALO, tm), :] = jnp.dot(h, wu_ref[:, cols], preferred_element_type=F32)
        g = _causal_conv3(g_scr, cwg_ref[:, cols], tm)
        u = _causal_conv3(u_scr, cwu_ref[:, cols], tm)
        a = (g / (1.0 + jnp.exp(-g)) * u).astype(BF16)
        d = jnp.dot(a, wd_ref[cols, :], preferred_element_type=F32)
        acc = d if acc is None else acc + d
        g_scr[pl.ds(0, HALO), :] = g_scr[pl.ds(tm, HALO), :]
        u_scr[pl.ds(0, HALO), :] = u_scr[pl.ds(tm, HALO), :]
    o_ref[...] = x + acc


def _resident(shape):
    return pl.BlockSpec(shape, lambda b, i: (0,) * len(shape), pipeline_mode=pl.Buffered(1))


def _tile(tm, width):
    return pl.BlockSpec((None, tm, width), lambda b, i: (b, i, 0))


def _params():
    return pltpu.CompilerParams(dimension_semantics=("arbitrary", "arbitrary"),
                                vmem_limit_bytes=VMEM_LIMIT_BYTES)


def _ffn_chunks(d_ff):
    tiles = d_ff // MXU_DIM
    first = (tiles + 1) // 2 * MXU_DIM
    return ((0, first), (first, d_ff - first))


def _layer(x, attn_norm_w, w_in, q_norm_w, k_norm_w, sinks, conv_mix_w, attn_out_norm_w,
           conv_out_norm_w, w_out, ffn_norm_w, w_up, ffn_conv_w, w_down):
    bsz, seq, d = x.shape
    d_ff = w_down.shape[0]
    tm = TOKEN_TILE
    assert seq % tm == 0 and tm % BLOCK == 0 and d_ff % MXU_DIM == 0
    grid = (bsz, seq // tm)
    qkv_w = ATTN_WIDTH + 2 * KV_WIDTH

    w_in = w_in.astype(BF16)
    w_qkv, w_conv = w_in[:, :qkv_w], w_in[:, qkv_w:]
    scale = HEAD_DIM ** -0.5
    qkw = jnp.concatenate([jnp.tile(q_norm_w, N_Q_HEADS) * scale, jnp.tile(k_norm_w, N_KV_HEADS)])[None, :]
    seg = jnp.arange(MXU_DIM) // HEAD_DIM
    bd = (seg[:, None] == seg[None, :]).astype(BF16)
    w_out = w_out.astype(BF16)
    w_up = w_up.astype(BF16)
    w_g, w_u = w_up[:, :d_ff], w_up[:, d_ff:]
    cw_g, cw_u = ffn_conv_w[:, :d_ff], ffn_conv_w[:, d_ff:]
    w_down = w_down.astype(BF16)

    act = jax.ShapeDtypeStruct((bsz, seq, 4 * LANES), BF16)
    qn, kk, vv, cn = pl.pallas_call(
        _in_proj_kernel,
        grid=grid,
        in_specs=[_tile(tm, d), _resident((1, d)), _resident((d, qkv_w)), _resident((d, 3 * CONV_WIDTH)),
                  _resident((1, ATTN_WIDTH + KV_WIDTH)), _resident((MXU_DIM, MXU_DIM)),
                  _resident((CONV_TAPS, CONV_WIDTH)), _resident((1, CONV_WIDTH))],
        out_specs=[_tile(tm, 4 * LANES)] * 4,
        out_shape=[act] * 4,
        scratch_shapes=[pltpu.VMEM((tm + HALO, CONV_WIDTH), F32)],
        compiler_params=_params(),
        name="in_proj",
    )(x, attn_norm_w[None, :], w_qkv, w_conv, qkw, bd, conv_mix_w, conv_out_norm_w[None, :])

    bpt = tm // BLOCK
    prev = pl.BlockSpec((None, BLOCK, 4 * LANES), lambda b, i: (b, jnp.maximum(i * bpt - 1, 0), 0))
    x = pl.pallas_call(
        _attn_kernel,
        grid=grid,
        in_specs=[pl.BlockSpec(memory_space=pltpu.SMEM),
                  _tile(tm, 4 * LANES), _tile(tm, 4 * LANES), prev, _tile(tm, 4 * LANES), prev,
                  _tile(tm, 4 * LANES), _tile(tm, d), _resident((ATTN_WIDTH + CONV_WIDTH, d)),
                  _resident((1, ATTN_WIDTH))],
        out_specs=_tile(tm, d),
        out_shape=jax.ShapeDtypeStruct(x.shape, x.dtype),
        scratch_shapes=[pltpu.VMEM((bpt, 2 * N_KV_HEADS, 2 * BLOCK, 2 * BLOCK), F32),
                        pltpu.VMEM((bpt, 2 * N_KV_HEADS, 2 * BLOCK, 2 * BLOCK), BF16),
                        pltpu.VMEM((bpt, 2 * N_KV_HEADS, 2 * BLOCK, LANES), F32),
                        pltpu.VMEM((tm, ATTN_WIDTH), F32),
                        pltpu.VMEM((2, BLOCK, 2 * BLOCK), jnp.int32)],
        compiler_params=_params(),
        name="attn_out_proj",
    )(sinks, qn, kk, kk, vv, vv, cn, x, w_out, attn_out_norm_w[None, :])

    chunks = _ffn_chunks(d_ff)
    scratch = []
    for _, n in chunks:
        scratch += [pltpu.VMEM((tm + HALO, n), F32)] * 2
    x = pl.pallas_call(
        functools.partial(_ffn_kernel, chunks),
        grid=grid,
        in_specs=[_tile(tm, d), _resident((1, d)), _resident((d, d_ff)), _resident((d, d_ff)),
                  _resident((CONV_TAPS, d_ff)), _resident((CONV_TAPS, d_ff)), _resident((d_ff, d))],
        out_specs=_tile(tm, d),
        out_shape=jax.ShapeDtypeStruct(x.shape, x.dtype),
        scratch_shapes=scratch,
        compiler_params=_params(),
        name="conv_ffn",
    )(x, ffn_norm_w[None, :], w_g, w_u, cw_g, cw_u, w_down)
    return x


def kernel(x, attn_norm_w, w_in, q_norm_w, k_norm_w, sinks, conv_mix_w, attn_out_norm_w,
           conv_out_norm_w, w_out, ffn_norm_w, w_up, ffn_conv_w, w_down):
    for l in range(attn_norm_w.shape[0]):
        x = _layer(x, attn_norm_w[l], w_in[l], q_norm_w[l], k_norm_w[l], sinks[l], conv_mix_w[l],
                   attn_out_norm_w[l], conv_out_norm_w[l], w_out[l], ffn_norm_w[l], w_up[l],
                   ffn_conv_w[l], w_down[l])
    return x
```

```python
import jax
import jax.numpy as jnp
from jax import lax
from jax.experimental import pallas as pl
from jax.experimental.pallas import tpu as pltpu

HEAD_DIM = 64
N_Q_HEADS = 8
N_KV_HEADS = 2
WINDOW = 128
BLOCK = 128
ATTN_WIDTH = N_Q_HEADS * HEAD_DIM
KV_WIDTH = N_KV_HEADS * HEAD_DIM
CONV_WIDTH = 512
CONV_TAPS = 3
EPS = 1e-6
NEG_INF = -1e30

LANES = 128
SUBLANES = 8
MXU_DIM = 256
VMEM_LIMIT_BYTES = 56 * 1024 * 1024

TOKEN_TILE = 512
HALO = SUBLANES
STRIDE = 4
SOFTMAX_ROWS = 16

F32 = jnp.float32
BF16 = jnp.bfloat16


def _rms(x, w):
    ms = jnp.mean(x * x, axis=-1, keepdims=True)
    return x * lax.rsqrt(ms + EPS) * w


def _causal_conv3(scr, cw, n):
    y = cw[0:1, :] * scr[pl.ds(HALO - 2, n), :]
    y = y + cw[1:2, :] * scr[pl.ds(HALO - 1, n), :]
    return y + cw[2:3, :] * scr[pl.ds(HALO, n), :]


def _in_proj_kernel(x_ref, nw_ref, wqkv_ref, wconv_ref, qkw_ref, bd_ref, cw_ref, cnw_ref,
                    q_ref, kk_ref, vv_ref, cn_ref, cu_scr):
    tm = x_ref.shape[0]
    h = _rms(x_ref[...], nw_ref[...]).astype(BF16)

    qkv = jnp.dot(h, wqkv_ref[...], preferred_element_type=F32)
    qk = qkv[:, :ATTN_WIDTH + KV_WIDTH]
    sq = (qk * qk).astype(BF16)
    bd = bd_ref[...]
    ssq = jnp.concatenate(
        [jnp.dot(sq[:, 0:MXU_DIM], bd, preferred_element_type=F32),
         jnp.dot(sq[:, MXU_DIM:2 * MXU_DIM], bd, preferred_element_type=F32),
         jnp.dot(sq[:, 2 * MXU_DIM:], bd[:KV_WIDTH, :KV_WIDTH], preferred_element_type=F32)],
        axis=1)
    qkn = qk * lax.rsqrt(ssq * (1.0 / HEAD_DIM) + EPS) * qkw_ref[...]
    q_ref[...] = qkn[:, :ATTN_WIDTH].astype(BF16)

    low = lax.broadcasted_iota(jnp.int32, (tm, KV_WIDTH), 1) < HEAD_DIM
    for src, dst in ((qkn[:, ATTN_WIDTH:], kk_ref), (qkv[:, ATTN_WIDTH + KV_WIDTH:], vv_ref)):
        swapped = pltpu.roll(src, HEAD_DIM, axis=1)
        dst[:, 0 * LANES:1 * LANES] = jnp.where(low, src, 0.0).astype(BF16)
        dst[:, 1 * LANES:2 * LANES] = jnp.where(low, 0.0, swapped).astype(BF16)
        dst[:, 2 * LANES:3 * LANES] = jnp.where(low, swapped, 0.0).astype(BF16)
        dst[:, 3 * LANES:4 * LANES] = jnp.where(low, 0.0, src).astype(BF16)

    pc = jnp.dot(h, wconv_ref[...], preferred_element_type=F32)
    bg = pc[:, :CONV_WIDTH]

    @pl.when(pl.program_id(1) == 0)
    def _():
        cu_scr[pl.ds(0, HALO), :] = jnp.zeros((HALO, CONV_WIDTH), F32)

    cu_scr[pl.ds(HALO, tm), :] = pc[:, CONV_WIDTH:2 * CONV_WIDTH] * pc[:, 2 * CONV_WIDTH:]
    conv = bg * _causal_conv3(cu_scr, cw_ref[...], tm)
    cn_ref[...] = _rms(conv, cnw_ref[...]).astype(BF16)
    cu_scr[pl.ds(0, HALO), :] = cu_scr[pl.ds(tm, HALO), :]


def _attn_kernel(sinks_ref, q_ref, kc_ref, kp_ref, vc_ref, vp_ref, cn_ref, x_ref, wo_ref, anw_ref,
                 o_ref, s_scr, p_scr, inv_scr, attn_scr, mask_scr):
    tq = q_ref.shape[0]
    nb = tq // BLOCK
    group_pairs = N_Q_HEADS // N_KV_HEADS // 2
    nt = (((1,), (1,)), ((), ()))

    qi = lax.broadcasted_iota(jnp.int32, (BLOCK, 2 * BLOCK), 0)
    sj = lax.broadcasted_iota(jnp.int32, (BLOCK, 2 * BLOCK), 1)
    rel = qi + BLOCK - sj
    band = (rel >= 0) & (rel < WINDOW)
    band0 = band & ((sj >= BLOCK) | (pl.program_id(1) > 0))
    mask_scr[0] = band0.astype(jnp.int32)
    mask_scr[1] = band.astype(jnp.int32)
    low = lax.broadcasted_iota(jnp.int32, (2 * BLOCK, LANES), 1) < HEAD_DIM

    def window(cur_ref, prev_ref, j, cols):
        if j == 0:
            return jnp.concatenate([prev_ref[:, cols], cur_ref[pl.ds(0, BLOCK), cols]], axis=0)
        return cur_ref[pl.ds((j - 1) * BLOCK, 2 * BLOCK), cols]

    def scores(j):
        rows = pl.ds(j * BLOCK, BLOCK)
        for c in range(N_KV_HEADS):
            qs = jnp.concatenate(
                [q_ref[rows, pl.ds((group_pairs * c + g) * LANES, LANES)] for g in range(group_pairs)], axis=0)
            for e in range(2):
                k = window(kc_ref, kp_ref, j, pl.ds((2 * c + e) * LANES, LANES))
                s_scr[j, 2 * c + e] = lax.dot_general(qs, k, nt, preferred_element_type=F32)

    def softmax(j):
        for ce in range(2 * N_KV_HEADS):
            c, e = divmod(ce, 2)
            for r in range(0, group_pairs * BLOCK, SOFTMAX_ROWS):
                rows = pl.ds(r, SOFTMAX_ROWS)
                keep = mask_scr[min(j, 1), pl.ds(r % BLOCK, SOFTMAX_ROWS), :] > 0
                s = jnp.where(keep, s_scr[j, ce, rows, :], NEG_INF)
                sink = sinks_ref[(N_Q_HEADS // N_KV_HEADS) * c + 2 * (r // BLOCK) + e]
                m = jnp.maximum(jnp.max(s, axis=-1, keepdims=True), sink)
                pe = jnp.exp(s - m)
                denom = jnp.sum(pe, axis=-1, keepdims=True) + jnp.exp(sink - m)
                p_scr[j, ce, rows, :] = pe.astype(BF16)
                inv_scr[j, ce, rows, :] = jnp.broadcast_to(1.0 / denom, (SOFTMAX_ROWS, LANES))

    def values(j):
        rows = pl.ds(j * BLOCK, BLOCK)
        for c in range(N_KV_HEADS):
            acc = None
            for e in range(2):
                v = window(vc_ref, vp_ref, j, pl.ds((2 * c + e) * LANES, LANES))
                pv = jnp.dot(p_scr[j, 2 * c + e], v, preferred_element_type=F32)
                acc = pv if acc is None else acc + pv
            o = acc * jnp.where(low, inv_scr[j, 2 * c], inv_scr[j, 2 * c + 1])
            for g in range(group_pairs):
                attn_scr[rows, pl.ds((group_pairs * c + g) * LANES, LANES)] = o[g * BLOCK:(g + 1) * BLOCK]

    for t in range(nb + 2):
        if t < nb:
            scores(t)
        if 0 <= t - 1 < nb:
            softmax(t - 1)
        if 0 <= t - 2 < nb:
            values(t - 2)

    an = _rms(attn_scr[...], anw_ref[...]).astype(BF16)
    y = jnp.dot(an, wo_ref[pl.ds(0, ATTN_WIDTH), :], preferred_element_type=F32)
    y = y + jnp.dot(cn_ref[...], wo_ref[pl.ds(ATTN_WIDTH, CONV_WIDTH), :], preferred_element_type=F32)
    o_ref[...] = x_ref[...] + y


def _ffn_kernel(x_ref, nw_ref, wup_ref, cw_ref, wd_ref, o_ref, x_slab, o_slab, up_scr, a_scr):
    tm, d = x_ref.shape
    m = tm // STRIDE
    d_ff = wd_ref.shape[0]
    nc = d_ff // MXU_DIM
    slabs = d // LANES
    moved_start = [STRIDE * m + k * (m + HALO) for k in range(2)]

    for k in range(slabs):
        x_slab[k] = x_ref[:, pl.ds(k * LANES, LANES)]
    x = jnp.concatenate(
        [jnp.concatenate([x_slab[k, pl.ds(v, m, stride=STRIDE), :] for k in range(slabs)], axis=1)
         for v in range(STRIDE)], axis=0)
    h = _rms(x, nw_ref[...]).astype(BF16)

    @pl.when(pl.program_id(1) == 0)
    def _():
        for s in moved_start:
            up_scr[:, pl.ds(s, SUBLANES), :] = jnp.zeros((nc, SUBLANES, 2 * MXU_DIM), F32)

    def up(c):
        r = jnp.dot(h, wup_ref[:, pl.ds(2 * c * MXU_DIM, 2 * MXU_DIM)], preferred_element_type=F32)
        up_scr[c, pl.ds(0, tm), :] = r
        for k, s in enumerate(moved_start):
            v = STRIDE - 2 + k
            up_scr[c, pl.ds(s + 1, m), :] = r[v * m:(v + 1) * m]

    def gate(c):
        cw = cw_ref[:, pl.ds(2 * c * MXU_DIM, 2 * MXU_DIM)]
        z = [up_scr[c, pl.ds(v * m, m), :] for v in range(STRIDE)]
        moved = [up_scr[c, pl.ds(s, m), :] for s in moved_start]
        s1 = moved[1:] + z[:STRIDE - 1]
        s2 = moved + z[:STRIDE - 2]
        for v in range(STRIDE):
            y = cw[0:1, :] * s2[v] + cw[1:2, :] * s1[v] + cw[2:3, :] * z[v]
            g, u = y[:, :MXU_DIM], y[:, MXU_DIM:]
            a_scr[c, pl.ds(v * m, m), :] = (g / (1.0 + jnp.exp(-g)) * u).astype(BF16)

    for c in range(nc):
        up(c)
        if c >= 1:
            gate(c - 1)
    gate(nc - 1)

    split = (nc - 1) * MXU_DIM
    a_head = jnp.concatenate([a_scr[c] for c in range(nc - 1)], axis=1)
    y = x + jnp.dot(a_head, wd_ref[pl.ds(0, split), :], preferred_element_type=F32)
    y = y + jnp.dot(a_scr[nc - 1], wd_ref[pl.ds(split, MXU_DIM), :], preferred_element_type=F32)

    for s in moved_start:
        up_scr[:, pl.ds(s, 1), :] = up_scr[:, pl.ds(s + m, 1), :]

    for k in range(slabs):
        for v in range(STRIDE):
            o_slab[k, pl.ds(v, m, stride=STRIDE), :] = y[v * m:(v + 1) * m, k * LANES:(k + 1) * LANES]
        o_ref[:, pl.ds(k * LANES, LANES)] = o_slab[k]


def _resident(shape):
    return pl.BlockSpec(shape, lambda b, i: (0,) * len(shape), pipeline_mode=pl.Buffered(1))


def _tile(tm, width):
    return pl.BlockSpec((None, tm, width), lambda b, i: (b, i, 0))


def _params(flags=None):
    return pltpu.CompilerParams(dimension_semantics=("arbitrary", "arbitrary"),
                                vmem_limit_bytes=VMEM_LIMIT_BYTES, flags=flags)


def _layer(x, attn_norm_w, w_in, q_norm_w, k_norm_w, sinks, conv_mix_w, attn_out_norm_w,
           conv_out_norm_w, w_out, ffn_norm_w, w_up, ffn_conv_w, w_down):
    bsz, seq, d = x.shape
    d_ff = w_down.shape[0]
    tm = TOKEN_TILE
    assert seq % tm == 0 and tm % BLOCK == 0 and d_ff % MXU_DIM == 0
    grid = (bsz, seq // tm)
    qkv_w = ATTN_WIDTH + 2 * KV_WIDTH

    w_in = w_in.astype(BF16)
    w_qkv, w_conv = w_in[:, :qkv_w], w_in[:, qkv_w:]
    scale = HEAD_DIM ** -0.5
    qkw = jnp.concatenate([jnp.tile(q_norm_w, N_Q_HEADS) * scale, jnp.tile(k_norm_w, N_KV_HEADS)])[None, :]
    seg = jnp.arange(MXU_DIM) // HEAD_DIM
    bd = (seg[:, None] == seg[None, :]).astype(BF16)
    w_out = w_out.astype(BF16)
    nc = d_ff // MXU_DIM

    def pair_chunks(w):
        lead = w.shape[:-1]
        return w.reshape(*lead, 2, nc, MXU_DIM).swapaxes(-3, -2).reshape(*lead, 2 * d_ff)

    w_up = pair_chunks(w_up.astype(BF16))
    cw_up = pair_chunks(ffn_conv_w)
    w_down = w_down.astype(BF16)

    act = jax.ShapeDtypeStruct((bsz, seq, 4 * LANES), BF16)
    qn, kk, vv, cn = pl.pallas_call(
        _in_proj_kernel,
        grid=grid,
        in_specs=[_tile(tm, d), _resident((1, d)), _resident((d, qkv_w)), _resident((d, 3 * CONV_WIDTH)),
                  _resident((1, ATTN_WIDTH + KV_WIDTH)), _resident((MXU_DIM, MXU_DIM)),
                  _resident((CONV_TAPS, CONV_WIDTH)), _resident((1, CONV_WIDTH))],
        out_specs=[_tile(tm, 4 * LANES)] * 4,
        out_shape=[act] * 4,
        scratch_shapes=[pltpu.VMEM((tm + HALO, CONV_WIDTH), F32)],
        compiler_params=_params(),
        name="in_proj",
    )(x, attn_norm_w[None, :], w_qkv, w_conv, qkw, bd, conv_mix_w, conv_out_norm_w[None, :])

    bpt = tm // BLOCK
    prev = pl.BlockSpec((None, BLOCK, 4 * LANES), lambda b, i: (b, jnp.maximum(i * bpt - 1, 0), 0))
    x = pl.pallas_call(
        _attn_kernel,
        grid=grid,
        in_specs=[pl.BlockSpec(memory_space=pltpu.SMEM),
                  _tile(tm, 4 * LANES), _tile(tm, 4 * LANES), prev, _tile(tm, 4 * LANES), prev,
                  _tile(tm, 4 * LANES), _tile(tm, d), _resident((ATTN_WIDTH + CONV_WIDTH, d)),
                  _resident((1, ATTN_WIDTH))],
        out_specs=_tile(tm, d),
        out_shape=jax.ShapeDtypeStruct(x.shape, x.dtype),
        scratch_shapes=[pltpu.VMEM((bpt, 2 * N_KV_HEADS, 2 * BLOCK, 2 * BLOCK), F32),
                        pltpu.VMEM((bpt, 2 * N_KV_HEADS, 2 * BLOCK, 2 * BLOCK), BF16),
                        pltpu.VMEM((bpt, 2 * N_KV_HEADS, 2 * BLOCK, LANES), F32),
                        pltpu.VMEM((tm, ATTN_WIDTH), F32),
                        pltpu.VMEM((2, BLOCK, 2 * BLOCK), jnp.int32)],
        compiler_params=_params(),
        name="attn_out_proj",
    )(sinks, qn, kk, kk, vv, vv, cn, x, w_out, attn_out_norm_w[None, :])

    x = pl.pallas_call(
        _ffn_kernel,
        grid=grid,
        in_specs=[_tile(tm, d), _resident((1, d)), _resident((d, 2 * d_ff)), _resident((CONV_TAPS, 2 * d_ff)),
                  _resident((d_ff, d))],
        out_specs=_tile(tm, d),
        out_shape=jax.ShapeDtypeStruct(x.shape, x.dtype),
        scratch_shapes=[pltpu.VMEM((d // LANES, tm, LANES), F32), pltpu.VMEM((d // LANES, tm, LANES), F32),
                        pltpu.VMEM((nc, tm + 2 * (tm // STRIDE + HALO), 2 * MXU_DIM), F32), pltpu.VMEM((nc, tm, MXU_DIM), BF16)],
        compiler_params=_params(),
        name="conv_ffn",
    )(x, ffn_norm_w[None, :], w_up, cw_up, w_down)
    return x


def kernel(x, attn_norm_w, w_in, q_norm_w, k_norm_w, sinks, conv_mix_w, attn_out_norm_w,
           conv_out_norm_w, w_out, ffn_norm_w, w_up, ffn_conv_w, w_down):
    for l in range(attn_norm_w.shape[0]):
        x = _layer(x, attn_norm_w[l], w_in[l], q_norm_w[l], k_norm_w[l], sinks[l], conv_mix_w[l],
                   attn_out_norm_w[l], conv_out_norm_w[l], w_out[l], ffn_norm_w[l], w_up[l],
                   ffn_conv_w[l], w_down[l])
    return x
```

```python
import jax
import jax.numpy as jnp
from jax import lax
from jax.experimental import pallas as pl
from jax.experimental.pallas import tpu as pltpu

HEAD_DIM = 64
N_Q_HEADS = 8
N_KV_HEADS = 2
WINDOW = 128
BLOCK = 128
ATTN_WIDTH = N_Q_HEADS * HEAD_DIM
KV_WIDTH = N_KV_HEADS * HEAD_DIM
CONV_WIDTH = 512
CONV_TAPS = 3
EPS = 1e-6
NEG_INF = -1e30
LOG2_E = 1.4426950408889634

LANES = 128
SUBLANES = 8
MXU_DIM = 256
VMEM_LIMIT_BYTES = 56 * 1024 * 1024

TOKEN_TILE = 512
HALO = SUBLANES
STRIDE = 4
SOFTMAX_ROWS = 16

F32 = jnp.float32
BF16 = jnp.bfloat16


def _rms(x, w):
    ms = jnp.mean(x * x, axis=-1, keepdims=True)
    return x * lax.rsqrt(ms + EPS) * w


def _causal_conv3(scr, cw, n):
    y = cw[0:1, :] * scr[pl.ds(HALO - 2, n), :]
    y = y + cw[1:2, :] * scr[pl.ds(HALO - 1, n), :]
    return y + cw[2:3, :] * scr[pl.ds(HALO, n), :]


def _in_proj_kernel(x_ref, nw_ref, wqkv_ref, wconv_ref, qkw_ref, bd_ref, cw_ref, cnw_ref,
                    q_ref, kk_ref, vv_ref, cn_ref, cu_scr):
    tm = x_ref.shape[0]
    h = _rms(x_ref[...], nw_ref[...]).astype(BF16)

    qkv = jnp.dot(h, wqkv_ref[...], preferred_element_type=F32)
    qk = qkv[:, :ATTN_WIDTH + KV_WIDTH]
    sq = (qk * qk).astype(BF16)
    bd = bd_ref[...]
    ssq = jnp.concatenate(
        [jnp.dot(sq[:, 0:MXU_DIM], bd, preferred_element_type=F32),
         jnp.dot(sq[:, MXU_DIM:2 * MXU_DIM], bd, preferred_element_type=F32),
         jnp.dot(sq[:, 2 * MXU_DIM:], bd[:KV_WIDTH, :KV_WIDTH], preferred_element_type=F32)],
        axis=1)
    qkn = qk * lax.rsqrt(ssq * (1.0 / HEAD_DIM) + EPS) * qkw_ref[...]
    q_ref[...] = qkn[:, :ATTN_WIDTH].astype(BF16)

    low = lax.broadcasted_iota(jnp.int32, (tm, KV_WIDTH), 1) < HEAD_DIM
    for src, dst in ((qkn[:, ATTN_WIDTH:], kk_ref), (qkv[:, ATTN_WIDTH + KV_WIDTH:], vv_ref)):
        swapped = pltpu.roll(src, HEAD_DIM, axis=1)
        dst[:, 0 * LANES:1 * LANES] = jnp.where(low, src, 0.0).astype(BF16)
        dst[:, 1 * LANES:2 * LANES] = jnp.where(low, 0.0, swapped).astype(BF16)
        dst[:, 2 * LANES:3 * LANES] = jnp.where(low, swapped, 0.0).astype(BF16)
        dst[:, 3 * LANES:4 * LANES] = jnp.where(low, 0.0, src).astype(BF16)

    pc = jnp.dot(h, wconv_ref[...], preferred_element_type=F32)
    bg = pc[:, :CONV_WIDTH]

    @pl.when(pl.program_id(1) == 0)
    def _():
        cu_scr[pl.ds(0, HALO), :] = jnp.zeros((HALO, CONV_WIDTH), F32)

    cu_scr[pl.ds(HALO, tm), :] = pc[:, CONV_WIDTH:2 * CONV_WIDTH] * pc[:, 2 * CONV_WIDTH:]
    conv = bg * _causal_conv3(cu_scr, cw_ref[...], tm)
    cn_ref[...] = _rms(conv, cnw_ref[...]).astype(BF16)
    cu_scr[pl.ds(0, HALO), :] = cu_scr[pl.ds(tm, HALO), :]


def _attn_kernel(sinks_ref, q_ref, kc_ref, kp_ref, vc_ref, vp_ref, cn_ref, x_ref, wo_ref, anw_ref,
                 o_ref, s_scr, p_scr, inv_scr, attn_scr, mask_scr):
    tq = q_ref.shape[0]
    nb = tq // BLOCK
    group_pairs = N_Q_HEADS // N_KV_HEADS // 2
    nt = (((1,), (1,)), ((), ()))

    qi = lax.broadcasted_iota(jnp.int32, (BLOCK, 2 * BLOCK), 0)
    sj = lax.broadcasted_iota(jnp.int32, (BLOCK, 2 * BLOCK), 1)
    rel = qi + BLOCK - sj
    band = (rel >= 0) & (rel < WINDOW)
    band0 = band & ((sj >= BLOCK) | (pl.program_id(1) > 0))
    mask_scr[0] = band0.astype(jnp.int32)
    mask_scr[1] = band.astype(jnp.int32)
    low = lax.broadcasted_iota(jnp.int32, (2 * BLOCK, LANES), 1) < HEAD_DIM

    def window(cur_ref, prev_ref, j, cols):
        if j == 0:
            return jnp.concatenate([prev_ref[:, cols], cur_ref[pl.ds(0, BLOCK), cols]], axis=0)
        return cur_ref[pl.ds((j - 1) * BLOCK, 2 * BLOCK), cols]

    def scores(j):
        rows = pl.ds(j * BLOCK, BLOCK)
        for c in range(N_KV_HEADS):
            qs = jnp.concatenate(
                [q_ref[rows, pl.ds((group_pairs * c + g) * LANES, LANES)] for g in range(group_pairs)], axis=0)
            for e in range(2):
                k = window(kc_ref, kp_ref, j, pl.ds((2 * c + e) * LANES, LANES))
                s_scr[j, 2 * c + e] = lax.dot_general(qs, k, nt, preferred_element_type=F32)

    def softmax(j):
        for ce in range(2 * N_KV_HEADS):
            c, e = divmod(ce, 2)
            for r in range(0, group_pairs * BLOCK, SOFTMAX_ROWS):
                rows = pl.ds(r, SOFTMAX_ROWS)
                keep = mask_scr[min(j, 1), pl.ds(r % BLOCK, SOFTMAX_ROWS), :] > 0
                s = jnp.where(keep, s_scr[j, ce, rows, :], NEG_INF)
                sink = sinks_ref[(N_Q_HEADS // N_KV_HEADS) * c + 2 * (r // BLOCK) + e]
                m = jnp.maximum(jnp.max(s, axis=-1, keepdims=True), sink)
                pe = jnp.exp(s - m)
                denom = jnp.sum(pe, axis=-1, keepdims=True) + jnp.exp(sink - m)
                p_scr[j, ce, rows, :] = pe.astype(BF16)
                inv_scr[j, ce, rows, :] = jnp.broadcast_to(1.0 / denom, (SOFTMAX_ROWS, LANES))

    def values(j):
        rows = pl.ds(j * BLOCK, BLOCK)
        for c in range(N_KV_HEADS):
            acc = None
            for e in range(2):
                v = window(vc_ref, vp_ref, j, pl.ds((2 * c + e) * LANES, LANES))
                pv = jnp.dot(p_scr[j, 2 * c + e], v, preferred_element_type=F32)
                acc = pv if acc is None else acc + pv
            o = acc * jnp.where(low, inv_scr[j, 2 * c], inv_scr[j, 2 * c + 1])
            for g in range(group_pairs):
                attn_scr[rows, pl.ds((group_pairs * c + g) * LANES, LANES)] = o[g * BLOCK:(g + 1) * BLOCK]

    for t in range(nb + 2):
        if t < nb:
            scores(t)
        if 0 <= t - 1 < nb:
            softmax(t - 1)
        if 0 <= t - 2 < nb:
            values(t - 2)

    an = _rms(attn_scr[...], anw_ref[...]).astype(BF16)
    y = jnp.dot(an, wo_ref[pl.ds(0, ATTN_WIDTH), :], preferred_element_type=F32)
    y = y + jnp.dot(cn_ref[...], wo_ref[pl.ds(ATTN_WIDTH, CONV_WIDTH), :], preferred_element_type=F32)
    o_ref[...] = x_ref[...] + y


def _ffn_kernel(x_ref, nw_ref, wup_ref, cw_ref, wd_ref, o_ref, x_slab, o_slab, up_scr, a_scr):
    tm, d = x_ref.shape
    m = tm // STRIDE
    d_ff = wd_ref.shape[0]
    nc = d_ff // MXU_DIM
    slabs = d // LANES
    moved_start = [STRIDE * m + k * (m + HALO) for k in range(2)]

    for k in range(slabs):
        x_slab[k] = x_ref[:, pl.ds(k * LANES, LANES)]
    x = jnp.concatenate(
        [jnp.concatenate([x_slab[k, pl.ds(v, m, stride=STRIDE), :] for k in range(slabs)], axis=1)
         for v in range(STRIDE)], axis=0)
    h = _rms(x, nw_ref[...]).astype(BF16)

    @pl.when(pl.program_id(1) == 0)
    def _():
        for s in moved_start:
            up_scr[:, pl.ds(s, SUBLANES), :] = jnp.zeros((nc, SUBLANES, 2 * MXU_DIM), F32)

    def pair(ref, c):
        return jnp.concatenate([ref[:, pl.ds(half + c * MXU_DIM, MXU_DIM)] for half in (0, d_ff)], axis=1)

    def up(c):
        r = jnp.dot(h, pair(wup_ref, c), preferred_element_type=F32)
        up_scr[c, pl.ds(0, tm), :] = r
        for k, s in enumerate(moved_start):
            v = STRIDE - 2 + k
            up_scr[c, pl.ds(s + 1, m), :] = r[v * m:(v + 1) * m]

    def gate(c):
        cw = pair(cw_ref, c)
        z = [up_scr[c, pl.ds(v * m, m), :] for v in range(STRIDE)]
        moved = [up_scr[c, pl.ds(s, m), :] for s in moved_start]
        s1 = moved[1:] + z[:STRIDE - 1]
        s2 = moved + z[:STRIDE - 2]
        for v in range(STRIDE):
            y = cw[0:1, :] * s2[v] + cw[1:2, :] * s1[v] + cw[2:3, :] * z[v]
            g, u = y[:, :MXU_DIM], y[:, MXU_DIM:]
            a_scr[c, pl.ds(v * m, m), :] = (g / (1.0 + jnp.exp2(g * -LOG2_E)) * u).astype(BF16)

    for c in range(nc):
        up(c)
        if c >= 1:
            gate(c - 1)
    gate(nc - 1)

    split = (nc - 1) * MXU_DIM
    a_head = jnp.concatenate([a_scr[c] for c in range(nc - 1)], axis=1)
    y = x + jnp.dot(a_head, wd_ref[pl.ds(0, split), :], preferred_element_type=F32)
    y = y + jnp.dot(a_scr[nc - 1], wd_ref[pl.ds(split, MXU_DIM), :], preferred_element_type=F32)

    for s in moved_start:
        up_scr[:, pl.ds(s, 1), :] = up_scr[:, pl.ds(s + m, 1), :]

    for k in range(slabs):
        for v in range(STRIDE):
            o_slab[k, pl.ds(v, m, stride=STRIDE), :] = y[v * m:(v + 1) * m, k * LANES:(k + 1) * LANES]
        o_ref[:, pl.ds(k * LANES, LANES)] = o_slab[k]


def _resident(shape):
    return pl.BlockSpec(shape, lambda b, i: (0,) * len(shape), pipeline_mode=pl.Buffered(1))


def _tile(tm, width):
    return pl.BlockSpec((None, tm, width), lambda b, i: (b, i, 0))


def _params(flags=None):
    return pltpu.CompilerParams(dimension_semantics=("arbitrary", "arbitrary"),
                                vmem_limit_bytes=VMEM_LIMIT_BYTES, flags=flags)


def _layer(x, attn_norm_w, w_in, q_norm_w, k_norm_w, sinks, conv_mix_w, attn_out_norm_w,
           conv_out_norm_w, w_out, ffn_norm_w, w_up, ffn_conv_w, w_down):
    bsz, seq, d = x.shape
    d_ff = w_down.shape[0]
    tm = TOKEN_TILE
    assert seq % tm == 0 and tm % BLOCK == 0 and d_ff % MXU_DIM == 0
    grid = (bsz, seq // tm)
    qkv_w = ATTN_WIDTH + 2 * KV_WIDTH

    w_in = w_in.astype(BF16)
    w_qkv, w_conv = w_in[:, :qkv_w], w_in[:, qkv_w:]
    scale = HEAD_DIM ** -0.5
    qkw = jnp.concatenate([jnp.tile(q_norm_w, N_Q_HEADS) * scale, jnp.tile(k_norm_w, N_KV_HEADS)])[None, :]
    seg = jnp.arange(MXU_DIM) // HEAD_DIM
    bd = (seg[:, None] == seg[None, :]).astype(BF16)
    w_out = w_out.astype(BF16)
    nc = d_ff // MXU_DIM
    w_up = w_up.astype(BF16)
    w_down = w_down.astype(BF16)

    act = jax.ShapeDtypeStruct((bsz, seq, 4 * LANES), BF16)
    qn, kk, vv, cn = pl.pallas_call(
        _in_proj_kernel,
        grid=grid,
        in_specs=[_tile(tm, d), _resident((1, d)), _resident((d, qkv_w)), _resident((d, 3 * CONV_WIDTH)),
                  _resident((1, ATTN_WIDTH + KV_WIDTH)), _resident((MXU_DIM, MXU_DIM)),
                  _resident((CONV_TAPS, CONV_WIDTH)), _resident((1, CONV_WIDTH))],
        out_specs=[_tile(tm, 4 * LANES)] * 4,
        out_shape=[act] * 4,
        scratch_shapes=[pltpu.VMEM((tm + HALO, CONV_WIDTH), F32)],
        compiler_params=_params(),
        name="in_proj",
    )(x, attn_norm_w[None, :], w_qkv, w_conv, qkw, bd, conv_mix_w, conv_out_norm_w[None, :])

    bpt = tm // BLOCK
    prev = pl.BlockSpec((None, BLOCK, 4 * LANES), lambda b, i: (b, jnp.maximum(i * bpt - 1, 0), 0))
    x = pl.pallas_call(
        _attn_kernel,
        grid=grid,
        in_specs=[pl.BlockSpec(memory_space=pltpu.SMEM),
                  _tile(tm, 4 * LANES), _tile(tm, 4 * LANES), prev, _tile(tm, 4 * LANES), prev,
                  _tile(tm, 4 * LANES), _tile(tm, d), _resident((ATTN_WIDTH + CONV_WIDTH, d)),
                  _resident((1, ATTN_WIDTH))],
        out_specs=_tile(tm, d),
        out_shape=jax.ShapeDtypeStruct(x.shape, x.dtype),
        scratch_shapes=[pltpu.VMEM((bpt, 2 * N_KV_HEADS, 2 * BLOCK, 2 * BLOCK), F32),
                        pltpu.VMEM((bpt, 2 * N_KV_HEADS, 2 * BLOCK, 2 * BLOCK), BF16),
                        pltpu.VMEM((bpt, 2 * N_KV_HEADS, 2 * BLOCK, LANES), F32),
                        pltpu.VMEM((tm, ATTN_WIDTH), F32),
                        pltpu.VMEM((2, BLOCK, 2 * BLOCK), jnp.int32)],
        compiler_params=_params(),
        name="attn_out_proj",
    )(sinks, qn, kk, kk, vv, vv, cn, x, w_out, attn_out_norm_w[None, :])

    x = pl.pallas_call(
        _ffn_kernel,
        grid=grid,
        in_specs=[_tile(tm, d), _resident((1, d)), _resident((d, 2 * d_ff)), _resident((CONV_TAPS, 2 * d_ff)),
                  _resident((d_ff, d))],
        out_specs=_tile(tm, d),
        out_shape=jax.ShapeDtypeStruct(x.shape, x.dtype),
        scratch_shapes=[pltpu.VMEM((d // LANES, tm, LANES), F32), pltpu.VMEM((d // LANES, tm, LANES), F32),
                        pltpu.VMEM((nc, tm + 2 * (tm // STRIDE + HALO), 2 * MXU_DIM), F32), pltpu.VMEM((nc, tm, MXU_DIM), BF16)],
        compiler_params=_params(),
        name="conv_ffn",
    )(x, ffn_norm_w[None, :], w_up, ffn_conv_w, w_down)
    return x


def kernel(x, attn_norm_w, w_in, q_norm_w, k_norm_w, sinks, conv_mix_w, attn_out_norm_w,
           conv_out_norm_w, w_out, ffn_norm_w, w_up, ffn_conv_w, w_down):
    for l in range(attn_norm_w.shape[0]):
        x = _layer(x, attn_norm_w[l], w_in[l], q_norm_w[l], k_norm_w[l], sinks[l], conv_mix_w[l],
                   attn_out_norm_w[l], conv_out_norm_w[l], w_out[l], ffn_norm_w[l], w_up[l],
                   ffn_conv_w[l], w_down[l])
    return x
```

```python
import jax
import jax.numpy as jnp
from jax import lax
from jax.experimental import pallas as pl
from jax.experimental.pallas import tpu as pltpu

HEAD_DIM = 64
N_Q_HEADS = 8
N_KV_HEADS = 2
WINDOW = 128
BLOCK = 128
ATTN_WIDTH = N_Q_HEADS * HEAD_DIM
KV_WIDTH = N_KV_HEADS * HEAD_DIM
CONV_WIDTH = 512
CONV_TAPS = 3
EPS = 1e-6
NEG_INF = -1e30
LOG2_E = 1.4426950408889634

LANES = 128
SUBLANES = 8
MXU_DIM = 256
VMEM_LIMIT_BYTES = 56 * 1024 * 1024

TOKEN_TILE = 512
HALO = SUBLANES
STRIDE = 4
CONV_ROWS = 32
SOFTMAX_ROWS = 16

F32 = jnp.float32
BF16 = jnp.bfloat16


def _rms(x, w):
    ms = jnp.mean(x * x, axis=-1, keepdims=True)
    return x * lax.rsqrt(ms + EPS) * w


def _causal_conv3(scr, cw, n):
    y = cw[0:1, :] * scr[pl.ds(HALO - 2, n), :]
    y = y + cw[1:2, :] * scr[pl.ds(HALO - 1, n), :]
    return y + cw[2:3, :] * scr[pl.ds(HALO, n), :]


def _in_proj_kernel(x_ref, nw_ref, win_ref, qkw_ref, bd_ref, cw_ref, cnw_ref,
                    q_ref, kk_ref, vv_ref, cn_ref, proj_a, proj_b, cu_scr):
    tm = x_ref.shape[0]
    qkv_w = ATTN_WIDTH + 2 * KV_WIDTH
    step = pl.program_id(1)

    @pl.when(step == 0)
    def _():
        proj_b[...] = jnp.zeros(proj_b.shape, F32)
        cu_scr[pl.ds(0, HALO), :] = jnp.zeros((HALO, CONV_WIDTH), F32)

    def body(fill, drain):
        h = _rms(x_ref[...], nw_ref[...]).astype(BF16)
        fill[...] = jnp.dot(h, win_ref[...], preferred_element_type=F32)

        def zero_word(packed):
            words = pltpu.bitcast(packed, jnp.uint32)
            while words.shape[1] > LANES:
                half = words.shape[1] // 2
                words = words[:, :half] | words[:, half:]
            while words.shape[0] > SUBLANES:
                half = words.shape[0] // 2
                words = words[:half] | words[half:]
            return (words >> 16) >> 16

        def variants(src, rows, dst):
            low = lax.broadcasted_iota(jnp.int32, src.shape, 1) < HEAD_DIM
            swapped = pltpu.roll(src, HEAD_DIM, axis=1)
            out = jnp.concatenate([jnp.where(low, src, 0.0), jnp.where(low, 0.0, swapped),
                                   jnp.where(low, swapped, 0.0), jnp.where(low, 0.0, src)], axis=1).astype(BF16)
            dst[rows, :] = out
            return out

        everything = pl.ds(0, tm)
        qk = drain[:, pl.ds(0, ATTN_WIDTH + KV_WIDTH)]
        sq = (qk * qk).astype(BF16)
        bd = bd_ref[...]
        ssq = jnp.concatenate(
            [jnp.dot(sq[:, 0:MXU_DIM], bd, preferred_element_type=F32),
             jnp.dot(sq[:, MXU_DIM:2 * MXU_DIM], bd, preferred_element_type=F32),
             jnp.dot(sq[:, 2 * MXU_DIM:], bd[:KV_WIDTH, :KV_WIDTH], preferred_element_type=F32)],
            axis=1)
        qkn = qk * lax.rsqrt(ssq * (1.0 / HEAD_DIM) + EPS) * qkw_ref[...]
        q_ref[...] = qkn[:, :ATTN_WIDTH].astype(BF16)
        variants(qkn[:, ATTN_WIDTH:], everything, kk_ref)
        variants(drain[:, pl.ds(ATTN_WIDTH + KV_WIDTH, KV_WIDTH)], everything, vv_ref)

        cw = cw_ref[...]
        token = None
        for r in range(0, tm, CONV_ROWS):
            rows = pl.ds(r, CONV_ROWS)
            bg = drain[rows, pl.ds(qkv_w, CONV_WIDTH)]
            if token is not None:
                zeros = jnp.concatenate([token.astype(F32)] * (CONV_ROWS // SUBLANES), axis=0)
                bg = jnp.concatenate([bg[:, :LANES] + zeros, bg[:, LANES:]], axis=1)
            cu_scr[pl.ds(HALO + r, CONV_ROWS), :] = (drain[rows, pl.ds(qkv_w + CONV_WIDTH, CONV_WIDTH)]
                                                     * drain[rows, pl.ds(qkv_w + 2 * CONV_WIDTH, CONV_WIDTH)])
            y = cw[0:1, :] * cu_scr[pl.ds(HALO + r - 2, CONV_ROWS), :]
            y = y + cw[1:2, :] * cu_scr[pl.ds(HALO + r - 1, CONV_ROWS), :]
            y = y + cw[2:3, :] * cu_scr[pl.ds(HALO + r, CONV_ROWS), :]
            cn = _rms(bg * y, cnw_ref[...]).astype(BF16)
            cn_ref[rows, :] = cn
            token = zero_word(cn)
        cu_scr[pl.ds(0, HALO), :] = cu_scr[pl.ds(tm, HALO), :]

    @pl.when(step % 2 == 0)
    def _():
        body(proj_a, proj_b)

    @pl.when(step % 2 == 1)
    def _():
        body(proj_b, proj_a)


def _attn_kernel(sinks_ref, q_ref, kc_ref, kp_ref, vc_ref, vp_ref, cn_ref, x_ref, wo_ref, anw_ref,
                 o_ref, s_scr, p_scr, inv_scr, attn_scr, mask_scr):
    tq = q_ref.shape[0]
    nb = tq // BLOCK
    group_pairs = N_Q_HEADS // N_KV_HEADS // 2
    nt = (((1,), (1,)), ((), ()))

    qi = lax.broadcasted_iota(jnp.int32, (BLOCK, 2 * BLOCK), 0)
    sj = lax.broadcasted_iota(jnp.int32, (BLOCK, 2 * BLOCK), 1)
    rel = qi + BLOCK - sj
    band = (rel >= 0) & (rel < WINDOW)
    band0 = band & ((sj >= BLOCK) | (pl.program_id(1) > 0))
    mask_scr[0] = band0.astype(jnp.int32)
    mask_scr[1] = band.astype(jnp.int32)
    low = lax.broadcasted_iota(jnp.int32, (2 * BLOCK, LANES), 1) < HEAD_DIM

    def window(cur_ref, prev_ref, j, cols):
        if j == 0:
            return jnp.concatenate([prev_ref[:, cols], cur_ref[pl.ds(0, BLOCK), cols]], axis=0)
        return cur_ref[pl.ds((j - 1) * BLOCK, 2 * BLOCK), cols]

    def scores(j):
        rows = pl.ds(j * BLOCK, BLOCK)
        for c in range(N_KV_HEADS):
            qs = jnp.concatenate(
                [q_ref[rows, pl.ds((group_pairs * c + g) * LANES, LANES)] for g in range(group_pairs)], axis=0)
            for e in range(2):
                k = window(kc_ref, kp_ref, j, pl.ds((2 * c + e) * LANES, LANES))
                s_scr[j, 2 * c + e] = lax.dot_general(qs, k, nt, preferred_element_type=F32)

    def softmax(j):
        for ce in range(2 * N_KV_HEADS):
            c, e = divmod(ce, 2)
            for r in range(0, group_pairs * BLOCK, SOFTMAX_ROWS):
                rows = pl.ds(r, SOFTMAX_ROWS)
                keep = mask_scr[min(j, 1), pl.ds(r % BLOCK, SOFTMAX_ROWS), :] > 0
                s = jnp.where(keep, s_scr[j, ce, rows, :], NEG_INF)
                sink = sinks_ref[(N_Q_HEADS // N_KV_HEADS) * c + 2 * (r // BLOCK) + e]
                m = jnp.maximum(jnp.max(s, axis=-1, keepdims=True), sink)
                pe = jnp.exp(s - m)
                denom = jnp.sum(pe, axis=-1, keepdims=True) + jnp.exp(sink - m)
                p_scr[j, ce, rows, :] = pe.astype(BF16)
                inv_scr[j, ce, rows, :] = jnp.broadcast_to(1.0 / denom, (SOFTMAX_ROWS, LANES))

    def values(j):
        rows = pl.ds(j * BLOCK, BLOCK)
        for c in range(N_KV_HEADS):
            acc = None
            for e in range(2):
                v = window(vc_ref, vp_ref, j, pl.ds((2 * c + e) * LANES, LANES))
                pv = jnp.dot(p_scr[j, 2 * c + e], v, preferred_element_type=F32)
                acc = pv if acc is None else acc + pv
            o = acc * jnp.where(low, inv_scr[j, 2 * c], inv_scr[j, 2 * c + 1])
            for g in range(group_pairs):
                attn_scr[rows, pl.ds((group_pairs * c + g) * LANES, LANES)] = o[g * BLOCK:(g + 1) * BLOCK]

    for t in range(nb + 2):
        if t < nb:
            scores(t)
        if 0 <= t - 1 < nb:
            softmax(t - 1)
        if 0 <= t - 2 < nb:
            values(t - 2)

    an = _rms(attn_scr[...], anw_ref[...]).astype(BF16)
    y = jnp.dot(an, wo_ref[pl.ds(0, ATTN_WIDTH), :], preferred_element_type=F32)
    y = y + jnp.dot(cn_ref[...], wo_ref[pl.ds(ATTN_WIDTH, CONV_WIDTH), :], preferred_element_type=F32)
    o_ref[...] = x_ref[...] + y


def _ffn_kernel(x_ref, nw_ref, wup_ref, cw_ref, wd_ref, o_ref, x_slab, o_slab, up_scr, a_scr):
    tm, d = x_ref.shape
    m = tm // STRIDE
    d_ff = wd_ref.shape[0]
    nc = d_ff // MXU_DIM
    slabs = d // LANES
    moved_start = [STRIDE * m + k * (m + HALO) for k in range(2)]

    for k in range(slabs):
        x_slab[k] = x_ref[:, pl.ds(k * LANES, LANES)]
    x = jnp.concatenate(
        [jnp.concatenate([x_slab[k, pl.ds(v, m, stride=STRIDE), :] for k in range(slabs)], axis=1)
         for v in range(STRIDE)], axis=0)
    h = _rms(x, nw_ref[...]).astype(BF16)

    @pl.when(pl.program_id(1) == 0)
    def _():
        for s in moved_start:
            up_scr[:, pl.ds(s, SUBLANES), :] = jnp.zeros((nc, SUBLANES, 2 * MXU_DIM), F32)

    def pair(ref, c):
        return jnp.concatenate([ref[:, pl.ds(half + c * MXU_DIM, MXU_DIM)] for half in (0, d_ff)], axis=1)

    def up(c):
        r = jnp.dot(h, pair(wup_ref, c), preferred_element_type=F32)
        up_scr[c, pl.ds(0, tm), :] = r
        for k, s in enumerate(moved_start):
            v = STRIDE - 2 + k
            up_scr[c, pl.ds(s + 1, m), :] = r[v * m:(v + 1) * m]

    def gate(c):
        cw = pair(cw_ref, c)
        z = [up_scr[c, pl.ds(v * m, m), :] for v in range(STRIDE)]
        moved = [up_scr[c, pl.ds(s, m), :] for s in moved_start]
        s1 = moved[1:] + z[:STRIDE - 1]
        s2 = moved + z[:STRIDE - 2]
        for v in range(STRIDE):
            y = cw[0:1, :] * s2[v] + cw[1:2, :] * s1[v] + cw[2:3, :] * z[v]
            g, u = y[:, :MXU_DIM], y[:, MXU_DIM:]
            a_scr[c, pl.ds(v * m, m), :] = (g / (1.0 + jnp.exp2(g * -LOG2_E)) * u).astype(BF16)

    for c in range(nc):
        up(c)
        if c >= 1:
            gate(c - 1)
    gate(nc - 1)

    split = (nc - 1) * MXU_DIM
    a_head = jnp.concatenate([a_scr[c] for c in range(nc - 1)], axis=1)
    y = x + jnp.dot(a_head, wd_ref[pl.ds(0, split), :], preferred_element_type=F32)
    y = y + jnp.dot(a_scr[nc - 1], wd_ref[pl.ds(split, MXU_DIM), :], preferred_element_type=F32)

    for s in moved_start:
        up_scr[:, pl.ds(s, 1), :] = up_scr[:, pl.ds(s + m, 1), :]

    for k in range(slabs):
        for v in range(STRIDE):
            o_slab[k, pl.ds(v, m, stride=STRIDE), :] = y[v * m:(v + 1) * m, k * LANES:(k + 1) * LANES]
        o_ref[:, pl.ds(k * LANES, LANES)] = o_slab[k]


def _resident(shape):
    return pl.BlockSpec(shape, lambda b, i: (0,) * len(shape), pipeline_mode=pl.Buffered(1))


def _tile(tm, width):
    return pl.BlockSpec((None, tm, width), lambda b, i: (b, i, 0))


def _params(flags=None):
    return pltpu.CompilerParams(dimension_semantics=("arbitrary", "arbitrary"),
                                vmem_limit_bytes=VMEM_LIMIT_BYTES, flags=flags)


def _layer(x, attn_norm_w, w_in, q_norm_w, k_norm_w, sinks, conv_mix_w, attn_out_norm_w,
           conv_out_norm_w, w_out, ffn_norm_w, w_up, ffn_conv_w, w_down):
    bsz, seq, d = x.shape
    d_ff = w_down.shape[0]
    tm = TOKEN_TILE
    assert seq % tm == 0 and tm % BLOCK == 0 and d_ff % MXU_DIM == 0
    grid = (bsz, seq // tm)
    qkv_w = ATTN_WIDTH + 2 * KV_WIDTH

    w_in = w_in.astype(BF16)
    scale = HEAD_DIM ** -0.5
    qkw = jnp.concatenate([jnp.tile(q_norm_w, N_Q_HEADS) * scale, jnp.tile(k_norm_w, N_KV_HEADS)])[None, :]
    seg = jnp.arange(MXU_DIM) // HEAD_DIM
    bd = (seg[:, None] == seg[None, :]).astype(BF16)
    w_out = w_out.astype(BF16)
    nc = d_ff // MXU_DIM
    w_up = w_up.astype(BF16)
    w_down = w_down.astype(BF16)

    act = jax.ShapeDtypeStruct((bsz, seq, 4 * LANES), BF16)
    n_tiles = seq // tm
    in_w = w_in.shape[1]
    lagged = pl.BlockSpec((None, tm, 4 * LANES), lambda b, i: (b, jnp.maximum(i - 1, 0), 0))
    qn, kk, vv, cn = pl.pallas_call(
        _in_proj_kernel,
        grid=(bsz, n_tiles + 1),
        in_specs=[pl.BlockSpec((None, tm, d), lambda b, i: (b, jnp.minimum(i, n_tiles - 1), 0)),
                  _resident((1, d)), _resident((d, in_w)),
                  _resident((1, ATTN_WIDTH + KV_WIDTH)), _resident((MXU_DIM, MXU_DIM)),
                  _resident((CONV_TAPS, CONV_WIDTH)), _resident((1, CONV_WIDTH))],
        out_specs=[lagged] * 4,
        out_shape=[act] * 4,
        scratch_shapes=[pltpu.VMEM((tm, in_w), F32), pltpu.VMEM((tm, in_w), F32),
                        pltpu.VMEM((tm + HALO, CONV_WIDTH), F32)],
        compiler_params=_params(),
        name="in_proj",
    )(x, attn_norm_w[None, :], w_in, qkw, bd, conv_mix_w, conv_out_norm_w[None, :])

    bpt = tm // BLOCK
    prev = pl.BlockSpec((None, BLOCK, 4 * LANES), lambda b, i: (b, jnp.maximum(i * bpt - 1, 0), 0))
    x = pl.pallas_call(
        _attn_kernel,
        grid=grid,
        in_specs=[pl.BlockSpec(memory_space=pltpu.SMEM),
                  _tile(tm, 4 * LANES), _tile(tm, 4 * LANES), prev, _tile(tm, 4 * LANES), prev,
                  _tile(tm, 4 * LANES), _tile(tm, d), _resident((ATTN_WIDTH + CONV_WIDTH, d)),
                  _resident((1, ATTN_WIDTH))],
        out_specs=_tile(tm, d),
        out_shape=jax.ShapeDtypeStruct(x.shape, x.dtype),
        scratch_shapes=[pltpu.VMEM((bpt, 2 * N_KV_HEADS, 2 * BLOCK, 2 * BLOCK), F32),
                        pltpu.VMEM((bpt, 2 * N_KV_HEADS, 2 * BLOCK, 2 * BLOCK), BF16),
                        pltpu.VMEM((bpt, 2 * N_KV_HEADS, 2 * BLOCK, LANES), F32),
                        pltpu.VMEM((tm, ATTN_WIDTH), F32),
                        pltpu.VMEM((2, BLOCK, 2 * BLOCK), jnp.int32)],
        compiler_params=_params(),
        name="attn_out_proj",
    )(sinks, qn, kk, kk, vv, vv, cn, x, w_out, attn_out_norm_w[None, :])

    x = pl.pallas_call(
        _ffn_kernel,
        grid=grid,
        in_specs=[_tile(tm, d), _resident((1, d)), _resident((d, 2 * d_ff)), _resident((CONV_TAPS, 2 * d_ff)),
                  _resident((d_ff, d))],
        out_specs=_tile(tm, d),
        out_shape=jax.ShapeDtypeStruct(x.shape, x.dtype),
        scratch_shapes=[pltpu.VMEM((d // LANES, tm, LANES), F32), pltpu.VMEM((d // LANES, tm, LANES), F32),
                        pltpu.VMEM((nc, tm + 2 * (tm // STRIDE + HALO), 2 * MXU_DIM), F32), pltpu.VMEM((nc, tm, MXU_DIM), BF16)],
        compiler_params=_params(),
        name="conv_ffn",
    )(x, ffn_norm_w[None, :], w_up, ffn_conv_w, w_down)
    return x


def kernel(x, attn_norm_w, w_in, q_norm_w, k_norm_w, sinks, conv_mix_w, attn_out_norm_w,
           conv_out_norm_w, w_out, ffn_norm_w, w_up, ffn_conv_w, w_down):
    for l in range(attn_norm_w.shape[0]):
        x = _layer(x, attn_norm_w[l], w_in[l], q_norm_w[l], k_norm_w[l], sinks[l], conv_mix_w[l],
                   attn_out_norm_w[l], conv_out_norm_w[l], w_out[l], ffn_norm_w[l], w_up[l],
                   ffn_conv_w[l], w_down[l])
    return x
```

```python
import jax
import jax.numpy as jnp
from jax import lax
from jax.experimental import pallas as pl
from jax.experimental.pallas import tpu as pltpu

HEAD_DIM = 64
N_Q_HEADS = 8
N_KV_HEADS = 2
WINDOW = 128
BLOCK = 128
ATTN_WIDTH = N_Q_HEADS * HEAD_DIM
KV_WIDTH = N_KV_HEADS * HEAD_DIM
CONV_WIDTH = 512
CONV_TAPS = 3
EPS = 1e-6
NEG_INF = -1e30
LOG2_E = 1.4426950408889634

LANES = 128
SUBLANES = 8
MXU_DIM = 256
VMEM_LIMIT_BYTES = 56 * 1024 * 1024

TOKEN_TILE = 512
HALO = SUBLANES
STRIDE = 4
GATE_ROWS = 64
CONV_ROWS = 32
SOFTMAX_ROWS = 16

F32 = jnp.float32
BF16 = jnp.bfloat16


def _rms(x, w):
    ms = jnp.mean(x * x, axis=-1, keepdims=True)
    return x * lax.rsqrt(ms + EPS) * w


def _causal_conv3(scr, cw, n):
    y = cw[0:1, :] * scr[pl.ds(HALO - 2, n), :]
    y = y + cw[1:2, :] * scr[pl.ds(HALO - 1, n), :]
    return y + cw[2:3, :] * scr[pl.ds(HALO, n), :]


def _zero_tile(packed):
    words = pltpu.bitcast(packed, jnp.uint32)
    while words.shape[1] > LANES:
        half = words.shape[1] // 2
        words = words[:, :half] | words[:, half:]
    while words.shape[0] > SUBLANES:
        half = words.shape[0] // 2
        words = words[:half] | words[half:]
    return ((words >> 16) >> 16).astype(F32)


def _chained(x, zero_tile, width=LANES):
    if zero_tile is None:
        return x
    zeros = jnp.concatenate([jnp.concatenate([zero_tile] * (x.shape[0] // SUBLANES), axis=0)] * (width // LANES),
                            axis=1)
    return jnp.concatenate([x[:, :width] + zeros, x[:, width:]], axis=1)


def _in_proj_kernel(x_ref, nw_ref, win_ref, qkw_ref, bd_ref, cw_ref, cnw_ref,
                    q_ref, kk_ref, vv_ref, cn_ref, proj_a, proj_b, cu_scr):
    tm = x_ref.shape[0]
    qkv_w = ATTN_WIDTH + 2 * KV_WIDTH
    step = pl.program_id(1)

    @pl.when(step == 0)
    def _():
        proj_b[...] = jnp.zeros(proj_b.shape, F32)
        cu_scr[pl.ds(0, HALO), :] = jnp.zeros((HALO, CONV_WIDTH), F32)

    def body(fill, drain):
        h = _rms(x_ref[...], nw_ref[...]).astype(BF16)
        fill[...] = jnp.dot(h, win_ref[...], preferred_element_type=F32)

        def variants(src, rows, dst):
            low = lax.broadcasted_iota(jnp.int32, src.shape, 1) < HEAD_DIM
            swapped = pltpu.roll(src, HEAD_DIM, axis=1)
            out = jnp.concatenate([jnp.where(low, src, 0.0), jnp.where(low, 0.0, swapped),
                                   jnp.where(low, swapped, 0.0), jnp.where(low, 0.0, src)], axis=1).astype(BF16)
            dst[rows, :] = out
            return out

        everything = pl.ds(0, tm)
        qk = drain[:, pl.ds(0, ATTN_WIDTH + KV_WIDTH)]
        sq = (qk * qk).astype(BF16)
        bd = bd_ref[...]
        ssq = jnp.concatenate(
            [jnp.dot(sq[:, 0:MXU_DIM], bd, preferred_element_type=F32),
             jnp.dot(sq[:, MXU_DIM:2 * MXU_DIM], bd, preferred_element_type=F32),
             jnp.dot(sq[:, 2 * MXU_DIM:], bd[:KV_WIDTH, :KV_WIDTH], preferred_element_type=F32)],
            axis=1)
        qkn = qk * lax.rsqrt(ssq * (1.0 / HEAD_DIM) + EPS) * qkw_ref[...]
        q_ref[...] = qkn[:, :ATTN_WIDTH].astype(BF16)
        variants(qkn[:, ATTN_WIDTH:], everything, kk_ref)
        variants(drain[:, pl.ds(ATTN_WIDTH + KV_WIDTH, KV_WIDTH)], everything, vv_ref)

        cw = cw_ref[...]
        token = None
        for r in range(0, tm, CONV_ROWS):
            rows = pl.ds(r, CONV_ROWS)
            bg = _chained(drain[rows, pl.ds(qkv_w, CONV_WIDTH)], token)
            cu_scr[pl.ds(HALO + r, CONV_ROWS), :] = (drain[rows, pl.ds(qkv_w + CONV_WIDTH, CONV_WIDTH)]
                                                     * drain[rows, pl.ds(qkv_w + 2 * CONV_WIDTH, CONV_WIDTH)])
            y = cw[0:1, :] * cu_scr[pl.ds(HALO + r - 2, CONV_ROWS), :]
            y = y + cw[1:2, :] * cu_scr[pl.ds(HALO + r - 1, CONV_ROWS), :]
            y = y + cw[2:3, :] * cu_scr[pl.ds(HALO + r, CONV_ROWS), :]
            cn = _rms(bg * y, cnw_ref[...]).astype(BF16)
            cn_ref[rows, :] = cn
            token = _zero_tile(cn)
        cu_scr[pl.ds(0, HALO), :] = cu_scr[pl.ds(tm, HALO), :]

    @pl.when(step % 2 == 0)
    def _():
        body(proj_a, proj_b)

    @pl.when(step % 2 == 1)
    def _():
        body(proj_b, proj_a)


def _attn_kernel(sinks_ref, q_ref, kc_ref, kp_ref, vc_ref, vp_ref, cn_ref, x_ref, wo_ref, anw_ref,
                 o_ref, s_scr, p_scr, inv_scr, attn_scr, mask_scr):
    tq = q_ref.shape[0]
    nb = tq // BLOCK
    group_pairs = N_Q_HEADS // N_KV_HEADS // 2
    nt = (((1,), (1,)), ((), ()))

    qi = lax.broadcasted_iota(jnp.int32, (BLOCK, 2 * BLOCK), 0)
    sj = lax.broadcasted_iota(jnp.int32, (BLOCK, 2 * BLOCK), 1)
    rel = qi + BLOCK - sj
    band = (rel >= 0) & (rel < WINDOW)
    band0 = band & ((sj >= BLOCK) | (pl.program_id(1) > 0))
    mask_scr[0] = band0.astype(jnp.int32)
    mask_scr[1] = band.astype(jnp.int32)
    low = lax.broadcasted_iota(jnp.int32, (2 * BLOCK, LANES), 1) < HEAD_DIM

    def window(cur_ref, prev_ref, j, cols):
        if j == 0:
            return jnp.concatenate([prev_ref[:, cols], cur_ref[pl.ds(0, BLOCK), cols]], axis=0)
        return cur_ref[pl.ds((j - 1) * BLOCK, 2 * BLOCK), cols]

    def scores(j):
        rows = pl.ds(j * BLOCK, BLOCK)
        for c in range(N_KV_HEADS):
            qs = jnp.concatenate(
                [q_ref[rows, pl.ds((group_pairs * c + g) * LANES, LANES)] for g in range(group_pairs)], axis=0)
            for e in range(2):
                k = window(kc_ref, kp_ref, j, pl.ds((2 * c + e) * LANES, LANES))
                s_scr[j, 2 * c + e] = lax.dot_general(qs, k, nt, preferred_element_type=F32)

    def softmax(j):
        for ce in range(2 * N_KV_HEADS):
            c, e = divmod(ce, 2)
            for r in range(0, group_pairs * BLOCK, SOFTMAX_ROWS):
                rows = pl.ds(r, SOFTMAX_ROWS)
                keep = mask_scr[min(j, 1), pl.ds(r % BLOCK, SOFTMAX_ROWS), :] > 0
                s = jnp.where(keep, s_scr[j, ce, rows, :], NEG_INF)
                sink = sinks_ref[(N_Q_HEADS // N_KV_HEADS) * c + 2 * (r // BLOCK) + e]
                m = jnp.maximum(jnp.max(s, axis=-1, keepdims=True), sink)
                pe = jnp.exp(s - m)
                denom = jnp.sum(pe, axis=-1, keepdims=True) + jnp.exp(sink - m)
                p_scr[j, ce, rows, :] = pe.astype(BF16)
                inv_scr[j, ce, rows, :] = jnp.broadcast_to(1.0 / denom, (SOFTMAX_ROWS, LANES))

    def values(j):
        rows = pl.ds(j * BLOCK, BLOCK)
        for c in range(N_KV_HEADS):
            acc = None
            for e in range(2):
                v = window(vc_ref, vp_ref, j, pl.ds((2 * c + e) * LANES, LANES))
                pv = jnp.dot(p_scr[j, 2 * c + e], v, preferred_element_type=F32)
                acc = pv if acc is None else acc + pv
            o = acc * jnp.where(low, inv_scr[j, 2 * c], inv_scr[j, 2 * c + 1])
            for g in range(group_pairs):
                attn_scr[rows, pl.ds((group_pairs * c + g) * LANES, LANES)] = o[g * BLOCK:(g + 1) * BLOCK]

    for t in range(nb + 2):
        if t < nb:
            scores(t)
        if 0 <= t - 1 < nb:
            softmax(t - 1)
        if 0 <= t - 2 < nb:
            values(t - 2)

    an = _rms(attn_scr[...], anw_ref[...]).astype(BF16)
    y = jnp.dot(an, wo_ref[pl.ds(0, ATTN_WIDTH), :], preferred_element_type=F32)
    y = y + jnp.dot(cn_ref[...], wo_ref[pl.ds(ATTN_WIDTH, CONV_WIDTH), :], preferred_element_type=F32)
    o_ref[...] = x_ref[...] + y


def _ffn_kernel(x_ref, nw_ref, wup_ref, cw_ref, wd_ref, o_ref, x_slab, o_slab, up_scr, a_scr):
    tm, d = x_ref.shape
    m = tm // STRIDE
    d_ff = wd_ref.shape[0]
    nc = d_ff // MXU_DIM
    slabs = d // LANES
    moved_start = [STRIDE * m + k * (m + HALO) for k in range(2)]

    for k in range(slabs):
        x_slab[k] = x_ref[:, pl.ds(k * LANES, LANES)]
    x = jnp.concatenate(
        [jnp.concatenate([x_slab[k, pl.ds(v, m, stride=STRIDE), :] for k in range(slabs)], axis=1)
         for v in range(STRIDE)], axis=0)
    h = _rms(x, nw_ref[...]).astype(BF16)

    @pl.when(pl.program_id(1) == 0)
    def _():
        for s in moved_start:
            up_scr[:, pl.ds(s, SUBLANES), :] = jnp.zeros((nc, SUBLANES, 2 * MXU_DIM), F32)

    def pair(ref, c):
        return jnp.concatenate([ref[:, pl.ds(half + c * MXU_DIM, MXU_DIM)] for half in (0, d_ff)], axis=1)

    def up(c):
        r = jnp.dot(h, pair(wup_ref, c), preferred_element_type=F32)
        up_scr[c, pl.ds(0, tm), :] = r
        for k, s in enumerate(moved_start):
            v = STRIDE - 2 + k
            up_scr[c, pl.ds(s + 1, m), :] = r[v * m:(v + 1) * m]

    def gate(c, token):
        cw = pair(cw_ref, c)
        first = [v * m for v in range(STRIDE)]
        s1 = moved_start[1:] + first[:STRIDE - 1]
        s2 = moved_start + first[:STRIDE - 2]
        for v in range(STRIDE):
            for r in range(0, m, GATE_ROWS):
                z = _chained(up_scr[c, pl.ds(first[v] + r, GATE_ROWS), :], token, MXU_DIM)
                y = (cw[0:1, :] * up_scr[c, pl.ds(s2[v] + r, GATE_ROWS), :]
                     + cw[1:2, :] * up_scr[c, pl.ds(s1[v] + r, GATE_ROWS), :] + cw[2:3, :] * z)
                g, u = y[:, :MXU_DIM], y[:, MXU_DIM:]
                a = (g / (1.0 + jnp.exp2(g * -LOG2_E)) * u).astype(BF16)
                a_scr[c, pl.ds(v * m + r, GATE_ROWS), :] = a
                token = _zero_tile(a)
        return token

    token = None
    for c in range(nc):
        up(c)
        if c >= 1:
            token = gate(c - 1, token)
    gate(nc - 1, token)

    split = (nc - 1) * MXU_DIM
    a_head = jnp.concatenate([a_scr[c] for c in range(nc - 1)], axis=1)
    y = x + jnp.dot(a_head, wd_ref[pl.ds(0, split), :], preferred_element_type=F32)
    y = y + jnp.dot(a_scr[nc - 1], wd_ref[pl.ds(split, MXU_DIM), :], preferred_element_type=F32)

    for s in moved_start:
        up_scr[:, pl.ds(s, 1), :] = up_scr[:, pl.ds(s + m, 1), :]

    for k in range(slabs):
        for v in range(STRIDE):
            o_slab[k, pl.ds(v, m, stride=STRIDE), :] = y[v * m:(v + 1) * m, k * LANES:(k + 1) * LANES]
        o_ref[:, pl.ds(k * LANES, LANES)] = o_slab[k]


def _resident(shape):
    return pl.BlockSpec(shape, lambda b, i: (0,) * len(shape), pipeline_mode=pl.Buffered(1))


def _tile(tm, width):
    return pl.BlockSpec((None, tm, width), lambda b, i: (b, i, 0))


def _params(flags=None):
    return pltpu.CompilerParams(dimension_semantics=("arbitrary", "arbitrary"),
                                vmem_limit_bytes=VMEM_LIMIT_BYTES, flags=flags)


def _layer(x, attn_norm_w, w_in, q_norm_w, k_norm_w, sinks, conv_mix_w, attn_out_norm_w,
           conv_out_norm_w, w_out, ffn_norm_w, w_up, ffn_conv_w, w_down):
    bsz, seq, d = x.shape
    d_ff = w_down.shape[0]
    tm = TOKEN_TILE
    assert seq % tm == 0 and tm % BLOCK == 0 and d_ff % MXU_DIM == 0
    grid = (bsz, seq // tm)
    qkv_w = ATTN_WIDTH + 2 * KV_WIDTH

    w_in = w_in.astype(BF16)
    scale = HEAD_DIM ** -0.5
    qkw = jnp.concatenate([jnp.tile(q_norm_w, N_Q_HEADS) * scale, jnp.tile(k_norm_w, N_KV_HEADS)])[None, :]
    seg = jnp.arange(MXU_DIM) // HEAD_DIM
    bd = (seg[:, None] == seg[None, :]).astype(BF16)
    w_out = w_out.astype(BF16)
    nc = d_ff // MXU_DIM
    w_up = w_up.astype(BF16)
    w_down = w_down.astype(BF16)

    act = jax.ShapeDtypeStruct((bsz, seq, 4 * LANES), BF16)
    n_tiles = seq // tm
    in_w = w_in.shape[1]
    lagged = pl.BlockSpec((None, tm, 4 * LANES), lambda b, i: (b, jnp.maximum(i - 1, 0), 0))
    qn, kk, vv, cn = pl.pallas_call(
        _in_proj_kernel,
        grid=(bsz, n_tiles + 1),
        in_specs=[pl.BlockSpec((None, tm, d), lambda b, i: (b, jnp.minimum(i, n_tiles - 1), 0)),
                  _resident((1, d)), _resident((d, in_w)),
                  _resident((1, ATTN_WIDTH + KV_WIDTH)), _resident((MXU_DIM, MXU_DIM)),
                  _resident((CONV_TAPS, CONV_WIDTH)), _resident((1, CONV_WIDTH))],
        out_specs=[lagged] * 4,
        out_shape=[act] * 4,
        scratch_shapes=[pltpu.VMEM((tm, in_w), F32), pltpu.VMEM((tm, in_w), F32),
                        pltpu.VMEM((tm + HALO, CONV_WIDTH), F32)],
        compiler_params=_params(),
        name="in_proj",
    )(x, attn_norm_w[None, :], w_in, qkw, bd, conv_mix_w, conv_out_norm_w[None, :])

    bpt = tm // BLOCK
    prev = pl.BlockSpec((None, BLOCK, 4 * LANES), lambda b, i: (b, jnp.maximum(i * bpt - 1, 0), 0))
    x = pl.pallas_call(
        _attn_kernel,
        grid=grid,
        in_specs=[pl.BlockSpec(memory_space=pltpu.SMEM),
                  _tile(tm, 4 * LANES), _tile(tm, 4 * LANES), prev, _tile(tm, 4 * LANES), prev,
                  _tile(tm, 4 * LANES), _tile(tm, d), _resident((ATTN_WIDTH + CONV_WIDTH, d)),
                  _resident((1, ATTN_WIDTH))],
        out_specs=_tile(tm, d),
        out_shape=jax.ShapeDtypeStruct(x.shape, x.dtype),
        scratch_shapes=[pltpu.VMEM((bpt, 2 * N_KV_HEADS, 2 * BLOCK, 2 * BLOCK), F32),
                        pltpu.VMEM((bpt, 2 * N_KV_HEADS, 2 * BLOCK, 2 * BLOCK), BF16),
                        pltpu.VMEM((bpt, 2 * N_KV_HEADS, 2 * BLOCK, LANES), F32),
                        pltpu.VMEM((tm, ATTN_WIDTH), F32),
                        pltpu.VMEM((2, BLOCK, 2 * BLOCK), jnp.int32)],
        compiler_params=_params(),
        name="attn_out_proj",
    )(sinks, qn, kk, kk, vv, vv, cn, x, w_out, attn_out_norm_w[None, :])

    x = pl.pallas_call(
        _ffn_kernel,
        grid=grid,
        in_specs=[_tile(tm, d), _resident((1, d)), _resident((d, 2 * d_ff)), _resident((CONV_TAPS, 2 * d_ff)),
                  _resident((d_ff, d))],
        out_specs=_tile(tm, d),
        out_shape=jax.ShapeDtypeStruct(x.shape, x.dtype),
        scratch_shapes=[pltpu.VMEM((d // LANES, tm, LANES), F32), pltpu.VMEM((d // LANES, tm, LANES), F32),
                        pltpu.VMEM((nc, tm + 2 * (tm // STRIDE + HALO), 2 * MXU_DIM), F32), pltpu.VMEM((nc, tm, MXU_DIM), BF16)],
        compiler_params=_params(),
        name="conv_ffn",
    )(x, ffn_norm_w[None, :], w_up, ffn_conv_w, w_down)
    return x


def kernel(x, attn_norm_w, w_in, q_norm_w, k_norm_w, sinks, conv_mix_w, attn_out_norm_w,
           conv_out_norm_w, w_out, ffn_norm_w, w_up, ffn_conv_w, w_down):
    for l in range(attn_norm_w.shape[0]):
        x = _layer(x, attn_norm_w[l], w_in[l], q_norm_w[l], k_norm_w[l], sinks[l], conv_mix_w[l],
                   attn_out_norm_w[l], conv_out_norm_w[l], w_out[l], ffn_norm_w[l], w_up[l],
                   ffn_conv_w[l], w_down[l])
    return x
```

```python
import jax
import jax.numpy as jnp
from jax import lax
from jax.experimental import pallas as pl
from jax.experimental.pallas import tpu as pltpu

HEAD_DIM = 64
N_Q_HEADS = 8
N_KV_HEADS = 2
WINDOW = 128
BLOCK = 128
ATTN_WIDTH = N_Q_HEADS * HEAD_DIM
KV_WIDTH = N_KV_HEADS * HEAD_DIM
CONV_WIDTH = 512
CONV_TAPS = 3
EPS = 1e-6
NEG_INF = -1e30
LOG2_E = 1.4426950408889634

LANES = 128
SUBLANES = 8
MXU_DIM = 256
VMEM_LIMIT_BYTES = 56 * 1024 * 1024

TOKEN_TILE = 512
HALO = SUBLANES
STRIDE = 4
GATE_ROWS = 64
PIECE_ROWS = 32
SOFTMAX_ROWS = 16

F32 = jnp.float32
BF16 = jnp.bfloat16


def _rms(x, w):
    ms = jnp.mean(x * x, axis=-1, keepdims=True)
    return x * lax.rsqrt(ms + EPS) * w


def _causal_conv3(scr, cw, n):
    y = cw[0:1, :] * scr[pl.ds(HALO - 2, n), :]
    y = y + cw[1:2, :] * scr[pl.ds(HALO - 1, n), :]
    return y + cw[2:3, :] * scr[pl.ds(HALO, n), :]


def _zero_tile(packed):
    words = pltpu.bitcast(packed, jnp.uint32)
    while words.shape[1] > LANES:
        half = words.shape[1] // 2
        words = words[:, :half] | words[:, half:]
    while words.shape[0] > SUBLANES:
        half = words.shape[0] // 2
        words = words[:half] | words[half:]
    return ((words >> 16) >> 16).astype(F32)


def _chained(x, zero_tile, width=LANES):
    if zero_tile is None:
        return x
    zeros = jnp.concatenate([jnp.concatenate([zero_tile] * (x.shape[0] // SUBLANES), axis=0)] * (width // LANES),
                            axis=1)
    if width == x.shape[1]:
        return x + zeros
    return jnp.concatenate([x[:, :width] + zeros, x[:, width:]], axis=1)


def _in_proj_kernel(x_ref, nw_ref, win_ref, qkw_ref, bd_ref, cw_ref, cnw_ref,
                    q_ref, kk_ref, vv_ref, cn_ref, proj_a, proj_b, cu_scr):
    tm = x_ref.shape[0]
    qkv_w = ATTN_WIDTH + 2 * KV_WIDTH
    step = pl.program_id(1)

    @pl.when(step == 0)
    def _():
        proj_b[...] = jnp.zeros(proj_b.shape, F32)
        cu_scr[pl.ds(0, HALO), :] = jnp.zeros((HALO, CONV_WIDTH), F32)

    def body(fill, drain):
        def variants(src, rows, dst):
            low = lax.broadcasted_iota(jnp.int32, src.shape, 1) < HEAD_DIM
            swapped = pltpu.roll(src, HEAD_DIM, axis=1)
            out = jnp.concatenate([jnp.where(low, src, 0.0), jnp.where(low, 0.0, swapped),
                                   jnp.where(low, swapped, 0.0), jnp.where(low, 0.0, src)], axis=1).astype(BF16)
            dst[rows, :] = out
            return out

        h = _rms(x_ref[...], nw_ref[...]).astype(BF16)
        fill[...] = jnp.dot(h, win_ref[...], preferred_element_type=F32)

        everything = pl.ds(0, tm)
        qk = drain[:, pl.ds(0, ATTN_WIDTH + KV_WIDTH)]
        sq = (qk * qk).astype(BF16)
        bd = bd_ref[...]
        ssq = jnp.concatenate(
            [jnp.dot(sq[:, 0:MXU_DIM], bd, preferred_element_type=F32),
             jnp.dot(sq[:, MXU_DIM:2 * MXU_DIM], bd, preferred_element_type=F32),
             jnp.dot(sq[:, 2 * MXU_DIM:], bd[:KV_WIDTH, :KV_WIDTH], preferred_element_type=F32)],
            axis=1)
        qkn = qk * lax.rsqrt(ssq * (1.0 / HEAD_DIM) + EPS) * qkw_ref[...]
        q_ref[...] = qkn[:, :ATTN_WIDTH].astype(BF16)
        variants(qkn[:, ATTN_WIDTH:], everything, kk_ref)
        variants(drain[:, pl.ds(ATTN_WIDTH + KV_WIDTH, KV_WIDTH)], everything, vv_ref)

        cw = cw_ref[...]
        token = None
        for r in range(0, tm, PIECE_ROWS):
            rows = pl.ds(r, PIECE_ROWS)
            bg = _chained(drain[rows, pl.ds(qkv_w, CONV_WIDTH)], token)
            cu_scr[pl.ds(HALO + r, PIECE_ROWS), :] = (drain[rows, pl.ds(qkv_w + CONV_WIDTH, CONV_WIDTH)]
                                                      * drain[rows, pl.ds(qkv_w + 2 * CONV_WIDTH, CONV_WIDTH)])
            y = cw[0:1, :] * cu_scr[pl.ds(HALO + r - 2, PIECE_ROWS), :]
            y = y + cw[1:2, :] * cu_scr[pl.ds(HALO + r - 1, PIECE_ROWS), :]
            y = y + cw[2:3, :] * cu_scr[pl.ds(HALO + r, PIECE_ROWS), :]
            cn = _rms(bg * y, cnw_ref[...]).astype(BF16)
            cn_ref[rows, :] = cn
            token = _zero_tile(cn)
        cu_scr[pl.ds(0, HALO), :] = cu_scr[pl.ds(tm, HALO), :]

    @pl.when(step % 2 == 0)
    def _():
        body(proj_a, proj_b)

    @pl.when(step % 2 == 1)
    def _():
        body(proj_b, proj_a)


def _attn_kernel(sinks_ref, q_ref, kc_ref, kp_ref, vc_ref, vp_ref, cn_ref, x_ref, wo_ref, anw_ref,
                 o_ref, s_scr, p_scr, inv_scr, mask_scr, attn_a, attn_b):
    tq = q_ref.shape[0]
    nb = tq // BLOCK
    group_pairs = N_Q_HEADS // N_KV_HEADS // 2
    nt = (((1,), (1,)), ((), ()))

    qi = lax.broadcasted_iota(jnp.int32, (BLOCK, 2 * BLOCK), 0)
    sj = lax.broadcasted_iota(jnp.int32, (BLOCK, 2 * BLOCK), 1)
    rel = qi + BLOCK - sj
    band = (rel >= 0) & (rel < WINDOW)
    step = pl.program_id(1)
    band0 = band & ((sj >= BLOCK) | (step > 0))
    mask_scr[0] = band0.astype(jnp.int32)
    mask_scr[1] = band.astype(jnp.int32)
    low = lax.broadcasted_iota(jnp.int32, (2 * BLOCK, LANES), 1) < HEAD_DIM

    def window(cur_ref, prev_ref, j, cols):
        if j == 0:
            return jnp.concatenate([prev_ref[:, cols], cur_ref[pl.ds(0, BLOCK), cols]], axis=0)
        return cur_ref[pl.ds((j - 1) * BLOCK, 2 * BLOCK), cols]

    def scores(j):
        rows = pl.ds(j * BLOCK, BLOCK)
        for c in range(N_KV_HEADS):
            qs = jnp.concatenate(
                [q_ref[rows, pl.ds((group_pairs * c + g) * LANES, LANES)] for g in range(group_pairs)], axis=0)
            for e in range(2):
                k = window(kc_ref, kp_ref, j, pl.ds((2 * c + e) * LANES, LANES))
                s_scr[j, 2 * c + e] = lax.dot_general(qs, k, nt, preferred_element_type=F32)

    def softmax(j):
        for ce in range(2 * N_KV_HEADS):
            c, e = divmod(ce, 2)
            for r in range(0, group_pairs * BLOCK, SOFTMAX_ROWS):
                rows = pl.ds(r, SOFTMAX_ROWS)
                keep = mask_scr[min(j, 1), pl.ds(r % BLOCK, SOFTMAX_ROWS), :] > 0
                s = jnp.where(keep, s_scr[j, ce, rows, :], NEG_INF)
                sink = sinks_ref[(N_Q_HEADS // N_KV_HEADS) * c + 2 * (r // BLOCK) + e]
                m = jnp.maximum(jnp.max(s, axis=-1, keepdims=True), sink)
                pe = jnp.exp(s - m)
                denom = jnp.sum(pe, axis=-1, keepdims=True) + jnp.exp(sink - m)
                p_scr[j, ce, rows, :] = pe.astype(BF16)
                inv_scr[j, ce, rows, :] = jnp.broadcast_to(1.0 / denom, (SOFTMAX_ROWS, LANES))

    def values(j, attn_scr):
        rows = pl.ds(j * BLOCK, BLOCK)
        for c in range(N_KV_HEADS):
            acc = None
            for e in range(2):
                v = window(vc_ref, vp_ref, j, pl.ds((2 * c + e) * LANES, LANES))
                pv = jnp.dot(p_scr[j, 2 * c + e], v, preferred_element_type=F32)
                acc = pv if acc is None else acc + pv
            o = acc * jnp.where(low, inv_scr[j, 2 * c], inv_scr[j, 2 * c + 1])
            for g in range(group_pairs):
                attn_scr[rows, pl.ds((group_pairs * c + g) * LANES, LANES)] = o[g * BLOCK:(g + 1) * BLOCK]

    @pl.when(step == 0)
    def _():
        attn_b[...] = jnp.zeros(attn_b.shape, F32)

    def body(fill, drain):
        for t in range(nb + 2):
            if t < nb:
                scores(t)
            if 0 <= t - 1 < nb:
                softmax(t - 1)
            if 0 <= t - 2 < nb:
                values(t - 2, fill)

        an = _rms(drain[...], anw_ref[...]).astype(BF16)
        y = jnp.dot(an, wo_ref[pl.ds(0, ATTN_WIDTH), :], preferred_element_type=F32)
        y = y + jnp.dot(cn_ref[...], wo_ref[pl.ds(ATTN_WIDTH, CONV_WIDTH), :], preferred_element_type=F32)
        o_ref[...] = x_ref[...] + y

    @pl.when(step % 2 == 0)
    def _():
        body(attn_a, attn_b)

    @pl.when(step % 2 == 1)
    def _():
        body(attn_b, attn_a)


def _ffn_kernel(x_ref, nw_ref, wup_ref, cw_ref, wd_ref, o_ref, x_slab, o_slab, up_scr, a_scr):
    tm, d = x_ref.shape
    m = tm // STRIDE
    d_ff = wd_ref.shape[0]
    nc = d_ff // MXU_DIM
    slabs = d // LANES
    moved_start = [STRIDE * m + k * (m + HALO) for k in range(2)]

    for k in range(slabs):
        x_slab[k] = x_ref[:, pl.ds(k * LANES, LANES)]
    x = jnp.concatenate(
        [jnp.concatenate([x_slab[k, pl.ds(v, m, stride=STRIDE), :] for k in range(slabs)], axis=1)
         for v in range(STRIDE)], axis=0)
    h = _rms(x, nw_ref[...]).astype(BF16)

    @pl.when(pl.program_id(1) == 0)
    def _():
        for s in moved_start:
            up_scr[:, pl.ds(s, SUBLANES), :] = jnp.zeros((nc, SUBLANES, 2 * MXU_DIM), F32)

    def pair(ref, c):
        return jnp.concatenate([ref[:, pl.ds(half + c * MXU_DIM, MXU_DIM)] for half in (0, d_ff)], axis=1)

    def up(c):
        r = jnp.dot(h, pair(wup_ref, c), preferred_element_type=F32)
        up_scr[c, pl.ds(0, tm), :] = r
        for k, s in enumerate(moved_start):
            v = STRIDE - 2 + k
            up_scr[c, pl.ds(s + 1, m), :] = r[v * m:(v + 1) * m]

    def gate(c, token):
        cw = pair(cw_ref, c)
        first = [v * m for v in range(STRIDE)]
        s1 = moved_start[1:] + first[:STRIDE - 1]
        s2 = moved_start + first[:STRIDE - 2]
        for v in range(STRIDE):
            for r in range(0, m, GATE_ROWS):
                z = _chained(up_scr[c, pl.ds(first[v] + r, GATE_ROWS), :], token, MXU_DIM)
                y = (cw[0:1, :] * up_scr[c, pl.ds(s2[v] + r, GATE_ROWS), :]
                     + cw[1:2, :] * up_scr[c, pl.ds(s1[v] + r, GATE_ROWS), :] + cw[2:3, :] * z)
                g, u = y[:, :MXU_DIM], y[:, MXU_DIM:]
                a = (g / (1.0 + jnp.exp2(g * -LOG2_E)) * u).astype(BF16)
                a_scr[c, pl.ds(v * m + r, GATE_ROWS), :] = a
                token = _zero_tile(a)
        return token

    token = None
    for c in range(nc):
        up(c)
        if c >= 1:
            token = gate(c - 1, token)
    gate(nc - 1, token)

    split = (nc - 1) * MXU_DIM
    a_head = jnp.concatenate([a_scr[c] for c in range(nc - 1)], axis=1)
    y = x + jnp.dot(a_head, wd_ref[pl.ds(0, split), :], preferred_element_type=F32)
    y = y + jnp.dot(a_scr[nc - 1], wd_ref[pl.ds(split, MXU_DIM), :], preferred_element_type=F32)

    for s in moved_start:
        up_scr[:, pl.ds(s, 1), :] = up_scr[:, pl.ds(s + m, 1), :]

    for k in range(slabs):
        for v in range(STRIDE):
            o_slab[k, pl.ds(v, m, stride=STRIDE), :] = y[v * m:(v + 1) * m, k * LANES:(k + 1) * LANES]
        o_ref[:, pl.ds(k * LANES, LANES)] = o_slab[k]


def _resident(shape):
    return pl.BlockSpec(shape, lambda b, i: (0,) * len(shape), pipeline_mode=pl.Buffered(1))


def _tile(tm, width):
    return pl.BlockSpec((None, tm, width), lambda b, i: (b, i, 0))


def _params(flags=None):
    return pltpu.CompilerParams(dimension_semantics=("arbitrary", "arbitrary"),
                                vmem_limit_bytes=VMEM_LIMIT_BYTES, flags=flags)


def _layer(x, attn_norm_w, w_in, q_norm_w, k_norm_w, sinks, conv_mix_w, attn_out_norm_w,
           conv_out_norm_w, w_out, ffn_norm_w, w_up, ffn_conv_w, w_down):
    bsz, seq, d = x.shape
    d_ff = w_down.shape[0]
    tm = TOKEN_TILE
    assert seq % tm == 0 and tm % BLOCK == 0 and d_ff % MXU_DIM == 0
    grid = (bsz, seq // tm)
    qkv_w = ATTN_WIDTH + 2 * KV_WIDTH

    w_in = w_in.astype(BF16)
    scale = HEAD_DIM ** -0.5
    qkw = jnp.concatenate([jnp.tile(q_norm_w, N_Q_HEADS) * scale, jnp.tile(k_norm_w, N_KV_HEADS)])[None, :]
    seg = jnp.arange(MXU_DIM) // HEAD_DIM
    bd = (seg[:, None] == seg[None, :]).astype(BF16)
    w_out = w_out.astype(BF16)
    nc = d_ff // MXU_DIM
    w_up = w_up.astype(BF16)
    w_down = w_down.astype(BF16)

    act = jax.ShapeDtypeStruct((bsz, seq, 4 * LANES), BF16)
    n_tiles = seq // tm
    in_w = w_in.shape[1]

    def this_tile(width):
        return pl.BlockSpec((None, tm, width), lambda b, i: (b, jnp.minimum(i, n_tiles - 1), 0))

    def last_tile(width):
        return pl.BlockSpec((None, tm, width), lambda b, i: (b, jnp.maximum(i - 1, 0), 0))

    qn, kk, vv, cn = pl.pallas_call(
        _in_proj_kernel,
        grid=(bsz, n_tiles + 1),
        in_specs=[this_tile(d), _resident((1, d)), _resident((d, in_w)),
                  _resident((1, ATTN_WIDTH + KV_WIDTH)), _resident((MXU_DIM, MXU_DIM)),
                  _resident((CONV_TAPS, CONV_WIDTH)), _resident((1, CONV_WIDTH))],
        out_specs=[last_tile(4 * LANES)] * 4,
        out_shape=[act] * 4,
        scratch_shapes=[pltpu.VMEM((tm, in_w), F32), pltpu.VMEM((tm, in_w), F32),
                        pltpu.VMEM((tm + HALO, CONV_WIDTH), F32)],
        compiler_params=_params(),
        name="in_proj",
    )(x, attn_norm_w[None, :], w_in, qkw, bd, conv_mix_w, conv_out_norm_w[None, :])

    bpt = tm // BLOCK
    prev = pl.BlockSpec((None, BLOCK, 4 * LANES),
                        lambda b, i: (b, jnp.maximum(jnp.minimum(i, n_tiles - 1) * bpt - 1, 0), 0))
    x = pl.pallas_call(
        _attn_kernel,
        grid=(bsz, n_tiles + 1),
        in_specs=[pl.BlockSpec(memory_space=pltpu.SMEM),
                  this_tile(4 * LANES), this_tile(4 * LANES), prev, this_tile(4 * LANES), prev,
                  last_tile(4 * LANES), last_tile(d), _resident((ATTN_WIDTH + CONV_WIDTH, d)),
                  _resident((1, ATTN_WIDTH))],
        out_specs=last_tile(d),
        out_shape=jax.ShapeDtypeStruct(x.shape, x.dtype),
        scratch_shapes=[pltpu.VMEM((bpt, 2 * N_KV_HEADS, 2 * BLOCK, 2 * BLOCK), F32),
                        pltpu.VMEM((bpt, 2 * N_KV_HEADS, 2 * BLOCK, 2 * BLOCK), BF16),
                        pltpu.VMEM((bpt, 2 * N_KV_HEADS, 2 * BLOCK, LANES), F32),
                        pltpu.VMEM((2, BLOCK, 2 * BLOCK), jnp.int32),
                        pltpu.VMEM((tm, ATTN_WIDTH), F32), pltpu.VMEM((tm, ATTN_WIDTH), F32)],
        compiler_params=_params(),
        name="attn_out_proj",
    )(sinks, qn, kk, kk, vv, vv, cn, x, w_out, attn_out_norm_w[None, :])

    x = pl.pallas_call(
        _ffn_kernel,
        grid=grid,
        in_specs=[_tile(tm, d), _resident((1, d)), _resident((d, 2 * d_ff)), _resident((CONV_TAPS, 2 * d_ff)),
                  _resident((d_ff, d))],
        out_specs=_tile(tm, d),
        out_shape=jax.ShapeDtypeStruct(x.shape, x.dtype),
        scratch_shapes=[pltpu.VMEM((d // LANES, tm, LANES), F32), pltpu.VMEM((d // LANES, tm, LANES), F32),
                        pltpu.VMEM((nc, tm + 2 * (tm // STRIDE + HALO), 2 * MXU_DIM), F32), pltpu.VMEM((nc, tm, MXU_DIM), BF16)],
        compiler_params=_params(),
        name="conv_ffn",
    )(x, ffn_norm_w[None, :], w_up, ffn_conv_w, w_down)
    return x


def kernel(x, attn_norm_w, w_in, q_norm_w, k_norm_w, sinks, conv_mix_w, attn_out_norm_w,
           conv_out_norm_w, w_out, ffn_norm_w, w_up, ffn_conv_w, w_down):
    for l in range(attn_norm_w.shape[0]):
        x = _layer(x, attn_norm_w[l], w_in[l], q_norm_w[l], k_norm_w[l], sinks[l], conv_mix_w[l],
                   attn_out_norm_w[l], conv_out_norm_w[l], w_out[l], ffn_norm_w[l], w_up[l],
                   ffn_conv_w[l], w_down[l])
    return x
```

```python
import jax
import jax.numpy as jnp
from jax import lax
from jax.experimental import pallas as pl
from jax.experimental.pallas import tpu as pltpu

HEAD_DIM = 64
N_Q_HEADS = 8
N_KV_HEADS = 2
WINDOW = 128
BLOCK = 128
ATTN_WIDTH = N_Q_HEADS * HEAD_DIM
KV_WIDTH = N_KV_HEADS * HEAD_DIM
CONV_WIDTH = 512
CONV_TAPS = 3
EPS = 1e-6
NEG_INF = -1e30
LOG2_E = 1.4426950408889634

LANES = 128
SUBLANES = 8
MXU_DIM = 256
VMEM_LIMIT_BYTES = 56 * 1024 * 1024

TOKEN_TILE = 512
HALO = SUBLANES
STRIDE = 4
GATE_ROWS = 64
PIECE_ROWS = 32
SOFTMAX_ROWS = 16

F32 = jnp.float32
BF16 = jnp.bfloat16


def _rms(x, w):
    ms = jnp.mean(x * x, axis=-1, keepdims=True)
    return x * lax.rsqrt(ms + EPS) * w


def _zero_tile(packed):
    words = pltpu.bitcast(packed, jnp.uint32)
    while words.shape[1] > LANES:
        half = words.shape[1] // 2
        words = words[:, :half] | words[:, half:]
    while words.shape[0] > SUBLANES:
        half = words.shape[0] // 2
        words = words[:half] | words[half:]
    return ((words >> 16) >> 16).astype(F32)


def _chained(x, zero_tile, width=LANES):
    if zero_tile is None:
        return x
    zeros = jnp.concatenate([jnp.concatenate([zero_tile] * (x.shape[0] // SUBLANES), axis=0)] * (width // LANES),
                            axis=1)
    if width == x.shape[1]:
        return x + zeros
    return jnp.concatenate([x[:, :width] + zeros, x[:, width:]], axis=1)


def _in_proj_kernel(x_ref, nw_ref, win_ref, qkw_ref, bd_ref, cw_ref, cnw_ref,
                    q_ref, kk_ref, vv_ref, cn_ref, proj_a, proj_b, cu_scr):
    tm = x_ref.shape[0]
    qkv_w = ATTN_WIDTH + 2 * KV_WIDTH
    step = pl.program_id(1)

    @pl.when(step == 0)
    def _():
        proj_b[...] = jnp.zeros(proj_b.shape, F32)
        cu_scr[pl.ds(0, HALO), :] = jnp.zeros((HALO, CONV_WIDTH), F32)

    def body(fill, drain):
        def variants(src, dst):
            low = lax.broadcasted_iota(jnp.int32, src.shape, 1) < HEAD_DIM
            swapped = pltpu.roll(src, HEAD_DIM, axis=1)
            dst[...] = jnp.concatenate([jnp.where(low, src, 0.0), jnp.where(low, 0.0, swapped),
                                        jnp.where(low, swapped, 0.0), jnp.where(low, 0.0, src)],
                                       axis=1).astype(BF16)

        h = _rms(x_ref[...], nw_ref[...]).astype(BF16)
        fill[...] = jnp.dot(h, win_ref[...], preferred_element_type=F32)

        qk = drain[:, pl.ds(0, ATTN_WIDTH + KV_WIDTH)]
        sq = (qk * qk).astype(BF16)
        bd = bd_ref[...]
        ssq = jnp.concatenate(
            [jnp.dot(sq[:, 0:MXU_DIM], bd, preferred_element_type=F32),
             jnp.dot(sq[:, MXU_DIM:2 * MXU_DIM], bd, preferred_element_type=F32),
             jnp.dot(sq[:, 2 * MXU_DIM:], bd[:KV_WIDTH, :KV_WIDTH], preferred_element_type=F32)],
            axis=1)
        qkn = qk * lax.rsqrt(ssq * (1.0 / HEAD_DIM) + EPS) * qkw_ref[...]
        q_ref[...] = qkn[:, :ATTN_WIDTH].astype(BF16)
        variants(qkn[:, ATTN_WIDTH:], kk_ref)
        variants(drain[:, pl.ds(ATTN_WIDTH + KV_WIDTH, KV_WIDTH)], vv_ref)

        cw = cw_ref[...]
        token = None
        for r in range(0, tm, PIECE_ROWS):
            rows = pl.ds(r, PIECE_ROWS)
            bg = _chained(drain[rows, pl.ds(qkv_w, CONV_WIDTH)], token)
            cu_scr[pl.ds(HALO + r, PIECE_ROWS), :] = (drain[rows, pl.ds(qkv_w + CONV_WIDTH, CONV_WIDTH)]
                                                      * drain[rows, pl.ds(qkv_w + 2 * CONV_WIDTH, CONV_WIDTH)])
            y = cw[0:1, :] * cu_scr[pl.ds(HALO + r - 2, PIECE_ROWS), :]
            y = y + cw[1:2, :] * cu_scr[pl.ds(HALO + r - 1, PIECE_ROWS), :]
            y = y + cw[2:3, :] * cu_scr[pl.ds(HALO + r, PIECE_ROWS), :]
            cn = _rms(bg * y, cnw_ref[...]).astype(BF16)
            cn_ref[rows, :] = cn
            token = _zero_tile(cn)
        cu_scr[pl.ds(0, HALO), :] = cu_scr[pl.ds(tm, HALO), :]

    @pl.when(step % 2 == 0)
    def _():
        body(proj_a, proj_b)

    @pl.when(step % 2 == 1)
    def _():
        body(proj_b, proj_a)


def _attn_kernel(sinks_ref, q_ref, kc_ref, kp_ref, vc_ref, vp_ref, cn_ref, x_ref, wo_ref, anw_ref,
                 o_ref, s_scr, p_scr, inv_scr, attn_scr, mask_scr):
    tq = q_ref.shape[0]
    nb = tq // BLOCK
    group_pairs = N_Q_HEADS // N_KV_HEADS // 2
    nt = (((1,), (1,)), ((), ()))

    qi = lax.broadcasted_iota(jnp.int32, (BLOCK, 2 * BLOCK), 0)
    sj = lax.broadcasted_iota(jnp.int32, (BLOCK, 2 * BLOCK), 1)
    rel = qi + BLOCK - sj
    band = (rel >= 0) & (rel < WINDOW)
    band0 = band & ((sj >= BLOCK) | (pl.program_id(1) > 0))
    mask_scr[0] = band0.astype(jnp.int32)
    mask_scr[1] = band.astype(jnp.int32)
    low = lax.broadcasted_iota(jnp.int32, (2 * BLOCK, LANES), 1) < HEAD_DIM

    def window(cur_ref, prev_ref, j, cols):
        if j == 0:
            return jnp.concatenate([prev_ref[:, cols], cur_ref[pl.ds(0, BLOCK), cols]], axis=0)
        return cur_ref[pl.ds((j - 1) * BLOCK, 2 * BLOCK), cols]

    def scores(j):
        rows = pl.ds(j * BLOCK, BLOCK)
        for c in range(N_KV_HEADS):
            qs = jnp.concatenate(
                [q_ref[rows, pl.ds((group_pairs * c + g) * LANES, LANES)] for g in range(group_pairs)], axis=0)
            for e in range(2):
                k = window(kc_ref, kp_ref, j, pl.ds((2 * c + e) * LANES, LANES))
                s_scr[j, 2 * c + e] = lax.dot_general(qs, k, nt, preferred_element_type=F32)

    def softmax(j):
        for q0 in range(0, BLOCK, SOFTMAX_ROWS):
            keep = mask_scr[min(j, 1), pl.ds(q0, SOFTMAX_ROWS), :] > 0
            for ce, g in [(ce, g) for ce in range(2 * N_KV_HEADS) for g in range(group_pairs)]:
                c, e = divmod(ce, 2)
                r = g * BLOCK + q0
                rows = pl.ds(r, SOFTMAX_ROWS)
                s = jnp.where(keep, s_scr[j, ce, rows, :], NEG_INF)
                sink = sinks_ref[(N_Q_HEADS // N_KV_HEADS) * c + 2 * (r // BLOCK) + e]
                m = jnp.maximum(jnp.max(s, axis=-1, keepdims=True), sink)
                pe = jnp.exp(s - m)
                denom = jnp.sum(pe, axis=-1, keepdims=True) + jnp.exp(sink - m)
                p_scr[j, ce, rows, :] = pe.astype(BF16)
                inv_scr[j, ce, rows, :] = jnp.broadcast_to(1.0 / denom, (SOFTMAX_ROWS, LANES))

    def values(j):
        rows = pl.ds(j * BLOCK, BLOCK)
        for c in range(N_KV_HEADS):
            acc = None
            for e in range(2):
                v = window(vc_ref, vp_ref, j, pl.ds((2 * c + e) * LANES, LANES))
                pv = jnp.dot(p_scr[j, 2 * c + e], v, preferred_element_type=F32)
                acc = pv if acc is None else acc + pv
            o = acc * jnp.where(low, inv_scr[j, 2 * c], inv_scr[j, 2 * c + 1])
            for g in range(group_pairs):
                attn_scr[rows, pl.ds((group_pairs * c + g) * LANES, LANES)] = o[g * BLOCK:(g + 1) * BLOCK]

    for t in range(nb + 2):
        if t < nb:
            scores(t)
        if 0 <= t - 1 < nb:
            softmax(t - 1)
        if 0 <= t - 2 < nb:
            values(t - 2)

    an = _rms(attn_scr[...], anw_ref[...]).astype(BF16)
    y = jnp.dot(an, wo_ref[pl.ds(0, ATTN_WIDTH), :], preferred_element_type=F32)
    y = y + jnp.dot(cn_ref[...], wo_ref[pl.ds(ATTN_WIDTH, CONV_WIDTH), :], preferred_element_type=F32)
    y = x_ref[...] + y
    for k in range(o_ref.shape[0]):
        o_ref[k] = y[:, k * LANES:(k + 1) * LANES]


def _ffn_kernel(x_ref, nw_ref, wup_ref, cw_ref, wd_ref, o_ref, o_slab, up_scr, a_scr):
    slabs, tm, _ = x_ref.shape
    m = tm // STRIDE
    d_ff = wd_ref.shape[0]
    nc = d_ff // MXU_DIM
    moved_start = [STRIDE * m + k * (m + HALO) for k in range(2)]

    x = jnp.concatenate(
        [jnp.concatenate([x_ref[k, pl.ds(v, m, stride=STRIDE), :] for k in range(slabs)], axis=1)
         for v in range(STRIDE)], axis=0)
    h = _rms(x, nw_ref[...]).astype(BF16)

    @pl.when(pl.program_id(1) == 0)
    def _():
        for s in moved_start:
            up_scr[:, pl.ds(s, SUBLANES), :] = jnp.zeros((nc, SUBLANES, 2 * MXU_DIM), F32)

    def pair(ref, c):
        return jnp.concatenate([ref[:, pl.ds(half + c * MXU_DIM, MXU_DIM)] for half in (0, d_ff)], axis=1)

    def up(c):
        r = jnp.dot(h, pair(wup_ref, c), preferred_element_type=F32)
        up_scr[c, pl.ds(0, tm), :] = r
        for k, s in enumerate(moved_start):
            v = STRIDE - 2 + k
            up_scr[c, pl.ds(s + 1, m), :] = r[v * m:(v + 1) * m]

    def gate(c, token):
        cw = pair(cw_ref, c)
        first = [v * m for v in range(STRIDE)]
        s1 = moved_start[1:] + first[:STRIDE - 1]
        s2 = moved_start + first[:STRIDE - 2]
        for v in range(STRIDE):
            for r in range(0, m, GATE_ROWS):
                z = _chained(up_scr[c, pl.ds(first[v] + r, GATE_ROWS), :], token, MXU_DIM)
                y = (cw[0:1, :] * up_scr[c, pl.ds(s2[v] + r, GATE_ROWS), :]
                     + cw[1:2, :] * up_scr[c, pl.ds(s1[v] + r, GATE_ROWS), :] + cw[2:3, :] * z)
                g, u = y[:, :MXU_DIM], y[:, MXU_DIM:]
                a = (g / (1.0 + jnp.exp2(g * -LOG2_E)) * u).astype(BF16)
                a_scr[c, pl.ds(v * m + r, GATE_ROWS), :] = a
                token = _zero_tile(a)
        return token

    token = None
    for c in range(nc):
        up(c)
        if c >= 1:
            token = gate(c - 1, token)
    gate(nc - 1, token)

    split = (nc - 1) * MXU_DIM
    a_head = jnp.concatenate([a_scr[c] for c in range(nc - 1)], axis=1)
    y = x + jnp.dot(a_head, wd_ref[pl.ds(0, split), :], preferred_element_type=F32)
    y = y + jnp.dot(a_scr[nc - 1], wd_ref[pl.ds(split, MXU_DIM), :], preferred_element_type=F32)

    for s in moved_start:
        up_scr[:, pl.ds(s, 1), :] = up_scr[:, pl.ds(s + m, 1), :]

    for k in range(slabs):
        for v in range(STRIDE):
            o_slab[k, pl.ds(v, m, stride=STRIDE), :] = y[v * m:(v + 1) * m, k * LANES:(k + 1) * LANES]
        o_ref[:, pl.ds(k * LANES, LANES)] = o_slab[k]


def _resident(shape):
    return pl.BlockSpec(shape, lambda b, i: (0,) * len(shape), pipeline_mode=pl.Buffered(1))


def _tile(tm, width):
    return pl.BlockSpec((None, tm, width), lambda b, i: (b, i, 0))


def _params(vmem_limit_bytes=VMEM_LIMIT_BYTES):
    return pltpu.CompilerParams(dimension_semantics=("arbitrary", "arbitrary"), vmem_limit_bytes=vmem_limit_bytes)


def _layer(x, attn_norm_w, w_in, q_norm_w, k_norm_w, sinks, conv_mix_w, attn_out_norm_w,
           conv_out_norm_w, w_out, ffn_norm_w, w_up, ffn_conv_w, w_down):
    bsz, seq, d = x.shape
    d_ff = w_down.shape[0]
    tm = TOKEN_TILE
    assert seq % tm == 0 and tm % BLOCK == 0 and d_ff % MXU_DIM == 0
    n_tiles = seq // tm
    nc = d_ff // MXU_DIM

    w_in = w_in.astype(BF16)
    scale = HEAD_DIM ** -0.5
    qkw = jnp.concatenate([jnp.tile(q_norm_w, N_Q_HEADS) * scale, jnp.tile(k_norm_w, N_KV_HEADS)])[None, :]
    seg = jnp.arange(MXU_DIM) // HEAD_DIM
    bd = (seg[:, None] == seg[None, :]).astype(BF16)
    w_out = w_out.astype(BF16)
    w_up = w_up.astype(BF16)
    w_down = w_down.astype(BF16)

    def this_tile(width):
        return pl.BlockSpec((None, tm, width), lambda b, i: (b, jnp.minimum(i, n_tiles - 1), 0))

    def last_tile(width):
        return pl.BlockSpec((None, tm, width), lambda b, i: (b, jnp.maximum(i - 1, 0), 0))

    act = jax.ShapeDtypeStruct((bsz, seq, 4 * LANES), BF16)
    in_w = w_in.shape[1]
    qn, kk, vv, cn = pl.pallas_call(
        _in_proj_kernel,
        grid=(bsz, n_tiles + 1),
        in_specs=[this_tile(d), _resident((1, d)), _resident((d, in_w)),
                  _resident((1, ATTN_WIDTH + KV_WIDTH)), _resident((MXU_DIM, MXU_DIM)),
                  _resident((CONV_TAPS, CONV_WIDTH)), _resident((1, CONV_WIDTH))],
        out_specs=[last_tile(4 * LANES)] * 4,
        out_shape=[act] * 4,
        scratch_shapes=[pltpu.VMEM((tm, in_w), F32), pltpu.VMEM((tm, in_w), F32),
                        pltpu.VMEM((tm + HALO, CONV_WIDTH), F32)],
        compiler_params=_params(),
        name="in_proj",
    )(x, attn_norm_w[None, :], w_in, qkw, bd, conv_mix_w, conv_out_norm_w[None, :])

    bpt = tm // BLOCK
    slab_tile = pl.BlockSpec((None, d // LANES, tm, LANES), lambda b, i: (b, 0, i, 0))
    prev = pl.BlockSpec((None, BLOCK, 4 * LANES), lambda b, i: (b, jnp.maximum(i * bpt - 1, 0), 0))
    x = pl.pallas_call(
        _attn_kernel,
        grid=(bsz, n_tiles),
        in_specs=[pl.BlockSpec(memory_space=pltpu.SMEM),
                  _tile(tm, 4 * LANES), _tile(tm, 4 * LANES), prev, _tile(tm, 4 * LANES), prev,
                  _tile(tm, 4 * LANES), _tile(tm, d), _resident((ATTN_WIDTH + CONV_WIDTH, d)),
                  _resident((1, ATTN_WIDTH))],
        out_specs=slab_tile,
        out_shape=jax.ShapeDtypeStruct((bsz, d // LANES, seq, LANES), x.dtype),
        scratch_shapes=[pltpu.VMEM((bpt, 2 * N_KV_HEADS, 2 * BLOCK, 2 * BLOCK), F32),
                        pltpu.VMEM((bpt, 2 * N_KV_HEADS, 2 * BLOCK, 2 * BLOCK), BF16),
                        pltpu.VMEM((bpt, 2 * N_KV_HEADS, 2 * BLOCK, LANES), F32),
                        pltpu.VMEM((tm, ATTN_WIDTH), F32),
                        pltpu.VMEM((2, BLOCK, 2 * BLOCK), jnp.int32)],
        compiler_params=_params(),
        name="attn_out_proj",
    )(sinks, qn, kk, kk, vv, vv, cn, x, w_out, attn_out_norm_w[None, :])

    x = pl.pallas_call(
        _ffn_kernel,
        grid=(bsz, n_tiles),
        in_specs=[slab_tile, _resident((1, d)), _resident((d, 2 * d_ff)), _resident((CONV_TAPS, 2 * d_ff)),
                  _resident((d_ff, d))],
        out_specs=_tile(tm, d),
        out_shape=jax.ShapeDtypeStruct((bsz, seq, d), x.dtype),
        scratch_shapes=[pltpu.VMEM((d // LANES, tm, LANES), F32),
                        pltpu.VMEM((nc, tm + 2 * (tm // STRIDE + HALO), 2 * MXU_DIM), F32),
                        pltpu.VMEM((nc, tm, MXU_DIM), BF16)],
        compiler_params=_params(),
        name="conv_ffn",
    )(x, ffn_norm_w[None, :], w_up, ffn_conv_w, w_down)
    return x


def kernel(x, attn_norm_w, w_in, q_norm_w, k_norm_w, sinks, conv_mix_w, attn_out_norm_w,
           conv_out_norm_w, w_out, ffn_norm_w, w_up, ffn_conv_w, w_down):
    for l in range(attn_norm_w.shape[0]):
        x = _layer(x, attn_norm_w[l], w_in[l], q_norm_w[l], k_norm_w[l], sinks[l], conv_mix_w[l],
                   attn_out_norm_w[l], conv_out_norm_w[l], w_out[l], ffn_norm_w[l], w_up[l],
                   ffn_conv_w[l], w_down[l])
    return x
```

```python
import jax
import jax.numpy as jnp
from jax import lax
from jax.experimental import pallas as pl
from jax.experimental.pallas import tpu as pltpu

HEAD_DIM = 64
N_Q_HEADS = 8
N_KV_HEADS = 2
WINDOW = 128
BLOCK = 128
ATTN_WIDTH = N_Q_HEADS * HEAD_DIM
KV_WIDTH = N_KV_HEADS * HEAD_DIM
CONV_WIDTH = 512
CONV_TAPS = 3
EPS = 1e-6
NEG_INF = -1e30
LOG2_E = 1.4426950408889634

LANES = 128
SUBLANES = 8
BF16_ROWS = 16
MXU_DIM = 256
VMEM_LIMIT_BYTES = 56 * 1024 * 1024

WEIGHT_CHUNK_BYTES = 704 * 1024
TOKEN_TILE = 512
HALO = SUBLANES
STRIDE = 4
GATE_ROWS = 64
PIECE_ROWS = 32
SOFTMAX_ROWS = 16

F32 = jnp.float32
BF16 = jnp.bfloat16


def _rms(x, w):
    ms = jnp.mean(x * x, axis=-1, keepdims=True)
    return x * lax.rsqrt(ms + EPS) * w


def _zero_tile(packed):
    words = pltpu.bitcast(packed, jnp.uint32)
    while words.shape[1] > LANES:
        half = words.shape[1] // 2
        words = words[:, :half] | words[:, half:]
    while words.shape[0] > SUBLANES:
        half = words.shape[0] // 2
        words = words[:half] | words[half:]
    return ((words >> 16) >> 16).astype(F32)


def _chained(x, zero_tile, width=LANES):
    if zero_tile is None:
        return x
    zeros = jnp.concatenate([jnp.concatenate([zero_tile] * (x.shape[0] // SUBLANES), axis=0)] * (width // LANES),
                            axis=1)
    if width == x.shape[1]:
        return x + zeros
    return jnp.concatenate([x[:, :width] + zeros, x[:, width:]], axis=1)


def _fetch_bf16(w_hbm, w_bf16, stage, sems, rows):
    n = w_hbm.shape[0] // rows

    def chunk(c):
        return pltpu.make_async_copy(w_hbm.at[pl.ds(c * rows, rows), :], stage.at[c % 2], sems.at[c % 2])

    for c in range(min(2, n)):
        chunk(c).start()
    for c in range(n):
        chunk(c).wait()
        w_bf16[pl.ds(c * rows, rows), :] = stage[c % 2].astype(BF16)
        if c + 2 < n:
            chunk(c + 2).start()


def _first_step():
    return (pl.program_id(0) == 0) & (pl.program_id(1) == 0)


def _in_proj_kernel(x_ref, nw_ref, win_hbm, qkw_ref, bd_ref, cw_ref, cnw_ref,
                    q_ref, kk_ref, vv_ref, cn_ref, proj_a, proj_b, cu_scr, win_ref, w_stage, w_sems):
    tm = x_ref.shape[0]
    qkv_w = ATTN_WIDTH + 2 * KV_WIDTH
    step = pl.program_id(1)

    @pl.when(_first_step())
    def _():
        _fetch_bf16(win_hbm, win_ref, w_stage, w_sems, w_stage.shape[1])

    @pl.when(step == 0)
    def _():
        proj_b[...] = jnp.zeros(proj_b.shape, F32)
        cu_scr[pl.ds(0, HALO), :] = jnp.zeros((HALO, CONV_WIDTH), F32)

    def body(fill, drain):
        def variants(src, dst):
            low = lax.broadcasted_iota(jnp.int32, src.shape, 1) < HEAD_DIM
            swapped = pltpu.roll(src, HEAD_DIM, axis=1)
            dst[...] = jnp.concatenate([jnp.where(low, src, 0.0), jnp.where(low, 0.0, swapped),
                                        jnp.where(low, swapped, 0.0), jnp.where(low, 0.0, src)],
                                       axis=1).astype(BF16)

        h = _rms(x_ref[...], nw_ref[...]).astype(BF16)
        fill[...] = jnp.dot(h, win_ref[...], preferred_element_type=F32)

        qk = drain[:, pl.ds(0, ATTN_WIDTH + KV_WIDTH)]
        sq = (qk * qk).astype(BF16)
        bd = bd_ref[...]
        ssq = jnp.concatenate(
            [jnp.dot(sq[:, 0:MXU_DIM], bd, preferred_element_type=F32),
             jnp.dot(sq[:, MXU_DIM:2 * MXU_DIM], bd, preferred_element_type=F32),
             jnp.dot(sq[:, 2 * MXU_DIM:], bd[:KV_WIDTH, :KV_WIDTH], preferred_element_type=F32)],
            axis=1)
        qkn = qk * lax.rsqrt(ssq * (1.0 / HEAD_DIM) + EPS) * qkw_ref[...]
        q_ref[...] = qkn[:, :ATTN_WIDTH].astype(BF16)
        variants(qkn[:, ATTN_WIDTH:], kk_ref)
        variants(drain[:, pl.ds(ATTN_WIDTH + KV_WIDTH, KV_WIDTH)], vv_ref)

        cw = cw_ref[...]
        token = None
        for r in range(0, tm, PIECE_ROWS):
            rows = pl.ds(r, PIECE_ROWS)
            bg = _chained(drain[rows, pl.ds(qkv_w, CONV_WIDTH)], token)
            cu_scr[pl.ds(HALO + r, PIECE_ROWS), :] = (drain[rows, pl.ds(qkv_w + CONV_WIDTH, CONV_WIDTH)]
                                                      * drain[rows, pl.ds(qkv_w + 2 * CONV_WIDTH, CONV_WIDTH)])
            y = cw[0:1, :] * cu_scr[pl.ds(HALO + r - 2, PIECE_ROWS), :]
            y = y + cw[1:2, :] * cu_scr[pl.ds(HALO + r - 1, PIECE_ROWS), :]
            y = y + cw[2:3, :] * cu_scr[pl.ds(HALO + r, PIECE_ROWS), :]
            cn = _rms(bg * y, cnw_ref[...]).astype(BF16)
            cn_ref[rows, :] = cn
            token = _zero_tile(cn)
        cu_scr[pl.ds(0, HALO), :] = cu_scr[pl.ds(tm, HALO), :]

    @pl.when(step % 2 == 0)
    def _():
        body(proj_a, proj_b)

    @pl.when(step % 2 == 1)
    def _():
        body(proj_b, proj_a)


def _attn_kernel(sinks_ref, q_ref, kc_ref, kp_ref, vc_ref, vp_ref, cn_ref, x_ref, wo_hbm, anw_ref,
                 o_ref, s_scr, p_scr, inv_scr, attn_scr, mask_scr, wo_ref, w_stage, w_sems):
    tq = q_ref.shape[0]
    nb = tq // BLOCK
    group_pairs = N_Q_HEADS // N_KV_HEADS // 2
    nt = (((1,), (1,)), ((), ()))

    @pl.when(_first_step())
    def _():
        _fetch_bf16(wo_hbm, wo_ref, w_stage, w_sems, w_stage.shape[1])

    qi = lax.broadcasted_iota(jnp.int32, (BLOCK, 2 * BLOCK), 0)
    sj = lax.broadcasted_iota(jnp.int32, (BLOCK, 2 * BLOCK), 1)
    rel = qi + BLOCK - sj
    band = (rel >= 0) & (rel < WINDOW)
    band0 = band & ((sj >= BLOCK) | (pl.program_id(1) > 0))
    mask_scr[0] = band0.astype(jnp.int32)
    mask_scr[1] = band.astype(jnp.int32)
    low = lax.broadcasted_iota(jnp.int32, (2 * BLOCK, LANES), 1) < HEAD_DIM

    def window(cur_ref, prev_ref, j, cols):
        if j == 0:
            return jnp.concatenate([prev_ref[:, cols], cur_ref[pl.ds(0, BLOCK), cols]], axis=0)
        return cur_ref[pl.ds((j - 1) * BLOCK, 2 * BLOCK), cols]

    def scores(j):
        rows = pl.ds(j * BLOCK, BLOCK)
        for c in range(N_KV_HEADS):
            qs = jnp.concatenate(
                [q_ref[rows, pl.ds((group_pairs * c + g) * LANES, LANES)] for g in range(group_pairs)], axis=0)
            for e in range(2):
                k = window(kc_ref, kp_ref, j, pl.ds((2 * c + e) * LANES, LANES))
                s_scr[j, 2 * c + e] = lax.dot_general(qs, k, nt, preferred_element_type=F32)

    def softmax(j):
        for q0 in range(0, BLOCK, SOFTMAX_ROWS):
            keep = mask_scr[min(j, 1), pl.ds(q0, SOFTMAX_ROWS), :] > 0
            for ce, g in [(ce, g) for ce in range(2 * N_KV_HEADS) for g in range(group_pairs)]:
                c, e = divmod(ce, 2)
                r = g * BLOCK + q0
                rows = pl.ds(r, SOFTMAX_ROWS)
                s = jnp.where(keep, s_scr[j, ce, rows, :], NEG_INF)
                sink = sinks_ref[(N_Q_HEADS // N_KV_HEADS) * c + 2 * (r // BLOCK) + e]
                m = jnp.maximum(jnp.max(s, axis=-1, keepdims=True), sink)
                pe = jnp.exp(s - m)
                denom = jnp.sum(pe, axis=-1, keepdims=True) + jnp.exp(sink - m)
                p_scr[j, ce, rows, :] = pe.astype(BF16)
                inv_scr[j, ce, rows, :] = jnp.broadcast_to(1.0 / denom, (SOFTMAX_ROWS, LANES))

    def values(j):
        rows = pl.ds(j * BLOCK, BLOCK)
        for c in range(N_KV_HEADS):
            acc = None
            for e in range(2):
                v = window(vc_ref, vp_ref, j, pl.ds((2 * c + e) * LANES, LANES))
                pv = jnp.dot(p_scr[j, 2 * c + e], v, preferred_element_type=F32)
                acc = pv if acc is None else acc + pv
            o = acc * jnp.where(low, inv_scr[j, 2 * c], inv_scr[j, 2 * c + 1])
            for g in range(group_pairs):
                attn_scr[rows, pl.ds((group_pairs * c + g) * LANES, LANES)] = o[g * BLOCK:(g + 1) * BLOCK]

    for t in range(nb + 2):
        if t < nb:
            scores(t)
        if 0 <= t - 1 < nb:
            softmax(t - 1)
        if 0 <= t - 2 < nb:
            values(t - 2)

    an = _rms(attn_scr[...], anw_ref[...]).astype(BF16)
    y = jnp.dot(an, wo_ref[pl.ds(0, ATTN_WIDTH), :], preferred_element_type=F32)
    y = y + jnp.dot(cn_ref[...], wo_ref[pl.ds(ATTN_WIDTH, CONV_WIDTH), :], preferred_element_type=F32)
    y = x_ref[...] + y
    for k in range(o_ref.shape[0]):
        o_ref[k] = y[:, k * LANES:(k + 1) * LANES]


def _ffn_kernel(x_ref, nw_ref, wup_hbm, cw_ref, wd_hbm, o_ref, o_slab, up_scr, a_scr,
                wup_ref, wd_ref, up_stage, down_stage, w_sems):
    slabs, tm, _ = x_ref.shape
    m = tm // STRIDE
    d_ff = wd_ref.shape[0]
    nc = d_ff // MXU_DIM
    moved_start = [STRIDE * m + k * (m + HALO) for k in range(2)]

    x = jnp.concatenate(
        [jnp.concatenate([x_ref[k, pl.ds(v, m, stride=STRIDE), :] for k in range(slabs)], axis=1)
         for v in range(STRIDE)], axis=0)
    h = _rms(x, nw_ref[...]).astype(BF16)

    @pl.when(_first_step())
    def _():
        _fetch_bf16(wup_hbm, wup_ref, up_stage, w_sems, up_stage.shape[1])
        _fetch_bf16(wd_hbm, wd_ref, down_stage, w_sems, down_stage.shape[1])

    @pl.when(pl.program_id(1) == 0)
    def _():
        for s in moved_start:
            up_scr[:, pl.ds(s, SUBLANES), :] = jnp.zeros((nc, SUBLANES, 2 * MXU_DIM), F32)

    def pair(ref, c):
        return jnp.concatenate([ref[:, pl.ds(half + c * MXU_DIM, MXU_DIM)] for half in (0, d_ff)], axis=1)

    def up(c):
        r = jnp.dot(h, pair(wup_ref, c), preferred_element_type=F32)
        up_scr[c, pl.ds(0, tm), :] = r
        for k, s in enumerate(moved_start):
            v = STRIDE - 2 + k
            up_scr[c, pl.ds(s + 1, m), :] = r[v * m:(v + 1) * m]

    def gate(c, token):
        cw = pair(cw_ref, c)
        first = [v * m for v in range(STRIDE)]
        s1 = moved_start[1:] + first[:STRIDE - 1]
        s2 = moved_start + first[:STRIDE - 2]
        for v in range(STRIDE):
            for r in range(0, m, GATE_ROWS):
                z = _chained(up_scr[c, pl.ds(first[v] + r, GATE_ROWS), :], token, MXU_DIM)
                y = (cw[0:1, :] * up_scr[c, pl.ds(s2[v] + r, GATE_ROWS), :]
                     + cw[1:2, :] * up_scr[c, pl.ds(s1[v] + r, GATE_ROWS), :] + cw[2:3, :] * z)
                g, u = y[:, :MXU_DIM], y[:, MXU_DIM:]
                a = (g / (1.0 + jnp.exp2(g * -LOG2_E)) * u).astype(BF16)
                a_scr[c, pl.ds(v * m + r, GATE_ROWS), :] = a
                token = _zero_tile(a)
        return token

    token = None
    for c in range(nc):
        up(c)
        if c >= 1:
            token = gate(c - 1, token)
    gate(nc - 1, token)

    split = (nc - 1) * MXU_DIM
    a_head = jnp.concatenate([a_scr[c] for c in range(nc - 1)], axis=1)
    y = x + jnp.dot(a_head, wd_ref[pl.ds(0, split), :], preferred_element_type=F32)
    y = y + jnp.dot(a_scr[nc - 1], wd_ref[pl.ds(split, MXU_DIM), :], preferred_element_type=F32)

    for s in moved_start:
        up_scr[:, pl.ds(s, 1), :] = up_scr[:, pl.ds(s + m, 1), :]

    for k in range(slabs):
        for v in range(STRIDE):
            o_slab[k, pl.ds(v, m, stride=STRIDE), :] = y[v * m:(v + 1) * m, k * LANES:(k + 1) * LANES]
        o_ref[:, pl.ds(k * LANES, LANES)] = o_slab[k]


def _resident(shape):
    return pl.BlockSpec(shape, lambda b, i: (0,) * len(shape), pipeline_mode=pl.Buffered(1))


def _tile(tm, width):
    return pl.BlockSpec((None, tm, width), lambda b, i: (b, i, 0))


def _params(vmem_limit_bytes=VMEM_LIMIT_BYTES):
    return pltpu.CompilerParams(dimension_semantics=("arbitrary", "arbitrary"), vmem_limit_bytes=vmem_limit_bytes)


def _layer(x, attn_norm_w, w_in, q_norm_w, k_norm_w, sinks, conv_mix_w, attn_out_norm_w,
           conv_out_norm_w, w_out, ffn_norm_w, w_up, ffn_conv_w, w_down):
    bsz, seq, d = x.shape
    d_ff = w_down.shape[0]
    tm = TOKEN_TILE
    assert seq % tm == 0 and tm % BLOCK == 0 and d_ff % MXU_DIM == 0
    n_tiles = seq // tm
    nc = d_ff // MXU_DIM

    scale = HEAD_DIM ** -0.5
    qkw = jnp.concatenate([jnp.tile(q_norm_w, N_Q_HEADS) * scale, jnp.tile(k_norm_w, N_KV_HEADS)])[None, :]
    seg = jnp.arange(MXU_DIM) // HEAD_DIM
    bd = (seg[:, None] == seg[None, :]).astype(BF16)
    in_hbm = pl.BlockSpec(memory_space=pl.ANY)

    def stage(w):
        rows = max(r for r in range(BF16_ROWS, w.shape[0] + 1, BF16_ROWS)
                   if w.shape[0] % r == 0 and r * w.shape[1] * 4 <= WEIGHT_CHUNK_BYTES)
        return pltpu.VMEM((2, rows, w.shape[1]), F32)

    dma_sems = pltpu.SemaphoreType.DMA((2,))

    def this_tile(width):
        return pl.BlockSpec((None, tm, width), lambda b, i: (b, jnp.minimum(i, n_tiles - 1), 0))

    def last_tile(width):
        return pl.BlockSpec((None, tm, width), lambda b, i: (b, jnp.maximum(i - 1, 0), 0))

    act = jax.ShapeDtypeStruct((bsz, seq, 4 * LANES), BF16)
    in_w = w_in.shape[1]
    qn, kk, vv, cn = pl.pallas_call(
        _in_proj_kernel,
        grid=(bsz, n_tiles + 1),
        in_specs=[this_tile(d), _resident((1, d)), in_hbm,
                  _resident((1, ATTN_WIDTH + KV_WIDTH)), _resident((MXU_DIM, MXU_DIM)),
                  _resident((CONV_TAPS, CONV_WIDTH)), _resident((1, CONV_WIDTH))],
        out_specs=[last_tile(4 * LANES)] * 4,
        out_shape=[act] * 4,
        scratch_shapes=[pltpu.VMEM((tm, in_w), F32), pltpu.VMEM((tm, in_w), F32),
                        pltpu.VMEM((tm + HALO, CONV_WIDTH), F32),
                        pltpu.VMEM((d, in_w), BF16), stage(w_in), dma_sems],
        compiler_params=_params(),
        name="in_proj",
    )(x, attn_norm_w[None, :], w_in, qkw, bd, conv_mix_w, conv_out_norm_w[None, :])

    bpt = tm // BLOCK
    slab_tile = pl.BlockSpec((None, d // LANES, tm, LANES), lambda b, i: (b, 0, i, 0))
    prev = pl.BlockSpec((None, BLOCK, 4 * LANES), lambda b, i: (b, jnp.maximum(i * bpt - 1, 0), 0))
    x = pl.pallas_call(
        _attn_kernel,
        grid=(bsz, n_tiles),
        in_specs=[pl.BlockSpec(memory_space=pltpu.SMEM),
                  _tile(tm, 4 * LANES), _tile(tm, 4 * LANES), prev, _tile(tm, 4 * LANES), prev,
                  _tile(tm, 4 * LANES), _tile(tm, d), in_hbm, _resident((1, ATTN_WIDTH))],
        out_specs=slab_tile,
        out_shape=jax.ShapeDtypeStruct((bsz, d // LANES, seq, LANES), x.dtype),
        scratch_shapes=[pltpu.VMEM((bpt, 2 * N_KV_HEADS, 2 * BLOCK, 2 * BLOCK), F32),
                        pltpu.VMEM((bpt, 2 * N_KV_HEADS, 2 * BLOCK, 2 * BLOCK), BF16),
                        pltpu.VMEM((bpt, 2 * N_KV_HEADS, 2 * BLOCK, LANES), F32),
                        pltpu.VMEM((tm, ATTN_WIDTH), F32),
                        pltpu.VMEM((2, BLOCK, 2 * BLOCK), jnp.int32),
                        pltpu.VMEM((ATTN_WIDTH + CONV_WIDTH, d), BF16), stage(w_out), dma_sems],
        compiler_params=_params(),
        name="attn_out_proj",
    )(sinks, qn, kk, kk, vv, vv, cn, x, w_out, attn_out_norm_w[None, :])

    x = pl.pallas_call(
        _ffn_kernel,
        grid=(bsz, n_tiles),
        in_specs=[slab_tile, _resident((1, d)), in_hbm, _resident((CONV_TAPS, 2 * d_ff)), in_hbm],
        out_specs=_tile(tm, d),
        out_shape=jax.ShapeDtypeStruct((bsz, seq, d), x.dtype),
        scratch_shapes=[pltpu.VMEM((d // LANES, tm, LANES), F32),
                        pltpu.VMEM((nc, tm + 2 * (tm // STRIDE + HALO), 2 * MXU_DIM), F32),
                        pltpu.VMEM((nc, tm, MXU_DIM), BF16),
                        pltpu.VMEM((d, 2 * d_ff), BF16), pltpu.VMEM((d_ff, d), BF16),
                        stage(w_up), stage(w_down), dma_sems],
        compiler_params=_params(),
        name="conv_ffn",
    )(x, ffn_norm_w[None, :], w_up, ffn_conv_w, w_down)
    return x


def kernel(x, attn_norm_w, w_in, q_norm_w, k_norm_w, sinks, conv_mix_w, attn_out_norm_w,
           conv_out_norm_w, w_out, ffn_norm_w, w_up, ffn_conv_w, w_down):
    for l in range(attn_norm_w.shape[0]):
        x = _layer(x, attn_norm_w[l], w_in[l], q_norm_w[l], k_norm_w[l], sinks[l], conv_mix_w[l],
                   attn_out_norm_w[l], conv_out_norm_w[l], w_out[l], ffn_norm_w[l], w_up[l],
                   ffn_conv_w[l], w_down[l])
    return x
```

```python
import jax
import jax.numpy as jnp
from jax import lax
from jax.experimental import pallas as pl
from jax.experimental.pallas import tpu as pltpu

HEAD_DIM = 64
N_Q_HEADS = 8
N_KV_HEADS = 2
WINDOW = 128
BLOCK = 128
ATTN_WIDTH = N_Q_HEADS * HEAD_DIM
KV_WIDTH = N_KV_HEADS * HEAD_DIM
CONV_WIDTH = 512
CONV_TAPS = 3
EPS = 1e-6
NEG_INF = -1e30
LOG2_E = 1.4426950408889634

LANES = 128
SUBLANES = 8
MXU_DIM = 256
VMEM_LIMIT_BYTES = 56 * 1024 * 1024

TOKEN_TILE = 512
HALO = SUBLANES
STRIDE = 4
GATE_ROWS = 64
PIECE_ROWS = 32
SOFTMAX_ROWS = 16

F32 = jnp.float32
BF16 = jnp.bfloat16


def _rms(x, w):
    ms = jnp.mean(x * x, axis=-1, keepdims=True)
    return x * lax.rsqrt(ms + EPS) * w


def _zero_tile(packed):
    words = pltpu.bitcast(packed, jnp.uint32)
    while words.shape[1] > LANES:
        half = words.shape[1] // 2
        words = words[:, :half] | words[:, half:]
    while words.shape[0] > SUBLANES:
        half = words.shape[0] // 2
        words = words[:half] | words[half:]
    return ((words >> 16) >> 16).astype(F32)


def _chained(x, zero_tile, width=LANES):
    if zero_tile is None:
        return x
    zeros = jnp.concatenate([jnp.concatenate([zero_tile] * (x.shape[0] // SUBLANES), axis=0)] * (width // LANES),
                            axis=1)
    if width == x.shape[1]:
        return x + zeros
    return jnp.concatenate([x[:, :width] + zeros, x[:, width:]], axis=1)


def _fetch_bf16(w_hbm, w_bf16, slots, sems):
    rows, cols = slots[0].shape
    blocks = [(r, c) for r in range(0, w_hbm.shape[0], rows) for c in range(0, w_hbm.shape[1], cols)]
    assert w_hbm.shape[0] % rows == 0 and w_hbm.shape[1] % cols == 0

    def copy(b):
        r, c = blocks[b]
        s = b % len(slots)
        return pltpu.make_async_copy(w_hbm.at[pl.ds(r, rows), pl.ds(c, cols)], slots[s], sems.at[s])

    for b in range(min(len(slots), len(blocks))):
        copy(b).start()
    for b, (r, c) in enumerate(blocks):
        copy(b).wait()
        w_bf16[pl.ds(r, rows), pl.ds(c, cols)] = slots[b % len(slots)][...].astype(BF16)
        if b + len(slots) < len(blocks):
            copy(b + len(slots)).start()


def _first_step():
    return (pl.program_id(0) == 0) & (pl.program_id(1) == 0)


def _in_proj_kernel(x_ref, nw_ref, win_hbm, qkw_ref, bd_ref, cw_ref, cnw_ref,
                    q_ref, kk_ref, vv_ref, cn_ref, proj_a, proj_b, cu_scr, win_ref, w_sems):
    tm = x_ref.shape[0]
    qkv_w = ATTN_WIDTH + 2 * KV_WIDTH
    step = pl.program_id(1)

    @pl.when(_first_step())
    def _():
        _fetch_bf16(win_hbm, win_ref, [proj_a, proj_b], w_sems)

    @pl.when(step == 0)
    def _():
        proj_b[...] = jnp.zeros(proj_b.shape, F32)
        cu_scr[pl.ds(0, HALO), :] = jnp.zeros((HALO, CONV_WIDTH), F32)

    def body(fill, drain):
        def variants(src, dst):
            low = lax.broadcasted_iota(jnp.int32, src.shape, 1) < HEAD_DIM
            swapped = pltpu.roll(src, HEAD_DIM, axis=1)
            dst[...] = jnp.concatenate([jnp.where(low, src, 0.0), jnp.where(low, 0.0, swapped),
                                        jnp.where(low, swapped, 0.0), jnp.where(low, 0.0, src)],
                                       axis=1).astype(BF16)

        h = _rms(x_ref[...], nw_ref[...]).astype(BF16)
        fill[...] = jnp.dot(h, win_ref[...], preferred_element_type=F32)

        qk = drain[:, pl.ds(0, ATTN_WIDTH + KV_WIDTH)]
        sq = (qk * qk).astype(BF16)
        bd = bd_ref[...]
        ssq = jnp.concatenate(
            [jnp.dot(sq[:, 0:MXU_DIM], bd, preferred_element_type=F32),
             jnp.dot(sq[:, MXU_DIM:2 * MXU_DIM], bd, preferred_element_type=F32),
             jnp.dot(sq[:, 2 * MXU_DIM:], bd[:KV_WIDTH, :KV_WIDTH], preferred_element_type=F32)],
            axis=1)
        qkn = qk * lax.rsqrt(ssq * (1.0 / HEAD_DIM) + EPS) * qkw_ref[...]
        q_ref[...] = qkn[:, :ATTN_WIDTH].astype(BF16)
        variants(qkn[:, ATTN_WIDTH:], kk_ref)
        variants(drain[:, pl.ds(ATTN_WIDTH + KV_WIDTH, KV_WIDTH)], vv_ref)

        cw = cw_ref[...]
        token = None
        for r in range(0, tm, PIECE_ROWS):
            rows = pl.ds(r, PIECE_ROWS)
            bg = _chained(drain[rows, pl.ds(qkv_w, CONV_WIDTH)], token)
            cu_scr[pl.ds(HALO + r, PIECE_ROWS), :] = (drain[rows, pl.ds(qkv_w + CONV_WIDTH, CONV_WIDTH)]
                                                      * drain[rows, pl.ds(qkv_w + 2 * CONV_WIDTH, CONV_WIDTH)])
            y = cw[0:1, :] * cu_scr[pl.ds(HALO + r - 2, PIECE_ROWS), :]
            y = y + cw[1:2, :] * cu_scr[pl.ds(HALO + r - 1, PIECE_ROWS), :]
            y = y + cw[2:3, :] * cu_scr[pl.ds(HALO + r, PIECE_ROWS), :]
            cn = _rms(bg * y, cnw_ref[...]).astype(BF16)
            cn_ref[rows, :] = cn
            token = _zero_tile(cn)
        cu_scr[pl.ds(0, HALO), :] = cu_scr[pl.ds(tm, HALO), :]

    @pl.when(step % 2 == 0)
    def _():
        body(proj_a, proj_b)

    @pl.when(step % 2 == 1)
    def _():
        body(proj_b, proj_a)


def _attn_kernel(sinks_ref, q_ref, kc_ref, kp_ref, vc_ref, vp_ref, cn_ref, x_ref, wo_hbm, anw_ref,
                 o_ref, s_scr, p_scr, inv_scr, attn_scr, mask_scr, wo_ref, w_sems):
    tq = q_ref.shape[0]
    nb = tq // BLOCK
    group_pairs = N_Q_HEADS // N_KV_HEADS // 2
    nt = (((1,), (1,)), ((), ()))

    @pl.when(_first_step())
    def _():
        _fetch_bf16(wo_hbm, wo_ref, [s_scr.at[j, ce] for j in range(s_scr.shape[0]) for ce in range(s_scr.shape[1])],
                    w_sems)

    qi = lax.broadcasted_iota(jnp.int32, (BLOCK, 2 * BLOCK), 0)
    sj = lax.broadcasted_iota(jnp.int32, (BLOCK, 2 * BLOCK), 1)
    rel = qi + BLOCK - sj
    band = (rel >= 0) & (rel < WINDOW)
    band0 = band & ((sj >= BLOCK) | (pl.program_id(1) > 0))
    mask_scr[0] = band0.astype(jnp.int32)
    mask_scr[1] = band.astype(jnp.int32)
    low = lax.broadcasted_iota(jnp.int32, (2 * BLOCK, LANES), 1) < HEAD_DIM

    def window(cur_ref, prev_ref, j, cols):
        if j == 0:
            return jnp.concatenate([prev_ref[:, cols], cur_ref[pl.ds(0, BLOCK), cols]], axis=0)
        return cur_ref[pl.ds((j - 1) * BLOCK, 2 * BLOCK), cols]

    def scores(j):
        rows = pl.ds(j * BLOCK, BLOCK)
        for c in range(N_KV_HEADS):
            qs = jnp.concatenate(
                [q_ref[rows, pl.ds((group_pairs * c + g) * LANES, LANES)] for g in range(group_pairs)], axis=0)
            for e in range(2):
                k = window(kc_ref, kp_ref, j, pl.ds((2 * c + e) * LANES, LANES))
                s_scr[j, 2 * c + e] = lax.dot_general(qs, k, nt, preferred_element_type=F32)

    def softmax(j):
        for q0 in range(0, BLOCK, SOFTMAX_ROWS):
            keep = mask_scr[min(j, 1), pl.ds(q0, SOFTMAX_ROWS), :] > 0
            for ce, g in [(ce, g) for ce in range(2 * N_KV_HEADS) for g in range(group_pairs)]:
                c, e = divmod(ce, 2)
                r = g * BLOCK + q0
                rows = pl.ds(r, SOFTMAX_ROWS)
                s = jnp.where(keep, s_scr[j, ce, rows, :], NEG_INF)
                sink = sinks_ref[(N_Q_HEADS // N_KV_HEADS) * c + 2 * (r // BLOCK) + e]
                m = jnp.maximum(jnp.max(s, axis=-1, keepdims=True), sink)
                pe = jnp.exp(s - m)
                denom = jnp.sum(pe, axis=-1, keepdims=True) + jnp.exp(sink - m)
                p_scr[j, ce, rows, :] = pe.astype(BF16)
                inv_scr[j, ce, rows, :] = jnp.broadcast_to(1.0 / denom, (SOFTMAX_ROWS, LANES))

    def values(j):
        rows = pl.ds(j * BLOCK, BLOCK)
        for c in range(N_KV_HEADS):
            acc = None
            for e in range(2):
                v = window(vc_ref, vp_ref, j, pl.ds((2 * c + e) * LANES, LANES))
                pv = jnp.dot(p_scr[j, 2 * c + e], v, preferred_element_type=F32)
                acc = pv if acc is None else acc + pv
            o = acc * jnp.where(low, inv_scr[j, 2 * c], inv_scr[j, 2 * c + 1])
            for g in range(group_pairs):
                attn_scr[rows, pl.ds((group_pairs * c + g) * LANES, LANES)] = o[g * BLOCK:(g + 1) * BLOCK]

    for t in range(nb + 2):
        if t < nb:
            scores(t)
        if 0 <= t - 1 < nb:
            softmax(t - 1)
        if 0 <= t - 2 < nb:
            values(t - 2)

    an = _rms(attn_scr[...], anw_ref[...]).astype(BF16)
    y = jnp.dot(an, wo_ref[pl.ds(0, ATTN_WIDTH), :], preferred_element_type=F32)
    y = y + jnp.dot(cn_ref[...], wo_ref[pl.ds(ATTN_WIDTH, CONV_WIDTH), :], preferred_element_type=F32)
    y = x_ref[...] + y
    for k in range(o_ref.shape[0]):
        o_ref[k] = y[:, k * LANES:(k + 1) * LANES]


def _ffn_kernel(x_ref, nw_ref, wup_hbm, cw_ref, wd_hbm, o_ref, o_slab, up_scr, a_scr,
                wup_ref, wd_ref, w_sems):
    slabs, tm, _ = x_ref.shape
    m = tm // STRIDE
    d_ff = wd_ref.shape[0]
    nc = d_ff // MXU_DIM
    moved_start = [STRIDE * m + k * (m + HALO) for k in range(2)]

    x = jnp.concatenate(
        [jnp.concatenate([x_ref[k, pl.ds(v, m, stride=STRIDE), :] for k in range(slabs)], axis=1)
         for v in range(STRIDE)], axis=0)
    h = _rms(x, nw_ref[...]).astype(BF16)

    @pl.when(_first_step())
    def _():
        _fetch_bf16(wup_hbm, wup_ref, [up_scr.at[c, pl.ds(0, tm), :] for c in range(nc)], w_sems)
        _fetch_bf16(wd_hbm, wd_ref, [up_scr.at[c, pl.ds(0, MXU_DIM), :] for c in range(nc)], w_sems)

    @pl.when(pl.program_id(1) == 0)
    def _():
        for s in moved_start:
            up_scr[:, pl.ds(s, SUBLANES), :] = jnp.zeros((nc, SUBLANES, 2 * MXU_DIM), F32)

    def pair(ref, c):
        return jnp.concatenate([ref[:, pl.ds(half + c * MXU_DIM, MXU_DIM)] for half in (0, d_ff)], axis=1)

    def up(c):
        r = jnp.dot(h, pair(wup_ref, c), preferred_element_type=F32)
        up_scr[c, pl.ds(0, tm), :] = r
        for k, s in enumerate(moved_start):
            v = STRIDE - 2 + k
            up_scr[c, pl.ds(s + 1, m), :] = r[v * m:(v + 1) * m]

    def gate(c, token):
        cw = pair(cw_ref, c)
        first = [v * m for v in range(STRIDE)]
        s1 = moved_start[1:] + first[:STRIDE - 1]
        s2 = moved_start + first[:STRIDE - 2]
        for v in range(STRIDE):
            for r in range(0, m, GATE_ROWS):
                z = _chained(up_scr[c, pl.ds(first[v] + r, GATE_ROWS), :], token, MXU_DIM)
                y = (cw[0:1, :] * up_scr[c, pl.ds(s2[v] + r, GATE_ROWS), :]
                     + cw[1:2, :] * up_scr[c, pl.ds(s1[v] + r, GATE_ROWS), :] + cw[2:3, :] * z)
                g, u = y[:, :MXU_DIM], y[:, MXU_DIM:]
                a = (g / (1.0 + jnp.exp2(g * -LOG2_E)) * u).astype(BF16)
                a_scr[c, pl.ds(v * m + r, GATE_ROWS), :] = a
                token = _zero_tile(a)
        return token

    token = None
    for c in range(nc):
        up(c)
        if c >= 1:
            token = gate(c - 1, token)
    gate(nc - 1, token)

    split = (nc - 1) * MXU_DIM
    a_head = jnp.concatenate([a_scr[c] for c in range(nc - 1)], axis=1)
    y = x + jnp.dot(a_head, wd_ref[pl.ds(0, split), :], preferred_element_type=F32)
    y = y + jnp.dot(a_scr[nc - 1], wd_ref[pl.ds(split, MXU_DIM), :], preferred_element_type=F32)

    for s in moved_start:
        up_scr[:, pl.ds(s, 1), :] = up_scr[:, pl.ds(s + m, 1), :]

    for k in range(slabs):
        for v in range(STRIDE):
            o_slab[k, pl.ds(v, m, stride=STRIDE), :] = y[v * m:(v + 1) * m, k * LANES:(k + 1) * LANES]
        o_ref[:, pl.ds(k * LANES, LANES)] = o_slab[k]


def _resident(shape):
    return pl.BlockSpec(shape, lambda b, i: (0,) * len(shape), pipeline_mode=pl.Buffered(1))


def _tile(tm, width):
    return pl.BlockSpec((None, tm, width), lambda b, i: (b, i, 0))


def _params(vmem_limit_bytes=VMEM_LIMIT_BYTES):
    return pltpu.CompilerParams(dimension_semantics=("arbitrary", "arbitrary"), vmem_limit_bytes=vmem_limit_bytes)


def _layer(x, attn_norm_w, w_in, q_norm_w, k_norm_w, sinks, conv_mix_w, attn_out_norm_w,
           conv_out_norm_w, w_out, ffn_norm_w, w_up, ffn_conv_w, w_down):
    bsz, seq, d = x.shape
    d_ff = w_down.shape[0]
    tm = TOKEN_TILE
    assert seq % tm == 0 and tm % BLOCK == 0 and d_ff % MXU_DIM == 0
    n_tiles = seq // tm
    nc = d_ff // MXU_DIM

    scale = HEAD_DIM ** -0.5
    qkw = jnp.concatenate([jnp.tile(q_norm_w, N_Q_HEADS) * scale, jnp.tile(k_norm_w, N_KV_HEADS)])[None, :]
    seg = jnp.arange(MXU_DIM) // HEAD_DIM
    bd = (seg[:, None] == seg[None, :]).astype(BF16)
    in_hbm = pl.BlockSpec(memory_space=pl.ANY)


    def this_tile(width):
        return pl.BlockSpec((None, tm, width), lambda b, i: (b, jnp.minimum(i, n_tiles - 1), 0))

    def last_tile(width):
        return pl.BlockSpec((None, tm, width), lambda b, i: (b, jnp.maximum(i - 1, 0), 0))

    act = jax.ShapeDtypeStruct((bsz, seq, 4 * LANES), BF16)
    in_w = w_in.shape[1]
    qn, kk, vv, cn = pl.pallas_call(
        _in_proj_kernel,
        grid=(bsz, n_tiles + 1),
        in_specs=[this_tile(d), _resident((1, d)), in_hbm,
                  _resident((1, ATTN_WIDTH + KV_WIDTH)), _resident((MXU_DIM, MXU_DIM)),
                  _resident((CONV_TAPS, CONV_WIDTH)), _resident((1, CONV_WIDTH))],
        out_specs=[last_tile(4 * LANES)] * 4,
        out_shape=[act] * 4,
        scratch_shapes=[pltpu.VMEM((tm, in_w), F32), pltpu.VMEM((tm, in_w), F32),
                        pltpu.VMEM((tm + HALO, CONV_WIDTH), F32),
                        pltpu.VMEM((d, in_w), BF16), pltpu.SemaphoreType.DMA((2,))],
        compiler_params=_params(),
        name="in_proj",
    )(x, attn_norm_w[None, :], w_in, qkw, bd, conv_mix_w, conv_out_norm_w[None, :])

    bpt = tm // BLOCK
    slab_tile = pl.BlockSpec((None, d // LANES, tm, LANES), lambda b, i: (b, 0, i, 0))
    prev = pl.BlockSpec((None, BLOCK, 4 * LANES), lambda b, i: (b, jnp.maximum(i * bpt - 1, 0), 0))
    x = pl.pallas_call(
        _attn_kernel,
        grid=(bsz, n_tiles),
        in_specs=[pl.BlockSpec(memory_space=pltpu.SMEM),
                  _tile(tm, 4 * LANES), _tile(tm, 4 * LANES), prev, _tile(tm, 4 * LANES), prev,
                  _tile(tm, 4 * LANES), _tile(tm, d), in_hbm, _resident((1, ATTN_WIDTH))],
        out_specs=slab_tile,
        out_shape=jax.ShapeDtypeStruct((bsz, d // LANES, seq, LANES), x.dtype),
        scratch_shapes=[pltpu.VMEM((bpt, 2 * N_KV_HEADS, 2 * BLOCK, 2 * BLOCK), F32),
                        pltpu.VMEM((bpt, 2 * N_KV_HEADS, 2 * BLOCK, 2 * BLOCK), BF16),
                        pltpu.VMEM((bpt, 2 * N_KV_HEADS, 2 * BLOCK, LANES), F32),
                        pltpu.VMEM((tm, ATTN_WIDTH), F32),
                        pltpu.VMEM((2, BLOCK, 2 * BLOCK), jnp.int32),
                        pltpu.VMEM((ATTN_WIDTH + CONV_WIDTH, d), BF16),
                        pltpu.SemaphoreType.DMA((bpt * 2 * N_KV_HEADS,))],
        compiler_params=_params(),
        name="attn_out_proj",
    )(sinks, qn, kk, kk, vv, vv, cn, x, w_out, attn_out_norm_w[None, :])

    x = pl.pallas_call(
        _ffn_kernel,
        grid=(bsz, n_tiles),
        in_specs=[slab_tile, _resident((1, d)), in_hbm, _resident((CONV_TAPS, 2 * d_ff)), in_hbm],
        out_specs=_tile(tm, d),
        out_shape=jax.ShapeDtypeStruct((bsz, seq, d), x.dtype),
        scratch_shapes=[pltpu.VMEM((d // LANES, tm, LANES), F32),
                        pltpu.VMEM((nc, tm + 2 * (tm // STRIDE + HALO), 2 * MXU_DIM), F32),
                        pltpu.VMEM((nc, tm, MXU_DIM), BF16),
                        pltpu.VMEM((d, 2 * d_ff), BF16), pltpu.VMEM((d_ff, d), BF16),
                        pltpu.SemaphoreType.DMA((nc,))],
        compiler_params=_params(),
        name="conv_ffn",
    )(x, ffn_norm_w[None, :], w_up, ffn_conv_w, w_down)
    return x


def kernel(x, attn_norm_w, w_in, q_norm_w, k_norm_w, sinks, conv_mix_w, attn_out_norm_w,
           conv_out_norm_w, w_out, ffn_norm_w, w_up, ffn_conv_w, w_down):
    for l in range(attn_norm_w.shape[0]):
        x = _layer(x, attn_norm_w[l], w_in[l], q_norm_w[l], k_norm_w[l], sinks[l], conv_mix_w[l],
                   attn_out_norm_w[l], conv_out_norm_w[l], w_out[l], ffn_norm_w[l], w_up[l],
                   ffn_conv_w[l], w_down[l])
    return x
```

```python
import jax
import jax.numpy as jnp
from jax import lax
from jax.experimental import pallas as pl
from jax.experimental.pallas import tpu as pltpu

HEAD_DIM = 64
N_Q_HEADS = 8
N_KV_HEADS = 2
WINDOW = 128
BLOCK = 128
ATTN_WIDTH = N_Q_HEADS * HEAD_DIM
KV_WIDTH = N_KV_HEADS * HEAD_DIM
CONV_WIDTH = 512
CONV_TAPS = 3
EPS = 1e-6
NEG_INF = -1e30
LOG2_E = 1.4426950408889634

LANES = 128
SUBLANES = 8
MXU_DIM = 256
VMEM_LIMIT_BYTES = 56 * 1024 * 1024

TOKEN_TILE = 512
HALO = SUBLANES
STRIDE = 4
GATE_ROWS = 64
PIECE_ROWS = 32
SOFTMAX_ROWS = 16

F32 = jnp.float32
BF16 = jnp.bfloat16


def _rms(x, w):
    ms = jnp.mean(x * x, axis=-1, keepdims=True)
    return x * lax.rsqrt(ms + EPS) * w


def _zero_tile(packed):
    words = pltpu.bitcast(packed, jnp.uint32)
    while words.shape[1] > LANES:
        half = words.shape[1] // 2
        words = words[:, :half] | words[:, half:]
    while words.shape[0] > SUBLANES:
        half = words.shape[0] // 2
        words = words[:half] | words[half:]
    return ((words >> 16) >> 16).astype(F32)


def _chained(x, zero_tile, width=LANES):
    if zero_tile is None:
        return x
    zeros = jnp.concatenate([jnp.concatenate([zero_tile] * (x.shape[0] // SUBLANES), axis=0)] * (width // LANES),
                            axis=1)
    if width == x.shape[1]:
        return x + zeros
    return jnp.concatenate([x[:, :width] + zeros, x[:, width:]], axis=1)


def _fetch_bf16(w_hbm, w_bf16, slots, sems):
    rows, cols = slots[0].shape
    blocks = [(r, c) for r in range(0, w_hbm.shape[0], rows) for c in range(0, w_hbm.shape[1], cols)]
    assert w_hbm.shape[0] % rows == 0 and w_hbm.shape[1] % cols == 0

    def copy(b):
        r, c = blocks[b]
        s = b % len(slots)
        return pltpu.make_async_copy(w_hbm.at[pl.ds(r, rows), pl.ds(c, cols)], slots[s], sems.at[s])

    for b in range(min(len(slots), len(blocks))):
        copy(b).start()
    for b, (r, c) in enumerate(blocks):
        copy(b).wait()
        w_bf16[pl.ds(r, rows), pl.ds(c, cols)] = slots[b % len(slots)][...].astype(BF16)
        if b + len(slots) < len(blocks):
            copy(b + len(slots)).start()


def _first_step():
    return (pl.program_id(0) == 0) & (pl.program_id(1) == 0)


def _in_proj_kernel(x_ref, nw_ref, win_hbm, qkw_ref, bd_ref, cw_ref, cnw_ref,
                    q_ref, kk_ref, vv_ref, cn_ref, proj_a, proj_b, cu_scr, win_ref, w_sems):
    tm = x_ref.shape[0]
    qkv_w = ATTN_WIDTH + 2 * KV_WIDTH
    step = pl.program_id(1)

    @pl.when(_first_step())
    def _():
        _fetch_bf16(win_hbm, win_ref, [proj_a, proj_b], w_sems)

    @pl.when(step == 0)
    def _():
        proj_b[...] = jnp.zeros(proj_b.shape, F32)
        cu_scr[pl.ds(0, HALO), :] = jnp.zeros((HALO, CONV_WIDTH), F32)

    def body(fill, drain):
        def variants(src, dst):
            low = lax.broadcasted_iota(jnp.int32, src.shape, 1) < HEAD_DIM
            swapped = pltpu.roll(src, HEAD_DIM, axis=1)
            dst[...] = jnp.concatenate([jnp.where(low, src, 0.0), jnp.where(low, 0.0, swapped),
                                        jnp.where(low, swapped, 0.0), jnp.where(low, 0.0, src)],
                                       axis=1).astype(BF16)

        x = x_ref[...]
        inv_rms = lax.rsqrt(jnp.mean(x * x, axis=-1, keepdims=True) + EPS)
        h = (x * nw_ref[...]).astype(BF16)
        fill[...] = jnp.dot(h, win_ref[...], preferred_element_type=F32) * inv_rms

        qk = drain[:, pl.ds(0, ATTN_WIDTH + KV_WIDTH)]
        sq = (qk * qk).astype(BF16)
        bd = bd_ref[...]
        ssq = jnp.concatenate(
            [jnp.dot(sq[:, 0:MXU_DIM], bd, preferred_element_type=F32),
             jnp.dot(sq[:, MXU_DIM:2 * MXU_DIM], bd, preferred_element_type=F32),
             jnp.dot(sq[:, 2 * MXU_DIM:], bd[:KV_WIDTH, :KV_WIDTH], preferred_element_type=F32)],
            axis=1)
        qkn = qk * lax.rsqrt(ssq * (1.0 / HEAD_DIM) + EPS) * qkw_ref[...]
        q_ref[...] = qkn[:, :ATTN_WIDTH].astype(BF16)
        variants(qkn[:, ATTN_WIDTH:], kk_ref)
        variants(drain[:, pl.ds(ATTN_WIDTH + KV_WIDTH, KV_WIDTH)], vv_ref)

        cw = cw_ref[...]
        token = None
        for r in range(0, tm, PIECE_ROWS):
            rows = pl.ds(r, PIECE_ROWS)
            bg = _chained(drain[rows, pl.ds(qkv_w, CONV_WIDTH)], token)
            cu_scr[pl.ds(HALO + r, PIECE_ROWS), :] = (drain[rows, pl.ds(qkv_w + CONV_WIDTH, CONV_WIDTH)]
                                                      * drain[rows, pl.ds(qkv_w + 2 * CONV_WIDTH, CONV_WIDTH)])
            y = cw[0:1, :] * cu_scr[pl.ds(HALO + r - 2, PIECE_ROWS), :]
            y = y + cw[1:2, :] * cu_scr[pl.ds(HALO + r - 1, PIECE_ROWS), :]
            y = y + cw[2:3, :] * cu_scr[pl.ds(HALO + r, PIECE_ROWS), :]
            cn = _rms(bg * y, cnw_ref[...]).astype(BF16)
            cn_ref[rows, :] = cn
            token = _zero_tile(cn)
        cu_scr[pl.ds(0, HALO), :] = cu_scr[pl.ds(tm, HALO), :]

    @pl.when(step % 2 == 0)
    def _():
        body(proj_a, proj_b)

    @pl.when(step % 2 == 1)
    def _():
        body(proj_b, proj_a)


def _attn_kernel(sinks_ref, q_ref, kc_ref, kp_ref, vc_ref, vp_ref, cn_ref, x_ref, wo_hbm, anw_ref,
                 o_ref, s_scr, p_scr, inv_scr, attn_scr, mask_scr, wo_ref, w_sems):
    tq = q_ref.shape[0]
    nb = tq // BLOCK
    group_pairs = N_Q_HEADS // N_KV_HEADS // 2
    nt = (((1,), (1,)), ((), ()))

    @pl.when(_first_step())
    def _():
        _fetch_bf16(wo_hbm, wo_ref, [s_scr.at[j, ce] for j in range(s_scr.shape[0]) for ce in range(s_scr.shape[1])],
                    w_sems)

    qi = lax.broadcasted_iota(jnp.int32, (BLOCK, 2 * BLOCK), 0)
    sj = lax.broadcasted_iota(jnp.int32, (BLOCK, 2 * BLOCK), 1)
    rel = qi + BLOCK - sj
    band = (rel >= 0) & (rel < WINDOW)
    band0 = band & ((sj >= BLOCK) | (pl.program_id(1) > 0))
    mask_scr[0] = band0.astype(jnp.int32)
    mask_scr[1] = band.astype(jnp.int32)
    low = lax.broadcasted_iota(jnp.int32, (2 * BLOCK, LANES), 1) < HEAD_DIM

    def window(cur_ref, prev_ref, j, cols):
        if j == 0:
            return jnp.concatenate([prev_ref[:, cols], cur_ref[pl.ds(0, BLOCK), cols]], axis=0)
        return cur_ref[pl.ds((j - 1) * BLOCK, 2 * BLOCK), cols]

    def scores(j):
        rows = pl.ds(j * BLOCK, BLOCK)
        for c in range(N_KV_HEADS):
            qs = jnp.concatenate(
                [q_ref[rows, pl.ds((group_pairs * c + g) * LANES, LANES)] for g in range(group_pairs)], axis=0)
            for e in range(2):
                k = window(kc_ref, kp_ref, j, pl.ds((2 * c + e) * LANES, LANES))
                s_scr[j, 2 * c + e] = lax.dot_general(qs, k, nt, preferred_element_type=F32)

    def softmax(j):
        for q0 in range(0, BLOCK, SOFTMAX_ROWS):
            keep = mask_scr[min(j, 1), pl.ds(q0, SOFTMAX_ROWS), :] > 0
            for ce, g in [(ce, g) for ce in range(2 * N_KV_HEADS) for g in range(group_pairs)]:
                c, e = divmod(ce, 2)
                r = g * BLOCK + q0
                rows = pl.ds(r, SOFTMAX_ROWS)
                s = jnp.where(keep, s_scr[j, ce, rows, :], NEG_INF)
                sink = sinks_ref[(N_Q_HEADS // N_KV_HEADS) * c + 2 * (r // BLOCK) + e]
                m = jnp.maximum(jnp.max(s, axis=-1, keepdims=True), sink)
                pe = jnp.exp(s - m)
                denom = jnp.sum(pe, axis=-1, keepdims=True) + jnp.exp(sink - m)
                p_scr[j, ce, rows, :] = pe.astype(BF16)
                inv_scr[j, ce, rows, :] = jnp.broadcast_to(1.0 / denom, (SOFTMAX_ROWS, LANES))

    def values(j):
        rows = pl.ds(j * BLOCK, BLOCK)
        for c in range(N_KV_HEADS):
            acc = None
            for e in range(2):
                v = window(vc_ref, vp_ref, j, pl.ds((2 * c + e) * LANES, LANES))
                pv = jnp.dot(p_scr[j, 2 * c + e], v, preferred_element_type=F32)
                acc = pv if acc is None else acc + pv
            o = acc * jnp.where(low, inv_scr[j, 2 * c], inv_scr[j, 2 * c + 1])
            for g in range(group_pairs):
                attn_scr[rows, pl.ds((group_pairs * c + g) * LANES, LANES)] = o[g * BLOCK:(g + 1) * BLOCK]

    for t in range(nb + 2):
        if t < nb:
            scores(t)
        if 0 <= t - 1 < nb:
            softmax(t - 1)
        if 0 <= t - 2 < nb:
            values(t - 2)

    an = _rms(attn_scr[...], anw_ref[...]).astype(BF16)
    y = jnp.dot(an, wo_ref[pl.ds(0, ATTN_WIDTH), :], preferred_element_type=F32)
    y = y + jnp.dot(cn_ref[...], wo_ref[pl.ds(ATTN_WIDTH, CONV_WIDTH), :], preferred_element_type=F32)
    y = x_ref[...] + y
    for k in range(o_ref.shape[0]):
        o_ref[k] = y[:, k * LANES:(k + 1) * LANES]


def _ffn_kernel(x_ref, nw_ref, wup_hbm, cw_ref, wd_hbm, o_ref, o_slab, up_scr, a_scr,
                wup_ref, wd_ref, w_sems):
    slabs, tm, _ = x_ref.shape
    m = tm // STRIDE
    d_ff = wd_ref.shape[0]
    nc = d_ff // MXU_DIM
    moved_start = [STRIDE * m + k * (m + HALO) for k in range(2)]

    @pl.when(_first_step())
    def _():
        _fetch_bf16(wup_hbm, wup_ref, [up_scr.at[c, pl.ds(0, tm), :] for c in range(nc)], w_sems)
        _fetch_bf16(wd_hbm, wd_ref, [up_scr.at[c, pl.ds(0, MXU_DIM), :] for c in range(nc)], w_sems)

    @pl.when(pl.program_id(1) == 0)
    def _():
        for s in moved_start:
            up_scr[:, pl.ds(s, SUBLANES), :] = jnp.zeros((nc, SUBLANES, 2 * MXU_DIM), F32)

    x = jnp.concatenate(
        [jnp.concatenate([x_ref[k, pl.ds(v, m, stride=STRIDE), :] for k in range(slabs)], axis=1)
         for v in range(STRIDE)], axis=0)
    inv_rms = lax.rsqrt(jnp.mean(x * x, axis=-1, keepdims=True) + EPS)
    h = (x * nw_ref[...]).astype(BF16)

    def pair(ref, c):
        return jnp.concatenate([ref[:, pl.ds(half + c * MXU_DIM, MXU_DIM)] for half in (0, d_ff)], axis=1)

    def up(c):
        r = jnp.dot(h, pair(wup_ref, c), preferred_element_type=F32) * inv_rms
        up_scr[c, pl.ds(0, tm), :] = r
        for k, s in enumerate(moved_start):
            v = STRIDE - 2 + k
            up_scr[c, pl.ds(s + 1, m), :] = r[v * m:(v + 1) * m]

    def gate(c, token):
        cw = pair(cw_ref, c)
        first = [v * m for v in range(STRIDE)]
        s1 = moved_start[1:] + first[:STRIDE - 1]
        s2 = moved_start + first[:STRIDE - 2]
        for v in range(STRIDE):
            for r in range(0, m, GATE_ROWS):
                z = _chained(up_scr[c, pl.ds(first[v] + r, GATE_ROWS), :], token, MXU_DIM)
                y = (cw[0:1, :] * up_scr[c, pl.ds(s2[v] + r, GATE_ROWS), :]
                     + cw[1:2, :] * up_scr[c, pl.ds(s1[v] + r, GATE_ROWS), :] + cw[2:3, :] * z)
                g, u = y[:, :MXU_DIM], y[:, MXU_DIM:]
                a = (g / (1.0 + jnp.exp2(g * -LOG2_E)) * u).astype(BF16)
                a_scr[c, pl.ds(v * m + r, GATE_ROWS), :] = a
                token = _zero_tile(a)
        return token

    token = None
    for c in range(nc):
        up(c)
        if c >= 1:
            token = gate(c - 1, token)
    gate(nc - 1, token)

    split = (nc - 1) * MXU_DIM
    a_head = jnp.concatenate([a_scr[c] for c in range(nc - 1)], axis=1)
    y = x + jnp.dot(a_head, wd_ref[pl.ds(0, split), :], preferred_element_type=F32)
    y = y + jnp.dot(a_scr[nc - 1], wd_ref[pl.ds(split, MXU_DIM), :], preferred_element_type=F32)

    for s in moved_start:
        up_scr[:, pl.ds(s, 1), :] = up_scr[:, pl.ds(s + m, 1), :]

    for k in range(slabs):
        for v in range(STRIDE):
            o_slab[k, pl.ds(v, m, stride=STRIDE), :] = y[v * m:(v + 1) * m, k * LANES:(k + 1) * LANES]
        o_ref[:, pl.ds(k * LANES, LANES)] = o_slab[k]


def _resident(shape):
    return pl.BlockSpec(shape, lambda b, i: (0,) * len(shape), pipeline_mode=pl.Buffered(1))


def _tile(tm, width):
    return pl.BlockSpec((None, tm, width), lambda b, i: (b, i, 0))


def _params(vmem_limit_bytes=VMEM_LIMIT_BYTES):
    return pltpu.CompilerParams(dimension_semantics=("arbitrary", "arbitrary"), vmem_limit_bytes=vmem_limit_bytes)


def _layer(x, attn_norm_w, w_in, q_norm_w, k_norm_w, sinks, conv_mix_w, attn_out_norm_w,
           conv_out_norm_w, w_out, ffn_norm_w, w_up, ffn_conv_w, w_down):
    bsz, seq, d = x.shape
    d_ff = w_down.shape[0]
    tm = TOKEN_TILE
    assert seq % tm == 0 and tm % BLOCK == 0 and d_ff % MXU_DIM == 0
    n_tiles = seq // tm
    nc = d_ff // MXU_DIM

    scale = HEAD_DIM ** -0.5
    qkw = jnp.concatenate([jnp.tile(q_norm_w, N_Q_HEADS) * scale, jnp.tile(k_norm_w, N_KV_HEADS)])[None, :]
    seg = jnp.arange(MXU_DIM) // HEAD_DIM
    bd = (seg[:, None] == seg[None, :]).astype(BF16)
    in_hbm = pl.BlockSpec(memory_space=pl.ANY)


    def this_tile(width):
        return pl.BlockSpec((None, tm, width), lambda b, i: (b, jnp.minimum(i, n_tiles - 1), 0))

    def last_tile(width):
        return pl.BlockSpec((None, tm, width), lambda b, i: (b, jnp.maximum(i - 1, 0), 0))

    act = jax.ShapeDtypeStruct((bsz, seq, 4 * LANES), BF16)
    in_w = w_in.shape[1]
    qn, kk, vv, cn = pl.pallas_call(
        _in_proj_kernel,
        grid=(bsz, n_tiles + 1),
        in_specs=[this_tile(d), _resident((1, d)), in_hbm,
                  _resident((1, ATTN_WIDTH + KV_WIDTH)), _resident((MXU_DIM, MXU_DIM)),
                  _resident((CONV_TAPS, CONV_WIDTH)), _resident((1, CONV_WIDTH))],
        out_specs=[last_tile(4 * LANES)] * 4,
        out_shape=[act] * 4,
        scratch_shapes=[pltpu.VMEM((tm, in_w), F32), pltpu.VMEM((tm, in_w), F32),
                        pltpu.VMEM((tm + HALO, CONV_WIDTH), F32),
                        pltpu.VMEM((d, in_w), BF16), pltpu.SemaphoreType.DMA((2,))],
        compiler_params=_params(),
        name="in_proj",
    )(x, attn_norm_w[None, :], w_in, qkw, bd, conv_mix_w, conv_out_norm_w[None, :])

    bpt = tm // BLOCK
    slab_tile = pl.BlockSpec((None, d // LANES, tm, LANES), lambda b, i: (b, 0, i, 0))
    prev = pl.BlockSpec((None, BLOCK, 4 * LANES), lambda b, i: (b, jnp.maximum(i * bpt - 1, 0), 0))
    x = pl.pallas_call(
        _attn_kernel,
        grid=(bsz, n_tiles),
        in_specs=[pl.BlockSpec(memory_space=pltpu.SMEM),
                  _tile(tm, 4 * LANES), _tile(tm, 4 * LANES), prev, _tile(tm, 4 * LANES), prev,
                  _tile(tm, 4 * LANES), _tile(tm, d), in_hbm, _resident((1, ATTN_WIDTH))],
        out_specs=slab_tile,
        out_shape=jax.ShapeDtypeStruct((bsz, d // LANES, seq, LANES), x.dtype),
        scratch_shapes=[pltpu.VMEM((bpt, 2 * N_KV_HEADS, 2 * BLOCK, 2 * BLOCK), F32),
                        pltpu.VMEM((bpt, 2 * N_KV_HEADS, 2 * BLOCK, 2 * BLOCK), BF16),
                        pltpu.VMEM((bpt, 2 * N_KV_HEADS, 2 * BLOCK, LANES), F32),
                        pltpu.VMEM((tm, ATTN_WIDTH), F32),
                        pltpu.VMEM((2, BLOCK, 2 * BLOCK), jnp.int32),
                        pltpu.VMEM((ATTN_WIDTH + CONV_WIDTH, d), BF16),
                        pltpu.SemaphoreType.DMA((bpt * 2 * N_KV_HEADS,))],
        compiler_params=_params(),
        name="attn_out_proj",
    )(sinks, qn, kk, kk, vv, vv, cn, x, w_out, attn_out_norm_w[None, :])

    x = pl.pallas_call(
        _ffn_kernel,
        grid=(bsz, n_tiles),
        in_specs=[slab_tile, _resident((1, d)), in_hbm, _resident((CONV_TAPS, 2 * d_ff)), in_hbm],
        out_specs=_tile(tm, d),
        out_shape=jax.ShapeDtypeStruct((bsz, seq, d), x.dtype),
        scratch_shapes=[pltpu.VMEM((d // LANES, tm, LANES), F32),
                        pltpu.VMEM((nc, tm + 2 * (tm // STRIDE + HALO), 2 * MXU_DIM), F32),
                        pltpu.VMEM((nc, tm, MXU_DIM), BF16),
                        pltpu.VMEM((d, 2 * d_ff), BF16), pltpu.VMEM((d_ff, d), BF16),
                        pltpu.SemaphoreType.DMA((nc,))],
        compiler_params=_params(),
        name="conv_ffn",
    )(x, ffn_norm_w[None, :], w_up, ffn_conv_w, w_down)
    return x


def kernel(x, attn_norm_w, w_in, q_norm_w, k_norm_w, sinks, conv_mix_w, attn_out_norm_w,
           conv_out_norm_w, w_out, ffn_norm_w, w_up, ffn_conv_w, w_down):
    for l in range(attn_norm_w.shape[0]):
        x = _layer(x, attn_norm_w[l], w_in[l], q_norm_w[l], k_norm_w[l], sinks[l], conv_mix_w[l],
                   attn_out_norm_w[l], conv_out_norm_w[l], w_out[l], ffn_norm_w[l], w_up[l],
                   ffn_conv_w[l], w_down[l])
    return x
```

```python
import jax
import jax.numpy as jnp
from jax import lax
from jax.experimental import pallas as pl
from jax.experimental.pallas import tpu as pltpu

HEAD_DIM = 64
N_Q_HEADS = 8
N_KV_HEADS = 2
WINDOW = 128
BLOCK = 128
ATTN_WIDTH = N_Q_HEADS * HEAD_DIM
KV_WIDTH = N_KV_HEADS * HEAD_DIM
CONV_WIDTH = 512
CONV_TAPS = 3
EPS = 1e-6
NEG_INF = -1e30
LOG2_E = 1.4426950408889634

LANES = 128
SUBLANES = 8
MXU_DIM = 256
VMEM_LIMIT_BYTES = 56 * 1024 * 1024

TOKEN_TILE = 512
HALO = SUBLANES
STRIDE = 4
GATE_ROWS = 64
PIECE_ROWS = 32
SOFTMAX_ROWS = 16

F32 = jnp.float32
BF16 = jnp.bfloat16


def _rms(x, w):
    ms = jnp.mean(x * x, axis=-1, keepdims=True)
    return x * lax.rsqrt(ms + EPS) * w


def _zero_tile(packed):
    words = pltpu.bitcast(packed, jnp.uint32)
    while words.shape[1] > LANES:
        half = words.shape[1] // 2
        words = words[:, :half] | words[:, half:]
    while words.shape[0] > SUBLANES:
        half = words.shape[0] // 2
        words = words[:half] | words[half:]
    return ((words >> 16) >> 16).astype(F32)


def _chained(x, zero_tile, width=LANES):
    if zero_tile is None:
        return x
    zeros = jnp.concatenate([jnp.concatenate([zero_tile] * (x.shape[0] // SUBLANES), axis=0)] * (width // LANES),
                            axis=1)
    if width == x.shape[1]:
        return x + zeros
    return jnp.concatenate([x[:, :width] + zeros, x[:, width:]], axis=1)


def _fetch_bf16(w_hbm, w_bf16, slots, sems):
    rows, cols = slots[0].shape
    blocks = [(r, c) for r in range(0, w_hbm.shape[0], rows) for c in range(0, w_hbm.shape[1], cols)]
    assert w_hbm.shape[0] % rows == 0 and w_hbm.shape[1] % cols == 0

    def copy(b):
        r, c = blocks[b]
        s = b % len(slots)
        return pltpu.make_async_copy(w_hbm.at[pl.ds(r, rows), pl.ds(c, cols)], slots[s], sems.at[s])

    for b in range(min(len(slots), len(blocks))):
        copy(b).start()
    for b, (r, c) in enumerate(blocks):
        copy(b).wait()
        w_bf16[pl.ds(r, rows), pl.ds(c, cols)] = slots[b % len(slots)][...].astype(BF16)
        if b + len(slots) < len(blocks):
            copy(b + len(slots)).start()


def _first_step():
    return (pl.program_id(0) == 0) & (pl.program_id(1) == 0)


def _in_proj_kernel(x_ref, nw_ref, win_hbm, qkw_ref, bd_ref, cw_ref, cnw_ref,
                    q_ref, kk_ref, vv_ref, cn_ref, proj_a, proj_b, cu_scr, win_ref, w_sems):
    tm = x_ref.shape[0]
    qkv_w = ATTN_WIDTH + 2 * KV_WIDTH
    step = pl.program_id(1)

    @pl.when(_first_step())
    def _():
        _fetch_bf16(win_hbm, win_ref, [proj_a, proj_b], w_sems)

    @pl.when(step == 0)
    def _():
        proj_b[...] = jnp.zeros(proj_b.shape, F32)
        cu_scr[pl.ds(0, HALO), :] = jnp.zeros((HALO, CONV_WIDTH), F32)

    def body(fill, drain):
        def variants(src, dst):
            low = lax.broadcasted_iota(jnp.int32, src.shape, 1) < HEAD_DIM
            swapped = pltpu.roll(src, HEAD_DIM, axis=1)
            dst[...] = jnp.concatenate([jnp.where(low, src, 0.0), jnp.where(low, 0.0, swapped),
                                        jnp.where(low, swapped, 0.0), jnp.where(low, 0.0, src)],
                                       axis=1).astype(BF16)

        x = x_ref[...]
        inv_rms = lax.rsqrt(jnp.mean(x * x, axis=-1, keepdims=True) + EPS)
        h = (x * nw_ref[...]).astype(BF16)
        fill[...] = jnp.dot(h, win_ref[...], preferred_element_type=F32) * inv_rms

        qk = drain[:, pl.ds(0, ATTN_WIDTH + KV_WIDTH)]
        sq = (qk * qk).astype(BF16)
        bd = bd_ref[...]
        ssq = jnp.concatenate(
            [jnp.dot(sq[:, 0:MXU_DIM], bd, preferred_element_type=F32),
             jnp.dot(sq[:, MXU_DIM:2 * MXU_DIM], bd, preferred_element_type=F32),
             jnp.dot(sq[:, 2 * MXU_DIM:], bd[:KV_WIDTH, :KV_WIDTH], preferred_element_type=F32)],
            axis=1)
        qkn = qk * lax.rsqrt(ssq * (1.0 / HEAD_DIM) + EPS) * qkw_ref[...]
        q_ref[...] = qkn[:, :ATTN_WIDTH].astype(BF16)
        variants(qkn[:, ATTN_WIDTH:], kk_ref)
        variants(drain[:, pl.ds(ATTN_WIDTH + KV_WIDTH, KV_WIDTH)], vv_ref)

        cw = cw_ref[...]
        token = None
        for r in range(0, tm, PIECE_ROWS):
            rows = pl.ds(r, PIECE_ROWS)
            bg = _chained(drain[rows, pl.ds(qkv_w, CONV_WIDTH)], token)
            cu_scr[pl.ds(HALO + r, PIECE_ROWS), :] = (drain[rows, pl.ds(qkv_w + CONV_WIDTH, CONV_WIDTH)]
                                                      * drain[rows, pl.ds(qkv_w + 2 * CONV_WIDTH, CONV_WIDTH)])
            y = cw[0:1, :] * cu_scr[pl.ds(HALO + r - 2, PIECE_ROWS), :]
            y = y + cw[1:2, :] * cu_scr[pl.ds(HALO + r - 1, PIECE_ROWS), :]
            y = y + cw[2:3, :] * cu_scr[pl.ds(HALO + r, PIECE_ROWS), :]
            cn = _rms(bg * y, cnw_ref[...]).astype(BF16)
            cn_ref[rows, :] = cn
            token = _zero_tile(cn)
        cu_scr[pl.ds(0, HALO), :] = cu_scr[pl.ds(tm, HALO), :]

    @pl.when(step % 2 == 0)
    def _():
        body(proj_a, proj_b)

    @pl.when(step % 2 == 1)
    def _():
        body(proj_b, proj_a)


def _attn_kernel(sinks_ref, q_ref, kc_ref, kp_ref, vc_ref, vp_ref, cn_ref, x_ref, wo_hbm, anw_ref,
                 o_ref, s_scr, p_scr, inv_scr, attn_scr, mask_scr, wo_ref, w_sems):
    tq = q_ref.shape[0]
    nb = tq // BLOCK
    group_pairs = N_Q_HEADS // N_KV_HEADS // 2
    nt = (((1,), (1,)), ((), ()))

    @pl.when(_first_step())
    def _():
        _fetch_bf16(wo_hbm, wo_ref, [s_scr.at[j, ce] for j in range(s_scr.shape[0]) for ce in range(s_scr.shape[1])],
                    w_sems)

    qi = lax.broadcasted_iota(jnp.int32, (BLOCK, 2 * BLOCK), 0)
    sj = lax.broadcasted_iota(jnp.int32, (BLOCK, 2 * BLOCK), 1)
    rel = qi + BLOCK - sj
    band = (rel >= 0) & (rel < WINDOW)
    band0 = band & ((sj >= BLOCK) | (pl.program_id(1) > 0))
    mask_scr[0] = band0.astype(jnp.int32)
    mask_scr[1] = band.astype(jnp.int32)
    low = lax.broadcasted_iota(jnp.int32, (2 * BLOCK, LANES), 1) < HEAD_DIM

    def window(cur_ref, prev_ref, j, cols):
        if j == 0:
            return jnp.concatenate([prev_ref[:, cols], cur_ref[pl.ds(0, BLOCK), cols]], axis=0)
        return cur_ref[pl.ds((j - 1) * BLOCK, 2 * BLOCK), cols]

    def scores(j):
        rows = pl.ds(j * BLOCK, BLOCK)
        for c in range(N_KV_HEADS):
            qs = jnp.concatenate(
                [q_ref[rows, pl.ds((group_pairs * c + g) * LANES, LANES)] for g in range(group_pairs)], axis=0)
            for e in range(2):
                k = window(kc_ref, kp_ref, j, pl.ds((2 * c + e) * LANES, LANES))
                s_scr[j, 2 * c + e] = lax.dot_general(qs, k, nt, preferred_element_type=F32)

    def softmax(j):
        for q0 in range(0, BLOCK, SOFTMAX_ROWS):
            keep = mask_scr[min(j, 1), pl.ds(q0, SOFTMAX_ROWS), :] > 0
            for ce, g in [(ce, g) for ce in range(2 * N_KV_HEADS) for g in range(group_pairs)]:
                c, e = divmod(ce, 2)
                r = g * BLOCK + q0
                rows = pl.ds(r, SOFTMAX_ROWS)
                s = jnp.where(keep, s_scr[j, ce, rows, :], NEG_INF)
                sink = sinks_ref[(N_Q_HEADS // N_KV_HEADS) * c + 2 * (r // BLOCK) + e]
                m = jnp.maximum(jnp.max(s, axis=-1, keepdims=True), sink)
                pe = jnp.exp(s - m)
                denom = jnp.sum(pe, axis=-1, keepdims=True) + jnp.exp(sink - m)
                p_scr[j, ce, rows, :] = pe.astype(BF16)
                inv_scr[j, ce, rows, :] = jnp.broadcast_to(1.0 / denom, (SOFTMAX_ROWS, LANES))

    def values(j):
        rows = pl.ds(j * BLOCK, BLOCK)
        for c in range(N_KV_HEADS):
            acc = None
            for e in range(2):
                v = window(vc_ref, vp_ref, j, pl.ds((2 * c + e) * LANES, LANES))
                pv = jnp.dot(p_scr[j, 2 * c + e], v, preferred_element_type=F32)
                acc = pv if acc is None else acc + pv
            o = acc * jnp.where(low, inv_scr[j, 2 * c], inv_scr[j, 2 * c + 1])
            for g in range(group_pairs):
                attn_scr[rows, pl.ds((group_pairs * c + g) * LANES, LANES)] = o[g * BLOCK:(g + 1) * BLOCK]

    for t in range(nb + 2):
        if t < nb:
            scores(t)
        if 0 <= t - 1 < nb:
            softmax(t - 1)
        if 0 <= t - 2 < nb:
            values(t - 2)

    an = _rms(attn_scr[...], anw_ref[...]).astype(BF16)
    y = jnp.dot(an, wo_ref[pl.ds(0, ATTN_WIDTH), :], preferred_element_type=F32)
    y = y + jnp.dot(cn_ref[...], wo_ref[pl.ds(ATTN_WIDTH, CONV_WIDTH), :], preferred_element_type=F32)
    o_ref[...] = x_ref[...] + y


def _ffn_kernel(x_ref, nw_ref, wup_hbm, cw_ref, wd_hbm, o_ref, up_scr, a_scr, wup_ref, wd_ref, w_sems):
    m = x_ref.shape[0]
    d = x_ref.shape[1] // STRIDE
    tm = m * STRIDE
    d_ff = wd_ref.shape[0]
    nc = d_ff // MXU_DIM
    moved_start = [STRIDE * m + k * (m + HALO) for k in range(2)]

    @pl.when(_first_step())
    def _():
        _fetch_bf16(wup_hbm, wup_ref, [up_scr.at[c, pl.ds(0, tm), :] for c in range(nc)], w_sems)
        _fetch_bf16(wd_hbm, wd_ref, [up_scr.at[c, pl.ds(0, MXU_DIM), :] for c in range(nc)], w_sems)

    @pl.when(pl.program_id(1) == 0)
    def _():
        for s in moved_start:
            up_scr[:, pl.ds(s, SUBLANES), :] = jnp.zeros((nc, SUBLANES, 2 * MXU_DIM), F32)

    x = jnp.concatenate([x_ref[:, pl.ds(v * d, d)] for v in range(STRIDE)], axis=0)
    inv_rms = lax.rsqrt(jnp.mean(x * x, axis=-1, keepdims=True) + EPS)
    h = (x * nw_ref[...]).astype(BF16)

    def pair(ref, c):
        return jnp.concatenate([ref[:, pl.ds(half + c * MXU_DIM, MXU_DIM)] for half in (0, d_ff)], axis=1)

    def up(c):
        r = jnp.dot(h, pair(wup_ref, c), preferred_element_type=F32) * inv_rms
        up_scr[c, pl.ds(0, tm), :] = r
        for k, s in enumerate(moved_start):
            v = STRIDE - 2 + k
            up_scr[c, pl.ds(s + 1, m), :] = r[v * m:(v + 1) * m]

    def gate(c, token):
        cw = pair(cw_ref, c)
        first = [v * m for v in range(STRIDE)]
        s1 = moved_start[1:] + first[:STRIDE - 1]
        s2 = moved_start + first[:STRIDE - 2]
        for v in range(STRIDE):
            for r in range(0, m, GATE_ROWS):
                z = _chained(up_scr[c, pl.ds(first[v] + r, GATE_ROWS), :], token, MXU_DIM)
                y = (cw[0:1, :] * up_scr[c, pl.ds(s2[v] + r, GATE_ROWS), :]
                     + cw[1:2, :] * up_scr[c, pl.ds(s1[v] + r, GATE_ROWS), :] + cw[2:3, :] * z)
                g, u = y[:, :MXU_DIM], y[:, MXU_DIM:]
                a = (g / (1.0 + jnp.exp2(g * -LOG2_E)) * u).astype(BF16)
                a_scr[c, pl.ds(v * m + r, GATE_ROWS), :] = a
                token = _zero_tile(a)
        return token

    token = None
    for c in range(nc):
        up(c)
        if c >= 1:
            token = gate(c - 1, token)
    gate(nc - 1, token)

    split = (nc - 1) * MXU_DIM
    a_head = jnp.concatenate([a_scr[c] for c in range(nc - 1)], axis=1)
    y = x + jnp.dot(a_head, wd_ref[pl.ds(0, split), :], preferred_element_type=F32)
    y = y + jnp.dot(a_scr[nc - 1], wd_ref[pl.ds(split, MXU_DIM), :], preferred_element_type=F32)

    for s in moved_start:
        up_scr[:, pl.ds(s, 1), :] = up_scr[:, pl.ds(s + m, 1), :]

    for v in range(STRIDE):
        o_ref[:, pl.ds(v * d, d)] = y[v * m:(v + 1) * m]


def _resident(shape):
    return pl.BlockSpec(shape, lambda b, i: (0,) * len(shape), pipeline_mode=pl.Buffered(1))


def _tile(tm, width):
    return pl.BlockSpec((None, tm, width), lambda b, i: (b, i, 0))


def _params(vmem_limit_bytes=VMEM_LIMIT_BYTES):
    return pltpu.CompilerParams(dimension_semantics=("arbitrary", "arbitrary"), vmem_limit_bytes=vmem_limit_bytes)


def _layer(x, attn_norm_w, w_in, q_norm_w, k_norm_w, sinks, conv_mix_w, attn_out_norm_w,
           conv_out_norm_w, w_out, ffn_norm_w, w_up, ffn_conv_w, w_down):
    bsz, seq, d = x.shape
    d_ff = w_down.shape[0]
    tm = TOKEN_TILE
    assert seq % tm == 0 and tm % BLOCK == 0 and d_ff % MXU_DIM == 0
    n_tiles = seq // tm
    nc = d_ff // MXU_DIM

    scale = HEAD_DIM ** -0.5
    qkw = jnp.concatenate([jnp.tile(q_norm_w, N_Q_HEADS) * scale, jnp.tile(k_norm_w, N_KV_HEADS)])[None, :]
    seg = jnp.arange(MXU_DIM) // HEAD_DIM
    bd = (seg[:, None] == seg[None, :]).astype(BF16)
    in_hbm = pl.BlockSpec(memory_space=pl.ANY)


    def this_tile(width):
        return pl.BlockSpec((None, tm, width), lambda b, i: (b, jnp.minimum(i, n_tiles - 1), 0))

    def last_tile(width):
        return pl.BlockSpec((None, tm, width), lambda b, i: (b, jnp.maximum(i - 1, 0), 0))

    act = jax.ShapeDtypeStruct((bsz, seq, 4 * LANES), BF16)
    in_w = w_in.shape[1]
    qn, kk, vv, cn = pl.pallas_call(
        _in_proj_kernel,
        grid=(bsz, n_tiles + 1),
        in_specs=[this_tile(d), _resident((1, d)), in_hbm,
                  _resident((1, ATTN_WIDTH + KV_WIDTH)), _resident((MXU_DIM, MXU_DIM)),
                  _resident((CONV_TAPS, CONV_WIDTH)), _resident((1, CONV_WIDTH))],
        out_specs=[last_tile(4 * LANES)] * 4,
        out_shape=[act] * 4,
        scratch_shapes=[pltpu.VMEM((tm, in_w), F32), pltpu.VMEM((tm, in_w), F32),
                        pltpu.VMEM((tm + HALO, CONV_WIDTH), F32),
                        pltpu.VMEM((d, in_w), BF16), pltpu.SemaphoreType.DMA((2,))],
        compiler_params=_params(),
        name="in_proj",
    )(x, attn_norm_w[None, :], w_in, qkw, bd, conv_mix_w, conv_out_norm_w[None, :])

    bpt = tm // BLOCK
    prev = pl.BlockSpec((None, BLOCK, 4 * LANES), lambda b, i: (b, jnp.maximum(i * bpt - 1, 0), 0))
    x = pl.pallas_call(
        _attn_kernel,
        grid=(bsz, n_tiles),
        in_specs=[pl.BlockSpec(memory_space=pltpu.SMEM),
                  _tile(tm, 4 * LANES), _tile(tm, 4 * LANES), prev, _tile(tm, 4 * LANES), prev,
                  _tile(tm, 4 * LANES), _tile(tm, d), in_hbm, _resident((1, ATTN_WIDTH))],
        out_specs=_tile(tm, d),
        out_shape=jax.ShapeDtypeStruct(x.shape, x.dtype),
        scratch_shapes=[pltpu.VMEM((bpt, 2 * N_KV_HEADS, 2 * BLOCK, 2 * BLOCK), F32),
                        pltpu.VMEM((bpt, 2 * N_KV_HEADS, 2 * BLOCK, 2 * BLOCK), BF16),
                        pltpu.VMEM((bpt, 2 * N_KV_HEADS, 2 * BLOCK, LANES), F32),
                        pltpu.VMEM((tm, ATTN_WIDTH), F32),
                        pltpu.VMEM((2, BLOCK, 2 * BLOCK), jnp.int32),
                        pltpu.VMEM((ATTN_WIDTH + CONV_WIDTH, d), BF16),
                        pltpu.SemaphoreType.DMA((bpt * 2 * N_KV_HEADS,))],
        compiler_params=_params(),
        name="attn_out_proj",
    )(sinks, qn, kk, kk, vv, vv, cn, x, w_out, attn_out_norm_w[None, :])

    strided = (bsz, seq // STRIDE, STRIDE * d)
    x = pl.pallas_call(
        _ffn_kernel,
        grid=(bsz, n_tiles),
        in_specs=[_tile(tm // STRIDE, STRIDE * d), _resident((1, d)), in_hbm, _resident((CONV_TAPS, 2 * d_ff)),
                  in_hbm],
        out_specs=_tile(tm // STRIDE, STRIDE * d),
        out_shape=jax.ShapeDtypeStruct(strided, x.dtype),
        scratch_shapes=[pltpu.VMEM((nc, tm + 2 * (tm // STRIDE + HALO), 2 * MXU_DIM), F32),
                        pltpu.VMEM((nc, tm, MXU_DIM), BF16),
                        pltpu.VMEM((d, 2 * d_ff), BF16), pltpu.VMEM((d_ff, d), BF16),
                        pltpu.SemaphoreType.DMA((nc,))],
        compiler_params=_params(),
        name="conv_ffn",
    )(x.reshape(strided), ffn_norm_w[None, :], w_up, ffn_conv_w, w_down)
    return x.reshape(bsz, seq, d)


def kernel(x, attn_norm_w, w_in, q_norm_w, k_norm_w, sinks, conv_mix_w, attn_out_norm_w,
           conv_out_norm_w, w_out, ffn_norm_w, w_up, ffn_conv_w, w_down):
    for l in range(attn_norm_w.shape[0]):
        x = _layer(x, attn_norm_w[l], w_in[l], q_norm_w[l], k_norm_w[l], sinks[l], conv_mix_w[l],
                   attn_out_norm_w[l], conv_out_norm_w[l], w_out[l], ffn_norm_w[l], w_up[l],
                   ffn_conv_w[l], w_down[l])
    return x
```

```python
import jax
import jax.numpy as jnp
from jax import lax
from jax.experimental import pallas as pl
from jax.experimental.pallas import tpu as pltpu

HEAD_DIM = 64
N_Q_HEADS = 8
N_KV_HEADS = 2
WINDOW = 128
BLOCK = 128
ATTN_WIDTH = N_Q_HEADS * HEAD_DIM
KV_WIDTH = N_KV_HEADS * HEAD_DIM
CONV_WIDTH = 512
CONV_TAPS = 3
EPS = 1e-6
NEG_INF = -1e30
LOG2_E = 1.4426950408889634

LANES = 128
SUBLANES = 8
MXU_DIM = 256
VMEM_LIMIT_BYTES = 56 * 1024 * 1024

TOKEN_TILE = 512
HALO = SUBLANES
STRIDE = 4
GATE_ROWS = 64
PIECE_ROWS = 32
SOFTMAX_ROWS = 16

F32 = jnp.float32
BF16 = jnp.bfloat16


def _rms(x, w):
    ms = jnp.mean(x * x, axis=-1, keepdims=True)
    return x * lax.rsqrt(ms + EPS) * w


def _zero_tile(packed):
    words = pltpu.bitcast(packed, jnp.uint32)
    while words.shape[1] > LANES:
        half = words.shape[1] // 2
        words = words[:, :half] | words[:, half:]
    while words.shape[0] > SUBLANES:
        half = words.shape[0] // 2
        words = words[:half] | words[half:]
    return ((words >> 16) >> 16).astype(F32)


def _chained(x, zero_tile, width=LANES):
    if zero_tile is None:
        return x
    zeros = jnp.concatenate([jnp.concatenate([zero_tile] * (x.shape[0] // SUBLANES), axis=0)] * (width // LANES),
                            axis=1)
    if width == x.shape[1]:
        return x + zeros
    return jnp.concatenate([x[:, :width] + zeros, x[:, width:]], axis=1)


def _fetch_bf16(w_hbm, w_bf16, slots, sems):
    rows, cols = slots[0].shape
    blocks = [(r, c) for r in range(0, w_hbm.shape[0], rows) for c in range(0, w_hbm.shape[1], cols)]
    assert w_hbm.shape[0] % rows == 0 and w_hbm.shape[1] % cols == 0

    def copy(b):
        r, c = blocks[b]
        s = b % len(slots)
        return pltpu.make_async_copy(w_hbm.at[pl.ds(r, rows), pl.ds(c, cols)], slots[s], sems.at[s])

    for b in range(min(len(slots), len(blocks))):
        copy(b).start()
    for b, (r, c) in enumerate(blocks):
        copy(b).wait()
        w_bf16[pl.ds(r, rows), pl.ds(c, cols)] = slots[b % len(slots)][...].astype(BF16)
        if b + len(slots) < len(blocks):
            copy(b + len(slots)).start()


def _first_step():
    return (pl.program_id(0) == 0) & (pl.program_id(1) == 0)


def _in_proj_kernel(x_ref, nw_ref, win_hbm, qkw_ref, bd_ref, cw_ref, cnw_ref,
                    q_ref, kk_ref, vv_ref, cn_ref, proj_a, proj_b, cu_scr, win_ref, w_sems):
    tm = x_ref.shape[0]
    qkv_w = ATTN_WIDTH + 2 * KV_WIDTH
    step = pl.program_id(1)

    @pl.when(_first_step())
    def _():
        _fetch_bf16(win_hbm, win_ref, [proj_a, proj_b], w_sems)

    @pl.when(step == 0)
    def _():
        proj_b[...] = jnp.zeros(proj_b.shape, F32)
        cu_scr[pl.ds(0, HALO), :] = jnp.zeros((HALO, CONV_WIDTH), F32)

    def body(fill, drain):
        def variants(src, dst):
            low = lax.broadcasted_iota(jnp.int32, src.shape, 1) < HEAD_DIM
            swapped = pltpu.roll(src, HEAD_DIM, axis=1)
            dst[...] = jnp.concatenate([jnp.where(low, src, 0.0), jnp.where(low, 0.0, swapped),
                                        jnp.where(low, swapped, 0.0), jnp.where(low, 0.0, src)],
                                       axis=1).astype(BF16)

        h = _rms(x_ref[...], nw_ref[...]).astype(BF16)
        fill[...] = jnp.dot(h, win_ref[...], preferred_element_type=F32)

        qk = drain[:, pl.ds(0, ATTN_WIDTH + KV_WIDTH)]
        low = lax.broadcasted_iota(jnp.int32, (tm, LANES), 1) < HEAD_DIM
        sums = []
        for t in range(0, ATTN_WIDTH + KV_WIDTH, LANES):
            sq = qk[:, t:t + LANES] * qk[:, t:t + LANES]
            sums.append(jnp.where(low, jnp.sum(jnp.where(low, sq, 0.0), axis=-1, keepdims=True),
                                  jnp.sum(jnp.where(low, 0.0, sq), axis=-1, keepdims=True)))
        ssq = jnp.concatenate(sums, axis=1)
        qkn = qk * lax.rsqrt(ssq * (1.0 / HEAD_DIM) + EPS) * qkw_ref[...]
        q_ref[...] = qkn[:, :ATTN_WIDTH].astype(BF16)
        variants(qkn[:, ATTN_WIDTH:], kk_ref)
        variants(drain[:, pl.ds(ATTN_WIDTH + KV_WIDTH, KV_WIDTH)], vv_ref)

        cw = cw_ref[...]
        token = None
        for r in range(0, tm, PIECE_ROWS):
            rows = pl.ds(r, PIECE_ROWS)
            bg = _chained(drain[rows, pl.ds(qkv_w, CONV_WIDTH)], token)
            cu_scr[pl.ds(HALO + r, PIECE_ROWS), :] = (drain[rows, pl.ds(qkv_w + CONV_WIDTH, CONV_WIDTH)]
                                                      * drain[rows, pl.ds(qkv_w + 2 * CONV_WIDTH, CONV_WIDTH)])
            y = cw[0:1, :] * cu_scr[pl.ds(HALO + r - 2, PIECE_ROWS), :]
            y = y + cw[1:2, :] * cu_scr[pl.ds(HALO + r - 1, PIECE_ROWS), :]
            y = y + cw[2:3, :] * cu_scr[pl.ds(HALO + r, PIECE_ROWS), :]
            cn = _rms(bg * y, cnw_ref[...]).astype(BF16)
            cn_ref[rows, :] = cn
            token = _zero_tile(cn)
        cu_scr[pl.ds(0, HALO), :] = cu_scr[pl.ds(tm, HALO), :]

    @pl.when(step % 2 == 0)
    def _():
        body(proj_a, proj_b)

    @pl.when(step % 2 == 1)
    def _():
        body(proj_b, proj_a)


def _attn_kernel(sinks_ref, q_ref, kc_ref, kp_ref, vc_ref, vp_ref, cn_ref, x_ref, wo_hbm, anw_ref,
                 o_ref, s_scr, p_scr, inv_scr, attn_scr, mask_scr, wo_ref, w_sems):
    tq = q_ref.shape[0]
    nb = tq // BLOCK
    group_pairs = N_Q_HEADS // N_KV_HEADS // 2
    nt = (((1,), (1,)), ((), ()))

    @pl.when(_first_step())
    def _():
        _fetch_bf16(wo_hbm, wo_ref, [s_scr.at[j, ce] for j in range(s_scr.shape[0]) for ce in range(s_scr.shape[1])],
                    w_sems)

    qi = lax.broadcasted_iota(jnp.int32, (BLOCK, 2 * BLOCK), 0)
    sj = lax.broadcasted_iota(jnp.int32, (BLOCK, 2 * BLOCK), 1)
    rel = qi + BLOCK - sj
    band = (rel >= 0) & (rel < WINDOW)
    band0 = band & ((sj >= BLOCK) | (pl.program_id(1) > 0))
    mask_scr[0] = band0.astype(jnp.int32)
    mask_scr[1] = band.astype(jnp.int32)
    low = lax.broadcasted_iota(jnp.int32, (2 * BLOCK, LANES), 1) < HEAD_DIM

    def window(cur_ref, prev_ref, j, cols):
        if j == 0:
            return jnp.concatenate([prev_ref[:, cols], cur_ref[pl.ds(0, BLOCK), cols]], axis=0)
        return cur_ref[pl.ds((j - 1) * BLOCK, 2 * BLOCK), cols]

    def scores(j):
        rows = pl.ds(j * BLOCK, BLOCK)
        for c in range(N_KV_HEADS):
            qs = jnp.concatenate(
                [q_ref[rows, pl.ds((group_pairs * c + g) * LANES, LANES)] for g in range(group_pairs)], axis=0)
            for e in range(2):
                k = window(kc_ref, kp_ref, j, pl.ds((2 * c + e) * LANES, LANES))
                s_scr[j, 2 * c + e] = lax.dot_general(qs, k, nt, preferred_element_type=F32)

    def softmax(j):
        for q0 in range(0, BLOCK, SOFTMAX_ROWS):
            keep = mask_scr[min(j, 1), pl.ds(q0, SOFTMAX_ROWS), :] > 0
            for ce, g in [(ce, g) for ce in range(2 * N_KV_HEADS) for g in range(group_pairs)]:
                c, e = divmod(ce, 2)
                r = g * BLOCK + q0
                rows = pl.ds(r, SOFTMAX_ROWS)
                s = jnp.where(keep, s_scr[j, ce, rows, :], NEG_INF)
                sink = sinks_ref[(N_Q_HEADS // N_KV_HEADS) * c + 2 * (r // BLOCK) + e]
                m = jnp.maximum(jnp.max(s, axis=-1, keepdims=True), sink)
                pe = jnp.exp(s - m)
                denom = jnp.sum(pe, axis=-1, keepdims=True) + jnp.exp(sink - m)
                p_scr[j, ce, rows, :] = pe.astype(BF16)
                inv_scr[j, ce, rows, :] = jnp.broadcast_to(1.0 / denom, (SOFTMAX_ROWS, LANES))

    def values(j):
        rows = pl.ds(j * BLOCK, BLOCK)
        for c in range(N_KV_HEADS):
            acc = None
            for e in range(2):
                v = window(vc_ref, vp_ref, j, pl.ds((2 * c + e) * LANES, LANES))
                pv = jnp.dot(p_scr[j, 2 * c + e], v, preferred_element_type=F32)
                acc = pv if acc is None else acc + pv
            o = acc * jnp.where(low, inv_scr[j, 2 * c], inv_scr[j, 2 * c + 1])
            for g in range(group_pairs):
                attn_scr[rows, pl.ds((group_pairs * c + g) * LANES, LANES)] = o[g * BLOCK:(g + 1) * BLOCK]

    for t in range(nb + 2):
        if t < nb:
            scores(t)
        if 0 <= t - 1 < nb:
            softmax(t - 1)
        if 0 <= t - 2 < nb:
            values(t - 2)

    an = _rms(attn_scr[...], anw_ref[...]).astype(BF16)
    y = jnp.dot(an, wo_ref[pl.ds(0, ATTN_WIDTH), :], preferred_element_type=F32)
    y = y + jnp.dot(cn_ref[...], wo_ref[pl.ds(ATTN_WIDTH, CONV_WIDTH), :], preferred_element_type=F32)
    y = x_ref[...] + y
    for k in range(o_ref.shape[0]):
        o_ref[k] = y[:, k * LANES:(k + 1) * LANES]


def _ffn_kernel(x_ref, nw_ref, wup_hbm, cw_ref, wd_hbm, o_ref, o_slab, up_scr, a_scr,
                wup_ref, wd_ref, w_sems):
    slabs, tm, _ = x_ref.shape
    m = tm // STRIDE
    d_ff = wd_ref.shape[0]
    nc = d_ff // MXU_DIM
    moved_start = [STRIDE * m + k * (m + HALO) for k in range(2)]

    @pl.when(_first_step())
    def _():
        _fetch_bf16(wup_hbm, wup_ref, [up_scr.at[c, pl.ds(0, tm), :] for c in range(nc)], w_sems)
        _fetch_bf16(wd_hbm, wd_ref, [up_scr.at[c, pl.ds(0, MXU_DIM), :] for c in range(nc)], w_sems)

    @pl.when(pl.program_id(1) == 0)
    def _():
        for s in moved_start:
            up_scr[:, pl.ds(s, SUBLANES), :] = jnp.zeros((nc, SUBLANES, 2 * MXU_DIM), F32)

    x = jnp.concatenate(
        [jnp.concatenate([x_ref[k, pl.ds(v, m, stride=STRIDE), :] for k in range(slabs)], axis=1)
         for v in range(STRIDE)], axis=0)
    inv_rms = lax.rsqrt(jnp.mean(x * x, axis=-1, keepdims=True) + EPS)
    h = (x * nw_ref[...]).astype(BF16)

    def pair(ref, c):
        return jnp.concatenate([ref[:, pl.ds(half + c * MXU_DIM, MXU_DIM)] for half in (0, d_ff)], axis=1)

    def up(c):
        r = jnp.dot(h, pair(wup_ref, c), preferred_element_type=F32) * inv_rms
        up_scr[c, pl.ds(0, tm), :] = r
        for k, s in enumerate(moved_start):
            v = STRIDE - 2 + k
            up_scr[c, pl.ds(s + 1, m), :] = r[v * m:(v + 1) * m]

    def gate(c, token):
        cw = pair(cw_ref, c)
        first = [v * m for v in range(STRIDE)]
        s1 = moved_start[1:] + first[:STRIDE - 1]
        s2 = moved_start + first[:STRIDE - 2]
        for v in range(STRIDE):
            for r in range(0, m, GATE_ROWS):
                z = _chained(up_scr[c, pl.ds(first[v] + r, GATE_ROWS), :], token, MXU_DIM)
                y = (cw[0:1, :] * up_scr[c, pl.ds(s2[v] + r, GATE_ROWS), :]
                     + cw[1:2, :] * up_scr[c, pl.ds(s1[v] + r, GATE_ROWS), :] + cw[2:3, :] * z)
                g, u = y[:, :MXU_DIM], y[:, MXU_DIM:]
                a = (g / (1.0 + jnp.exp2(g * -LOG2_E)) * u).astype(BF16)
                a_scr[c, pl.ds(v * m + r, GATE_ROWS), :] = a
                token = _zero_tile(a)
        return token

    token = None
    for c in range(nc):
        up(c)
        if c >= 1:
            token = gate(c - 1, token)
    gate(nc - 1, token)

    split = (nc - 1) * MXU_DIM
    a_head = jnp.concatenate([a_scr[c] for c in range(nc - 1)], axis=1)
    y = x + jnp.dot(a_head, wd_ref[pl.ds(0, split), :], preferred_element_type=F32)
    y = y + jnp.dot(a_scr[nc - 1], wd_ref[pl.ds(split, MXU_DIM), :], preferred_element_type=F32)

    for s in moved_start:
        up_scr[:, pl.ds(s, 1), :] = up_scr[:, pl.ds(s + m, 1), :]

    for k in range(slabs):
        for v in range(STRIDE):
            o_slab[k, pl.ds(v, m, stride=STRIDE), :] = y[v * m:(v + 1) * m, k * LANES:(k + 1) * LANES]
        o_ref[:, pl.ds(k * LANES, LANES)] = o_slab[k]


def _resident(shape):
    return pl.BlockSpec(shape, lambda b, i: (0,) * len(shape), pipeline_mode=pl.Buffered(1))


def _tile(tm, width):
    return pl.BlockSpec((None, tm, width), lambda b, i: (b, i, 0))


def _params(vmem_limit_bytes=VMEM_LIMIT_BYTES):
    return pltpu.CompilerParams(dimension_semantics=("arbitrary", "arbitrary"), vmem_limit_bytes=vmem_limit_bytes)


def _layer(x, attn_norm_w, w_in, q_norm_w, k_norm_w, sinks, conv_mix_w, attn_out_norm_w,
           conv_out_norm_w, w_out, ffn_norm_w, w_up, ffn_conv_w, w_down):
    bsz, seq, d = x.shape
    d_ff = w_down.shape[0]
    tm = TOKEN_TILE
    assert seq % tm == 0 and tm % BLOCK == 0 and d_ff % MXU_DIM == 0
    n_tiles = seq // tm
    nc = d_ff // MXU_DIM

    scale = HEAD_DIM ** -0.5
    qkw = jnp.concatenate([jnp.tile(q_norm_w, N_Q_HEADS) * scale, jnp.tile(k_norm_w, N_KV_HEADS)])[None, :]
    seg = jnp.arange(MXU_DIM) // HEAD_DIM
    bd = (seg[:, None] == seg[None, :]).astype(BF16)
    in_hbm = pl.BlockSpec(memory_space=pl.ANY)


    def this_tile(width):
        return pl.BlockSpec((None, tm, width), lambda b, i: (b, jnp.minimum(i, n_tiles - 1), 0))

    def last_tile(width):
        return pl.BlockSpec((None, tm, width), lambda b, i: (b, jnp.maximum(i - 1, 0), 0))

    act = jax.ShapeDtypeStruct((bsz, seq, 4 * LANES), BF16)
    in_w = w_in.shape[1]
    qn, kk, vv, cn = pl.pallas_call(
        _in_proj_kernel,
        grid=(bsz, n_tiles + 1),
        in_specs=[this_tile(d), _resident((1, d)), in_hbm,
                  _resident((1, ATTN_WIDTH + KV_WIDTH)), _resident((MXU_DIM, MXU_DIM)),
                  _resident((CONV_TAPS, CONV_WIDTH)), _resident((1, CONV_WIDTH))],
        out_specs=[last_tile(4 * LANES)] * 4,
        out_shape=[act] * 4,
        scratch_shapes=[pltpu.VMEM((tm, in_w), F32), pltpu.VMEM((tm, in_w), F32),
                        pltpu.VMEM((tm + HALO, CONV_WIDTH), F32),
                        pltpu.VMEM((d, in_w), BF16), pltpu.SemaphoreType.DMA((2,))],
        compiler_params=_params(),
        name="in_proj",
    )(x, attn_norm_w[None, :], w_in, qkw, bd, conv_mix_w, conv_out_norm_w[None, :])

    bpt = tm // BLOCK
    slab_tile = pl.BlockSpec((None, d // LANES, tm, LANES), lambda b, i: (b, 0, i, 0))
    prev = pl.BlockSpec((None, BLOCK, 4 * LANES), lambda b, i: (b, jnp.maximum(i * bpt - 1, 0), 0))
    x = pl.pallas_call(
        _attn_kernel,
        grid=(bsz, n_tiles),
        in_specs=[pl.BlockSpec(memory_space=pltpu.SMEM),
                  _tile(tm, 4 * LANES), _tile(tm, 4 * LANES), prev, _tile(tm, 4 * LANES), prev,
                  _tile(tm, 4 * LANES), _tile(tm, d), in_hbm, _resident((1, ATTN_WIDTH))],
        out_specs=slab_tile,
        out_shape=jax.ShapeDtypeStruct((bsz, d // LANES, seq, LANES), x.dtype),
        scratch_shapes=[pltpu.VMEM((bpt, 2 * N_KV_HEADS, 2 * BLOCK, 2 * BLOCK), F32),
                        pltpu.VMEM((bpt, 2 * N_KV_HEADS, 2 * BLOCK, 2 * BLOCK), BF16),
                        pltpu.VMEM((bpt, 2 * N_KV_HEADS, 2 * BLOCK, LANES), F32),
                        pltpu.VMEM((tm, ATTN_WIDTH), F32),
                        pltpu.VMEM((2, BLOCK, 2 * BLOCK), jnp.int32),
                        pltpu.VMEM((ATTN_WIDTH + CONV_WIDTH, d), BF16),
                        pltpu.SemaphoreType.DMA((bpt * 2 * N_KV_HEADS,))],
        compiler_params=_params(),
        name="attn_out_proj",
    )(sinks, qn, kk, kk, vv, vv, cn, x, w_out, attn_out_norm_w[None, :])

    x = pl.pallas_call(
        _ffn_kernel,
        grid=(bsz, n_tiles),
        in_specs=[slab_tile, _resident((1, d)), in_hbm, _resident((CONV_TAPS, 2 * d_ff)), in_hbm],
        out_specs=_tile(tm, d),
        out_shape=jax.ShapeDtypeStruct((bsz, seq, d), x.dtype),
        scratch_shapes=[pltpu.VMEM((d // LANES, tm, LANES), F32),
                        pltpu.VMEM((nc, tm + 2 * (tm // STRIDE + HALO), 2 * MXU_DIM), F32),
                        pltpu.VMEM((nc, tm, MXU_DIM), BF16),
                        pltpu.VMEM((d, 2 * d_ff), BF16), pltpu.VMEM((d_ff, d), BF16),
                        pltpu.SemaphoreType.DMA((nc,))],
        compiler_params=_params(),
        name="conv_ffn",
    )(x, ffn_norm_w[None, :], w_up, ffn_conv_w, w_down)
    return x


def kernel(x, attn_norm_w, w_in, q_norm_w, k_norm_w, sinks, conv_mix_w, attn_out_norm_w,
           conv_out_norm_w, w_out, ffn_norm_w, w_up, ffn_conv_w, w_down):
    for l in range(attn_norm_w.shape[0]):
        x = _layer(x, attn_norm_w[l], w_in[l], q_norm_w[l], k_norm_w[l], sinks[l], conv_mix_w[l],
                   attn_out_norm_w[l], conv_out_norm_w[l], w_out[l], ffn_norm_w[l], w_up[l],
                   ffn_conv_w[l], w_down[l])
    return x
```

```python
import jax
import jax.numpy as jnp
from jax import lax
from jax.experimental import pallas as pl
from jax.experimental.pallas import tpu as pltpu

HEAD_DIM = 64
N_Q_HEADS = 8
N_KV_HEADS = 2
WINDOW = 128
BLOCK = 128
ATTN_WIDTH = N_Q_HEADS * HEAD_DIM
KV_WIDTH = N_KV_HEADS * HEAD_DIM
CONV_WIDTH = 512
CONV_TAPS = 3
EPS = 1e-6
NEG_INF = -1e30
LOG2_E = 1.4426950408889634

LANES = 128
SUBLANES = 8
MXU_DIM = 256
VMEM_LIMIT_BYTES = 56 * 1024 * 1024

TOKEN_TILE = 512
HALO = SUBLANES
STRIDE = 4
GATE_ROWS = 64
QKV_ROWS = 64
PIECE_ROWS = 32
PACKED_ROWS = 2 * SUBLANES

F32 = jnp.float32
BF16 = jnp.bfloat16


def _rms(x, w):
    ms = jnp.mean(x * x, axis=-1, keepdims=True)
    return x * lax.rsqrt(ms + EPS) * w


def _zero_tile(packed):
    words = pltpu.bitcast(packed, jnp.uint32)
    while words.shape[1] > LANES:
        half = words.shape[1] // 2
        words = words[:, :half] | words[:, half:]
    while words.shape[0] > SUBLANES:
        half = words.shape[0] // 2
        words = words[:half] | words[half:]
    return ((words >> 16) >> 16).astype(F32)


def _chained(x, zero_tile, width=LANES):
    if zero_tile is None:
        return x
    zeros = jnp.concatenate([jnp.concatenate([zero_tile] * (x.shape[0] // SUBLANES), axis=0)] * (width // LANES),
                            axis=1)
    if width == x.shape[1]:
        return x + zeros
    return jnp.concatenate([x[:, :width] + zeros, x[:, width:]], axis=1)


def _fetch_bf16(w_hbm, w_bf16, slots, sems):
    rows, cols = slots[0].shape
    blocks = [(r, c) for r in range(0, w_hbm.shape[0], rows) for c in range(0, w_hbm.shape[1], cols)]
    assert w_hbm.shape[0] % rows == 0 and w_hbm.shape[1] % cols == 0

    def copy(b):
        r, c = blocks[b]
        s = b % len(slots)
        return pltpu.make_async_copy(w_hbm.at[pl.ds(r, rows), pl.ds(c, cols)], slots[s], sems.at[s])

    for b in range(min(len(slots), len(blocks))):
        copy(b).start()
    for b, (r, c) in enumerate(blocks):
        copy(b).wait()
        w_bf16[pl.ds(r, rows), pl.ds(c, cols)] = slots[b % len(slots)][...].astype(BF16)
        if b + len(slots) < len(blocks):
            copy(b + len(slots)).start()


def _first_step():
    return (pl.program_id(0) == 0) & (pl.program_id(1) == 0)


def _in_proj_kernel(x_ref, nw_ref, win_hbm, qkw_ref, cw_ref, cnw_ref,
                    q_ref, kk_ref, vt_ref, cn_ref, proj_a, proj_b, cu_scr, win_ref, w_sems):
    tm = x_ref.shape[0]
    qkv_w = ATTN_WIDTH + 2 * KV_WIDTH
    step = pl.program_id(1)

    @pl.when(_first_step())
    def _():
        _fetch_bf16(win_hbm, win_ref, [proj_a, proj_b], w_sems)

    @pl.when(step == 0)
    def _():
        proj_b[...] = jnp.zeros(proj_b.shape, F32)
        cu_scr[pl.ds(0, HALO), :] = jnp.zeros((HALO, CONV_WIDTH), F32)

    def body(fill, drain):
        h = _rms(x_ref[...], nw_ref[...]).astype(BF16)
        fill[...] = jnp.dot(h, win_ref[...], preferred_element_type=F32)

        low = lax.broadcasted_iota(jnp.int32, (QKV_ROWS, LANES), 1) < HEAD_DIM
        token = None
        for r in range(0, tm, QKV_ROWS):
            rows = pl.ds(r, QKV_ROWS)
            qk = _chained(drain[rows, pl.ds(0, ATTN_WIDTH + KV_WIDTH)], token, ATTN_WIDTH + KV_WIDTH)
            sums = []
            for t in range(0, ATTN_WIDTH + KV_WIDTH, LANES):
                sq = qk[:, t:t + LANES] * qk[:, t:t + LANES]
                sums.append(jnp.where(low, jnp.sum(jnp.where(low, sq, 0.0), axis=-1, keepdims=True),
                                      jnp.sum(jnp.where(low, 0.0, sq), axis=-1, keepdims=True)))
            qkn = qk * lax.rsqrt(jnp.concatenate(sums, axis=1) * (1.0 / HEAD_DIM) + EPS) * qkw_ref[...]
            qn = qkn[:, :ATTN_WIDTH].astype(BF16)
            q_ref[rows, :] = qn
            k = qkn[:, ATTN_WIDTH:]
            swapped = pltpu.roll(k, HEAD_DIM, axis=1)
            kk = jnp.concatenate([jnp.where(low, k, 0.0), jnp.where(low, 0.0, swapped),
                                  jnp.where(low, swapped, 0.0), jnp.where(low, 0.0, k)], axis=1).astype(BF16)
            kk_ref[rows, :] = kk
            token = _zero_tile(qn) + _zero_tile(kk)
        vt_ref[...] = drain[:, pl.ds(ATTN_WIDTH + KV_WIDTH, KV_WIDTH)].T.astype(BF16)

        cw = cw_ref[...]
        token = None
        for r in range(0, tm, PIECE_ROWS):
            rows = pl.ds(r, PIECE_ROWS)
            bg = _chained(drain[rows, pl.ds(qkv_w, CONV_WIDTH)], token)
            cu_scr[pl.ds(HALO + r, PIECE_ROWS), :] = (drain[rows, pl.ds(qkv_w + CONV_WIDTH, CONV_WIDTH)]
                                                      * drain[rows, pl.ds(qkv_w + 2 * CONV_WIDTH, CONV_WIDTH)])
            y = cw[0:1, :] * cu_scr[pl.ds(HALO + r - 2, PIECE_ROWS), :]
            y = y + cw[1:2, :] * cu_scr[pl.ds(HALO + r - 1, PIECE_ROWS), :]
            y = y + cw[2:3, :] * cu_scr[pl.ds(HALO + r, PIECE_ROWS), :]
            cn = _rms(bg * y, cnw_ref[...]).astype(BF16)
            cn_ref[rows, :] = cn
            token = _zero_tile(cn)
        cu_scr[pl.ds(0, HALO), :] = cu_scr[pl.ds(tm, HALO), :]

    @pl.when(step % 2 == 0)
    def _():
        body(proj_a, proj_b)

    @pl.when(step % 2 == 1)
    def _():
        body(proj_b, proj_a)


def _all_sublanes(x, op):
    for shift in (4, 2, 1):
        x = op(x, pltpu.roll(x, shift, axis=0))
    return x


def _attn_kernel(sinks_ref, q_ref, kc_ref, kp_ref, vtc_ref, vtp_ref, cn_ref, x_ref, wo_hbm, anw_ref,
                 o_ref, s_scr, p_scr, inv_scr, attn_t, an_scr, mask_scr, wo_ref, w_sems):
    tq = q_ref.shape[0]
    nb = tq // BLOCK
    group = N_Q_HEADS // N_KV_HEADS
    group_pairs = group // 2
    key_tiles = 2 * BLOCK // SUBLANES
    nt = (((1,), (1,)), ((), ()))

    @pl.when(_first_step())
    def _():
        _fetch_bf16(wo_hbm, wo_ref, [s_scr.at[j, ce] for j in range(s_scr.shape[0]) for ce in range(s_scr.shape[1])],
                    w_sems)

    sj = lax.broadcasted_iota(jnp.int32, (2 * BLOCK, BLOCK), 0)
    qi = lax.broadcasted_iota(jnp.int32, (2 * BLOCK, BLOCK), 1)
    rel = qi + BLOCK - sj
    band = (rel >= 0) & (rel < WINDOW)
    band0 = band & ((sj >= BLOCK) | (pl.program_id(1) > 0))
    mask_scr[0] = band0.astype(jnp.int32)
    mask_scr[1] = band.astype(jnp.int32)

    def scores(j):
        rows = pl.ds(j * BLOCK, BLOCK)
        for c in range(N_KV_HEADS):
            qs = jnp.concatenate(
                [q_ref[rows, pl.ds((group_pairs * c + g) * LANES, LANES)] for g in range(group_pairs)], axis=0)
            for e in range(2):
                cols = pl.ds((2 * c + e) * LANES, LANES)
                if j == 0:
                    k = jnp.concatenate([kp_ref[:, cols], kc_ref[pl.ds(0, BLOCK), cols]], axis=0)
                else:
                    k = kc_ref[pl.ds((j - 1) * BLOCK, 2 * BLOCK), cols]
                s_scr[j, 2 * c + e] = lax.dot_general(k, qs, nt, preferred_element_type=F32)

    def softmax(j):
        for ce, g in [(ce, g) for ce in range(2 * N_KV_HEADS) for g in range(group_pairs)]:
            c, e = divmod(ce, 2)
            cols = pl.ds(g * BLOCK, BLOCK)
            sink = sinks_ref[group * c + 2 * g + e]
            m = None
            for r in range(key_tiles):
                rows = pl.ds(r * SUBLANES, SUBLANES)
                s = jnp.where(mask_scr[min(j, 1), rows, :] > 0, s_scr[j, ce, rows, cols], NEG_INF)
                s_scr[j, ce, rows, cols] = s
                m = s if m is None else jnp.maximum(m, s)
            m = jnp.maximum(_all_sublanes(m, jnp.maximum), sink)
            m2 = jnp.concatenate([m, m], axis=0)
            total = None
            for r in range(0, 2 * BLOCK, PACKED_ROWS):
                rows = pl.ds(r, PACKED_ROWS)
                pe = jnp.exp(s_scr[j, ce, rows, cols] - m2)
                p_scr[j, ce, rows, cols] = pe.astype(BF16)
                part = pe[:SUBLANES] + pe[SUBLANES:]
                total = part if total is None else total + part
            denom = _all_sublanes(total, jnp.add) + jnp.exp(sink - m)
            inv_scr[j, ce, :, cols] = 1.0 / denom

    def values(j):
        for c in range(N_KV_HEADS):
            dims = pl.ds(c * HEAD_DIM, HEAD_DIM)
            if j == 0:
                vt = jnp.concatenate([vtp_ref[dims, :], vtc_ref[dims, pl.ds(0, BLOCK)]], axis=1)
            else:
                vt = vtc_ref[dims, pl.ds((j - 1) * BLOCK, 2 * BLOCK)]
            for e in range(2):
                inv = jnp.concatenate([inv_scr[j, 2 * c + e]] * (HEAD_DIM // SUBLANES), axis=0)
                o = jnp.dot(vt, p_scr[j, 2 * c + e], preferred_element_type=F32) * inv
                for g in range(group_pairs):
                    attn_t[pl.ds((group * c + 2 * g + e) * HEAD_DIM, HEAD_DIM), pl.ds(j * BLOCK, BLOCK)] = (
                        o[:, g * BLOCK:(g + 1) * BLOCK])
        a = attn_t[:, pl.ds(j * BLOCK, BLOCK)]
        ms = jnp.mean(a * a, axis=0, keepdims=True)
        an_scr[pl.ds(j * BLOCK, BLOCK), :] = (a * lax.rsqrt(ms + EPS) * anw_ref[...]).T.astype(BF16)

    for t in range(nb + 2):
        if t < nb:
            scores(t)
        if 0 <= t - 1 < nb:
            softmax(t - 1)
        if 0 <= t - 2 < nb:
            values(t - 2)

    y = jnp.dot(an_scr[...], wo_ref[pl.ds(0, ATTN_WIDTH), :], preferred_element_type=F32)
    y = y + jnp.dot(cn_ref[...], wo_ref[pl.ds(ATTN_WIDTH, CONV_WIDTH), :], preferred_element_type=F32)
    y = x_ref[...] + y
    for k in range(o_ref.shape[0]):
        o_ref[k] = y[:, k * LANES:(k + 1) * LANES]


def _ffn_kernel(x_ref, nw_ref, wup_hbm, cw_ref, wd_hbm, o_ref, o_slab, up_scr, a_scr,
                wup_ref, wd_ref, w_sems):
    slabs, tm, _ = x_ref.shape
    m = tm // STRIDE
    d_ff = wd_ref.shape[0]
    nc = d_ff // MXU_DIM
    moved_start = [STRIDE * m + k * (m + HALO) for k in range(2)]

    @pl.when(_first_step())
    def _():
        _fetch_bf16(wup_hbm, wup_ref, [up_scr.at[c, pl.ds(0, tm), :] for c in range(nc)], w_sems)
        _fetch_bf16(wd_hbm, wd_ref, [up_scr.at[c, pl.ds(0, MXU_DIM), :] for c in range(nc)], w_sems)

    @pl.when(pl.program_id(1) == 0)
    def _():
        for s in moved_start:
            up_scr[:, pl.ds(s, SUBLANES), :] = jnp.zeros((nc, SUBLANES, 2 * MXU_DIM), F32)

    x = jnp.concatenate(
        [jnp.concatenate([x_ref[k, pl.ds(v, m, stride=STRIDE), :] for k in range(slabs)], axis=1)
         for v in range(STRIDE)], axis=0)
    inv_rms = lax.rsqrt(jnp.mean(x * x, axis=-1, keepdims=True) + EPS)
    h = (x * nw_ref[...]).astype(BF16)

    def pair(ref, c):
        return jnp.concatenate([ref[:, pl.ds(half + c * MXU_DIM, MXU_DIM)] for half in (0, d_ff)], axis=1)

    def up(c):
        r = jnp.dot(h, pair(wup_ref, c), preferred_element_type=F32) * inv_rms
        up_scr[c, pl.ds(0, tm), :] = r
        for k, s in enumerate(moved_start):
            v = STRIDE - 2 + k
            up_scr[c, pl.ds(s + 1, m), :] = r[v * m:(v + 1) * m]

    def gate(c, token):
        cw = pair(cw_ref, c)
        first = [v * m for v in range(STRIDE)]
        s1 = moved_start[1:] + first[:STRIDE - 1]
        s2 = moved_start + first[:STRIDE - 2]
        for v in range(STRIDE):
            for r in range(0, m, GATE_ROWS):
                z = _chained(up_scr[c, pl.ds(first[v] + r, GATE_ROWS), :], token, MXU_DIM)
                y = (cw[0:1, :] * up_scr[c, pl.ds(s2[v] + r, GATE_ROWS), :]
                     + cw[1:2, :] * up_scr[c, pl.ds(s1[v] + r, GATE_ROWS), :] + cw[2:3, :] * z)
                g, u = y[:, :MXU_DIM], y[:, MXU_DIM:]
                a = (g / (1.0 + jnp.exp2(g * -LOG2_E)) * u).astype(BF16)
                a_scr[c, pl.ds(v * m + r, GATE_ROWS), :] = a
                token = _zero_tile(a)
        return token

    token = None
    for c in range(nc):
        up(c)
        if c >= 1:
            token = gate(c - 1, token)
    gate(nc - 1, token)

    split = (nc - 1) * MXU_DIM
    a_head = jnp.concatenate([a_scr[c] for c in range(nc - 1)], axis=1)
    y = x + jnp.dot(a_head, wd_ref[pl.ds(0, split), :], preferred_element_type=F32)
    y = y + jnp.dot(a_scr[nc - 1], wd_ref[pl.ds(split, MXU_DIM), :], preferred_element_type=F32)

    for s in moved_start:
        up_scr[:, pl.ds(s, 1), :] = up_scr[:, pl.ds(s + m, 1), :]

    for k in range(slabs):
        for v in range(STRIDE):
            o_slab[k, pl.ds(v, m, stride=STRIDE), :] = y[v * m:(v + 1) * m, k * LANES:(k + 1) * LANES]
        o_ref[:, pl.ds(k * LANES, LANES)] = o_slab[k]


def _resident(shape):
    return pl.BlockSpec(shape, lambda b, i: (0,) * len(shape), pipeline_mode=pl.Buffered(1))


def _tile(tm, width):
    return pl.BlockSpec((None, tm, width), lambda b, i: (b, i, 0))


def _params(vmem_limit_bytes=VMEM_LIMIT_BYTES):
    return pltpu.CompilerParams(dimension_semantics=("arbitrary", "arbitrary"), vmem_limit_bytes=vmem_limit_bytes)


def _layer(x, attn_norm_w, w_in, q_norm_w, k_norm_w, sinks, conv_mix_w, attn_out_norm_w,
           conv_out_norm_w, w_out, ffn_norm_w, w_up, ffn_conv_w, w_down):
    bsz, seq, d = x.shape
    d_ff = w_down.shape[0]
    tm = TOKEN_TILE
    assert seq % tm == 0 and tm % BLOCK == 0 and d_ff % MXU_DIM == 0
    n_tiles = seq // tm
    nc = d_ff // MXU_DIM

    scale = HEAD_DIM ** -0.5
    qkw = jnp.concatenate([jnp.tile(q_norm_w, N_Q_HEADS) * scale, jnp.tile(k_norm_w, N_KV_HEADS)])[None, :]
    in_hbm = pl.BlockSpec(memory_space=pl.ANY)


    def this_tile(width):
        return pl.BlockSpec((None, tm, width), lambda b, i: (b, jnp.minimum(i, n_tiles - 1), 0))

    def last_tile(width):
        return pl.BlockSpec((None, tm, width), lambda b, i: (b, jnp.maximum(i - 1, 0), 0))

    act = jax.ShapeDtypeStruct((bsz, seq, 4 * LANES), BF16)
    in_w = w_in.shape[1]
    v_t = jax.ShapeDtypeStruct((bsz, KV_WIDTH, seq), BF16)
    last_tile_t = pl.BlockSpec((None, KV_WIDTH, tm), lambda b, i: (b, 0, jnp.maximum(i - 1, 0)))
    qn, kk, vt, cn = pl.pallas_call(
        _in_proj_kernel,
        grid=(bsz, n_tiles + 1),
        in_specs=[this_tile(d), _resident((1, d)), in_hbm,
                  _resident((1, ATTN_WIDTH + KV_WIDTH)),
                  _resident((CONV_TAPS, CONV_WIDTH)), _resident((1, CONV_WIDTH))],
        out_specs=[last_tile(4 * LANES), last_tile(4 * LANES), last_tile_t, last_tile(4 * LANES)],
        out_shape=[act, act, v_t, act],
        scratch_shapes=[pltpu.VMEM((tm, in_w), F32), pltpu.VMEM((tm, in_w), F32),
                        pltpu.VMEM((tm + HALO, CONV_WIDTH), F32),
                        pltpu.VMEM((d, in_w), BF16), pltpu.SemaphoreType.DMA((2,))],
        compiler_params=_params(),
        name="in_proj",
    )(x, attn_norm_w[None, :], w_in, qkw, conv_mix_w, conv_out_norm_w[None, :])

    bpt = tm // BLOCK
    slab_tile = pl.BlockSpec((None, d // LANES, tm, LANES), lambda b, i: (b, 0, i, 0))
    prev = pl.BlockSpec((None, BLOCK, 4 * LANES), lambda b, i: (b, jnp.maximum(i * bpt - 1, 0), 0))
    tile_t = pl.BlockSpec((None, KV_WIDTH, tm), lambda b, i: (b, 0, i))
    prev_t = pl.BlockSpec((None, KV_WIDTH, BLOCK), lambda b, i: (b, 0, jnp.maximum(i * bpt - 1, 0)))
    x = pl.pallas_call(
        _attn_kernel,
        grid=(bsz, n_tiles),
        in_specs=[pl.BlockSpec(memory_space=pltpu.SMEM),
                  _tile(tm, 4 * LANES), _tile(tm, 4 * LANES), prev, tile_t, prev_t,
                  _tile(tm, 4 * LANES), _tile(tm, d), in_hbm, _resident((ATTN_WIDTH, 1))],
        out_specs=slab_tile,
        out_shape=jax.ShapeDtypeStruct((bsz, d // LANES, seq, LANES), x.dtype),
        scratch_shapes=[pltpu.VMEM((bpt, 2 * N_KV_HEADS, 2 * BLOCK, 2 * BLOCK), F32),
                        pltpu.VMEM((bpt, 2 * N_KV_HEADS, 2 * BLOCK, 2 * BLOCK), BF16),
                        pltpu.VMEM((bpt, 2 * N_KV_HEADS, SUBLANES, 2 * BLOCK), F32),
                        pltpu.VMEM((ATTN_WIDTH, tm), F32),
                        pltpu.VMEM((tm, ATTN_WIDTH), BF16),
                        pltpu.VMEM((2, 2 * BLOCK, BLOCK), jnp.int32),
                        pltpu.VMEM((ATTN_WIDTH + CONV_WIDTH, d), BF16),
                        pltpu.SemaphoreType.DMA((bpt * 2 * N_KV_HEADS,))],
        compiler_params=_params(),
        name="attn_out_proj",
    )(sinks, qn, kk, kk, vt, vt, cn, x, w_out, attn_out_norm_w[:, None])

    x = pl.pallas_call(
        _ffn_kernel,
        grid=(bsz, n_tiles),
        in_specs=[slab_tile, _resident((1, d)), in_hbm, _resident((CONV_TAPS, 2 * d_ff)), in_hbm],
        out_specs=_tile(tm, d),
        out_shape=jax.ShapeDtypeStruct((bsz, seq, d), x.dtype),
        scratch_shapes=[pltpu.VMEM((d // LANES, tm, LANES), F32),
                        pltpu.VMEM((nc, tm + 2 * (tm // STRIDE + HALO), 2 * MXU_DIM), F32),
                        pltpu.VMEM((nc, tm, MXU_DIM), BF16),
                        pltpu.VMEM((d, 2 * d_ff), BF16), pltpu.VMEM((d_ff, d), BF16),
                        pltpu.SemaphoreType.DMA((nc,))],
        compiler_params=_params(),
        name="conv_ffn",
    )(x, ffn_norm_w[None, :], w_up, ffn_conv_w, w_down)
    return x


def kernel(x, attn_norm_w, w_in, q_norm_w, k_norm_w, sinks, conv_mix_w, attn_out_norm_w,
           conv_out_norm_w, w_out, ffn_norm_w, w_up, ffn_conv_w, w_down):
    for l in range(attn_norm_w.shape[0]):
        x = _layer(x, attn_norm_w[l], w_in[l], q_norm_w[l], k_norm_w[l], sinks[l], conv_mix_w[l],
                   attn_out_norm_w[l], conv_out_norm_w[l], w_out[l], ffn_norm_w[l], w_up[l],
                   ffn_conv_w[l], w_down[l])
    return x
```

```python
import jax
import jax.numpy as jnp
from jax import lax
from jax.experimental import pallas as pl
from jax.experimental.pallas import tpu as pltpu

HEAD_DIM = 64
N_Q_HEADS = 8
N_KV_HEADS = 2
WINDOW = 128
BLOCK = 128
ATTN_WIDTH = N_Q_HEADS * HEAD_DIM
KV_WIDTH = N_KV_HEADS * HEAD_DIM
CONV_WIDTH = 512
CONV_TAPS = 3
EPS = 1e-6
NEG_INF = -1e30
LOG2_E = 1.4426950408889634

LANES = 128
SUBLANES = 8
MXU_DIM = 256
VMEM_LIMIT_BYTES = 56 * 1024 * 1024

TOKEN_TILE = 512
HALO = SUBLANES
STRIDE = 4
GATE_ROWS = 64
QKV_ROWS = 64
PIECE_ROWS = 32
PACKED_ROWS = 2 * SUBLANES

F32 = jnp.float32
BF16 = jnp.bfloat16


def _rms(x, w):
    ms = jnp.mean(x * x, axis=-1, keepdims=True)
    return x * lax.rsqrt(ms + EPS) * w


def _zero_tile(packed):
    words = pltpu.bitcast(packed, jnp.uint32)
    while words.shape[1] > LANES:
        half = words.shape[1] // 2
        words = words[:, :half] | words[:, half:]
    while words.shape[0] > SUBLANES:
        half = words.shape[0] // 2
        words = words[:half] | words[half:]
    return ((words >> 16) >> 16).astype(F32)


def _chained(x, zero_tile, width=LANES):
    if zero_tile is None:
        return x
    zeros = jnp.concatenate([jnp.concatenate([zero_tile] * (x.shape[0] // SUBLANES), axis=0)] * (width // LANES),
                            axis=1)
    if width == x.shape[1]:
        return x + zeros
    return jnp.concatenate([x[:, :width] + zeros, x[:, width:]], axis=1)


def _fetch_bf16(w_hbm, w_bf16, slots, sems):
    rows, cols = slots[0].shape
    blocks = [(r, c) for r in range(0, w_hbm.shape[0], rows) for c in range(0, w_hbm.shape[1], cols)]
    assert w_hbm.shape[0] % rows == 0 and w_hbm.shape[1] % cols == 0

    def copy(b):
        r, c = blocks[b]
        s = b % len(slots)
        return pltpu.make_async_copy(w_hbm.at[pl.ds(r, rows), pl.ds(c, cols)], slots[s], sems.at[s])

    for b in range(min(len(slots), len(blocks))):
        copy(b).start()
    for b, (r, c) in enumerate(blocks):
        copy(b).wait()
        w_bf16[pl.ds(r, rows), pl.ds(c, cols)] = slots[b % len(slots)][...].astype(BF16)
        if b + len(slots) < len(blocks):
            copy(b + len(slots)).start()


def _first_step():
    return (pl.program_id(0) == 0) & (pl.program_id(1) == 0)


def _in_proj_kernel(x_ref, nw_ref, win_hbm, qkw_ref, cw_ref, cnw_ref,
                    q_ref, kk_ref, vt_ref, cn_ref, proj_a, proj_b, cu_scr, win_ref, w_sems):
    tm = x_ref.shape[0]
    qkv_w = ATTN_WIDTH + 2 * KV_WIDTH
    step = pl.program_id(1)

    @pl.when(_first_step())
    def _():
        _fetch_bf16(win_hbm, win_ref, [proj_a, proj_b], w_sems)

    @pl.when(step == 0)
    def _():
        proj_b[...] = jnp.zeros(proj_b.shape, F32)
        cu_scr[pl.ds(0, HALO), :] = jnp.zeros((HALO, CONV_WIDTH), F32)

    def body(fill, drain):
        h = _rms(x_ref[...], nw_ref[...]).astype(BF16)
        fill[...] = jnp.dot(h, win_ref[...], preferred_element_type=F32)

        low = lax.broadcasted_iota(jnp.int32, (QKV_ROWS, LANES), 1) < HEAD_DIM
        token = None
        for r in range(0, tm, QKV_ROWS):
            rows = pl.ds(r, QKV_ROWS)
            qk = _chained(drain[rows, pl.ds(0, ATTN_WIDTH + KV_WIDTH)], token, ATTN_WIDTH + KV_WIDTH)
            sums = []
            for t in range(0, ATTN_WIDTH + KV_WIDTH, LANES):
                sq = qk[:, t:t + LANES] * qk[:, t:t + LANES]
                sums.append(jnp.where(low, jnp.sum(jnp.where(low, sq, 0.0), axis=-1, keepdims=True),
                                      jnp.sum(jnp.where(low, 0.0, sq), axis=-1, keepdims=True)))
            qkn = qk * lax.rsqrt(jnp.concatenate(sums, axis=1) * (1.0 / HEAD_DIM) + EPS) * qkw_ref[...]
            qn = qkn[:, :ATTN_WIDTH].astype(BF16)
            q_ref[rows, :] = qn
            k = qkn[:, ATTN_WIDTH:]
            swapped = pltpu.roll(k, HEAD_DIM, axis=1)
            kk = jnp.concatenate([jnp.where(low, k, 0.0), jnp.where(low, 0.0, swapped),
                                  jnp.where(low, swapped, 0.0), jnp.where(low, 0.0, k)], axis=1).astype(BF16)
            kk_ref[rows, :] = kk
            token = _zero_tile(qn) + _zero_tile(kk)
        vt_ref[...] = drain[:, pl.ds(ATTN_WIDTH + KV_WIDTH, KV_WIDTH)].T.astype(BF16)

        cw = cw_ref[...]
        token = None
        for r in range(0, tm, PIECE_ROWS):
            rows = pl.ds(r, PIECE_ROWS)
            bg = _chained(drain[rows, pl.ds(qkv_w, CONV_WIDTH)], token)
            cu_scr[pl.ds(HALO + r, PIECE_ROWS), :] = (drain[rows, pl.ds(qkv_w + CONV_WIDTH, CONV_WIDTH)]
                                                      * drain[rows, pl.ds(qkv_w + 2 * CONV_WIDTH, CONV_WIDTH)])
            y = cw[0:1, :] * cu_scr[pl.ds(HALO + r - 2, PIECE_ROWS), :]
            y = y + cw[1:2, :] * cu_scr[pl.ds(HALO + r - 1, PIECE_ROWS), :]
            y = y + cw[2:3, :] * cu_scr[pl.ds(HALO + r, PIECE_ROWS), :]
            cn = _rms(bg * y, cnw_ref[...]).astype(BF16)
            cn_ref[rows, :] = cn
            token = _zero_tile(cn)
        cu_scr[pl.ds(0, HALO), :] = cu_scr[pl.ds(tm, HALO), :]

    @pl.when(step % 2 == 0)
    def _():
        body(proj_a, proj_b)

    @pl.when(step % 2 == 1)
    def _():
        body(proj_b, proj_a)


def _all_sublanes(x, op):
    for shift in (4, 2, 1):
        x = op(x, pltpu.roll(x, shift, axis=0))
    return x


def _attn_kernel(sinks_ref, q_ref, kc_ref, kp_ref, vtc_ref, vtp_ref, cn_ref, x_ref, wo_hbm, anw_ref,
                 o_ref, s_scr, p_scr, inv_scr, attn_t, an_scr, mask_scr, wo_ref, w_sems):
    tq = q_ref.shape[0]
    nb = tq // BLOCK
    group = N_Q_HEADS // N_KV_HEADS
    group_pairs = group // 2
    key_tiles = 2 * BLOCK // SUBLANES
    nt = (((1,), (1,)), ((), ()))

    @pl.when(_first_step())
    def _():
        slots = [s_scr.at[j, ce, g] for j in range(nb) for ce in range(2 * N_KV_HEADS) for g in range(group_pairs)]
        _fetch_bf16(wo_hbm, wo_ref, slots, w_sems)

    sj = lax.broadcasted_iota(jnp.int32, (2 * BLOCK, BLOCK), 0)
    qi = lax.broadcasted_iota(jnp.int32, (2 * BLOCK, BLOCK), 1)
    rel = qi + BLOCK - sj
    band = (rel >= 0) & (rel < WINDOW)
    band0 = band & ((sj >= BLOCK) | (pl.program_id(1) > 0))
    mask_scr[0] = band0.astype(jnp.int32)
    mask_scr[1] = band.astype(jnp.int32)

    def scores(j):
        rows = pl.ds(j * BLOCK, BLOCK)
        for c in range(N_KV_HEADS):
            qs = jnp.concatenate(
                [q_ref[rows, pl.ds((group_pairs * c + g) * LANES, LANES)] for g in range(group_pairs)], axis=0)
            for e in range(2):
                cols = pl.ds((2 * c + e) * LANES, LANES)
                if j == 0:
                    k = jnp.concatenate([kp_ref[:, cols], kc_ref[pl.ds(0, BLOCK), cols]], axis=0)
                else:
                    k = kc_ref[pl.ds((j - 1) * BLOCK, 2 * BLOCK), cols]
                s = lax.dot_general(k, qs, nt, preferred_element_type=F32)
                for g in range(group_pairs):
                    s_scr[j, 2 * c + e, g] = s[:, g * BLOCK:(g + 1) * BLOCK]

    def softmax(j):
        for ce, g in [(ce, g) for ce in range(2 * N_KV_HEADS) for g in range(group_pairs)]:
            c, e = divmod(ce, 2)
            cols = pl.ds(g * BLOCK, BLOCK)
            sink = sinks_ref[group * c + 2 * g + e]
            m = None
            for r in range(key_tiles):
                rows = pl.ds(r * SUBLANES, SUBLANES)
                s = jnp.where(mask_scr[min(j, 1), rows, :] > 0, s_scr[j, ce, g, rows, :], NEG_INF)
                s_scr[j, ce, g, rows, :] = s
                m = s if m is None else jnp.maximum(m, s)
            m = jnp.maximum(_all_sublanes(m, jnp.maximum), sink)
            m2 = jnp.concatenate([m, m], axis=0)
            total = None
            for r in range(0, 2 * BLOCK, PACKED_ROWS):
                rows = pl.ds(r, PACKED_ROWS)
                pe = jnp.exp(s_scr[j, ce, g, rows, :] - m2)
                p_scr[j, ce, g, rows, :] = pe.astype(BF16)
                part = pe[:SUBLANES] + pe[SUBLANES:]
                total = part if total is None else total + part
            denom = _all_sublanes(total, jnp.add) + jnp.exp(sink - m)
            inv_scr[j, ce, :, cols] = 1.0 / denom

    def values(j):
        for c in range(N_KV_HEADS):
            dims = pl.ds(c * HEAD_DIM, HEAD_DIM)
            if j == 0:
                vt = jnp.concatenate([vtp_ref[dims, :], vtc_ref[dims, pl.ds(0, BLOCK)]], axis=1)
            else:
                vt = vtc_ref[dims, pl.ds((j - 1) * BLOCK, 2 * BLOCK)]
            for e in range(2):
                inv = jnp.concatenate([inv_scr[j, 2 * c + e]] * (HEAD_DIM // SUBLANES), axis=0)
                p = jnp.concatenate([p_scr[j, 2 * c + e, g] for g in range(group_pairs)], axis=1)
                o = jnp.dot(vt, p, preferred_element_type=F32) * inv
                for g in range(group_pairs):
                    attn_t[pl.ds((group * c + 2 * g + e) * HEAD_DIM, HEAD_DIM), pl.ds(j * BLOCK, BLOCK)] = (
                        o[:, g * BLOCK:(g + 1) * BLOCK])
        a = attn_t[:, pl.ds(j * BLOCK, BLOCK)]
        ms = jnp.mean(a * a, axis=0, keepdims=True)
        an_scr[pl.ds(j * BLOCK, BLOCK), :] = (a * lax.rsqrt(ms + EPS) * anw_ref[...]).T.astype(BF16)

    for t in range(nb + 2):
        if t < nb:
            scores(t)
        if 0 <= t - 1 < nb:
            softmax(t - 1)
        if 0 <= t - 2 < nb:
            values(t - 2)

    y = jnp.dot(an_scr[...], wo_ref[pl.ds(0, ATTN_WIDTH), :], preferred_element_type=F32)
    y = y + jnp.dot(cn_ref[...], wo_ref[pl.ds(ATTN_WIDTH, CONV_WIDTH), :], preferred_element_type=F32)
    y = x_ref[...] + y
    for k in range(o_ref.shape[0]):
        o_ref[k] = y[:, k * LANES:(k + 1) * LANES]


def _ffn_kernel(x_ref, nw_ref, wup_hbm, cw_ref, wd_hbm, o_ref, o_slab, up_scr, a_scr,
                wup_ref, wd_ref, w_sems):
    slabs, tm, _ = x_ref.shape
    m = tm // STRIDE
    d_ff = wd_ref.shape[0]
    nc = d_ff // MXU_DIM
    moved_start = [STRIDE * m + k * (m + HALO) for k in range(2)]

    @pl.when(_first_step())
    def _():
        _fetch_bf16(wup_hbm, wup_ref, [up_scr.at[c, pl.ds(0, tm), :] for c in range(nc)], w_sems)
        _fetch_bf16(wd_hbm, wd_ref, [up_scr.at[c, pl.ds(0, MXU_DIM), :] for c in range(nc)], w_sems)

    @pl.when(pl.program_id(1) == 0)
    def _():
        for s in moved_start:
            up_scr[:, pl.ds(s, SUBLANES), :] = jnp.zeros((nc, SUBLANES, 2 * MXU_DIM), F32)

    x = jnp.concatenate(
        [jnp.concatenate([x_ref[k, pl.ds(v, m, stride=STRIDE), :] for k in range(slabs)], axis=1)
         for v in range(STRIDE)], axis=0)
    inv_rms = lax.rsqrt(jnp.mean(x * x, axis=-1, keepdims=True) + EPS)
    h = (x * nw_ref[...]).astype(BF16)

    def pair(ref, c):
        return jnp.concatenate([ref[:, pl.ds(half + c * MXU_DIM, MXU_DIM)] for half in (0, d_ff)], axis=1)

    def up(c):
        r = jnp.dot(h, pair(wup_ref, c), preferred_element_type=F32) * inv_rms
        up_scr[c, pl.ds(0, tm), :] = r
        for k, s in enumerate(moved_start):
            v = STRIDE - 2 + k
            up_scr[c, pl.ds(s + 1, m), :] = r[v * m:(v + 1) * m]

    def gate(c, token):
        cw = pair(cw_ref, c)
        first = [v * m for v in range(STRIDE)]
        s1 = moved_start[1:] + first[:STRIDE - 1]
        s2 = moved_start + first[:STRIDE - 2]
        for v in range(STRIDE):
            for r in range(0, m, GATE_ROWS):
                z = _chained(up_scr[c, pl.ds(first[v] + r, GATE_ROWS), :], token, MXU_DIM)
                y = (cw[0:1, :] * up_scr[c, pl.ds(s2[v] + r, GATE_ROWS), :]
                     + cw[1:2, :] * up_scr[c, pl.ds(s1[v] + r, GATE_ROWS), :] + cw[2:3, :] * z)
                g, u = y[:, :MXU_DIM], y[:, MXU_DIM:]
                a = (g / (1.0 + jnp.exp2(g * -LOG2_E)) * u).astype(BF16)
                a_scr[c, pl.ds(v * m + r, GATE_ROWS), :] = a
                token = _zero_tile(a)
        return token

    token = None
    for c in range(nc):
        up(c)
        if c >= 1:
            token = gate(c - 1, token)
    gate(nc - 1, token)

    split = (nc - 1) * MXU_DIM
    a_head = jnp.concatenate([a_scr[c] for c in range(nc - 1)], axis=1)
    y = x + jnp.dot(a_head, wd_ref[pl.ds(0, split), :], preferred_element_type=F32)
    y = y + jnp.dot(a_scr[nc - 1], wd_ref[pl.ds(split, MXU_DIM), :], preferred_element_type=F32)

    for s in moved_start:
        up_scr[:, pl.ds(s, 1), :] = up_scr[:, pl.ds(s + m, 1), :]

    for k in range(slabs):
        for v in range(STRIDE):
            o_slab[k, pl.ds(v, m, stride=STRIDE), :] = y[v * m:(v + 1) * m, k * LANES:(k + 1) * LANES]
        o_ref[:, pl.ds(k * LANES, LANES)] = o_slab[k]


def _resident(shape):
    return pl.BlockSpec(shape, lambda b, i: (0,) * len(shape), pipeline_mode=pl.Buffered(1))


def _tile(tm, width):
    return pl.BlockSpec((None, tm, width), lambda b, i: (b, i, 0))


def _params(vmem_limit_bytes=VMEM_LIMIT_BYTES):
    return pltpu.CompilerParams(dimension_semantics=("arbitrary", "arbitrary"), vmem_limit_bytes=vmem_limit_bytes)


def _layer(x, attn_norm_w, w_in, q_norm_w, k_norm_w, sinks, conv_mix_w, attn_out_norm_w,
           conv_out_norm_w, w_out, ffn_norm_w, w_up, ffn_conv_w, w_down):
    bsz, seq, d = x.shape
    d_ff = w_down.shape[0]
    tm = TOKEN_TILE
    assert seq % tm == 0 and tm % BLOCK == 0 and d_ff % MXU_DIM == 0
    n_tiles = seq // tm
    nc = d_ff // MXU_DIM

    scale = HEAD_DIM ** -0.5
    qkw = jnp.concatenate([jnp.tile(q_norm_w, N_Q_HEADS) * scale, jnp.tile(k_norm_w, N_KV_HEADS)])[None, :]
    in_hbm = pl.BlockSpec(memory_space=pl.ANY)


    def this_tile(width):
        return pl.BlockSpec((None, tm, width), lambda b, i: (b, jnp.minimum(i, n_tiles - 1), 0))

    def last_tile(width):
        return pl.BlockSpec((None, tm, width), lambda b, i: (b, jnp.maximum(i - 1, 0), 0))

    act = jax.ShapeDtypeStruct((bsz, seq, 4 * LANES), BF16)
    in_w = w_in.shape[1]
    v_t = jax.ShapeDtypeStruct((bsz, KV_WIDTH, seq), BF16)
    last_tile_t = pl.BlockSpec((None, KV_WIDTH, tm), lambda b, i: (b, 0, jnp.maximum(i - 1, 0)))
    qn, kk, vt, cn = pl.pallas_call(
        _in_proj_kernel,
        grid=(bsz, n_tiles + 1),
        in_specs=[this_tile(d), _resident((1, d)), in_hbm,
                  _resident((1, ATTN_WIDTH + KV_WIDTH)),
                  _resident((CONV_TAPS, CONV_WIDTH)), _resident((1, CONV_WIDTH))],
        out_specs=[last_tile(4 * LANES), last_tile(4 * LANES), last_tile_t, last_tile(4 * LANES)],
        out_shape=[act, act, v_t, act],
        scratch_shapes=[pltpu.VMEM((tm, in_w), F32), pltpu.VMEM((tm, in_w), F32),
                        pltpu.VMEM((tm + HALO, CONV_WIDTH), F32),
                        pltpu.VMEM((d, in_w), BF16), pltpu.SemaphoreType.DMA((2,))],
        compiler_params=_params(),
        name="in_proj",
    )(x, attn_norm_w[None, :], w_in, qkw, conv_mix_w, conv_out_norm_w[None, :])

    bpt = tm // BLOCK
    slab_tile = pl.BlockSpec((None, d // LANES, tm, LANES), lambda b, i: (b, 0, i, 0))
    prev = pl.BlockSpec((None, BLOCK, 4 * LANES), lambda b, i: (b, jnp.maximum(i * bpt - 1, 0), 0))
    tile_t = pl.BlockSpec((None, KV_WIDTH, tm), lambda b, i: (b, 0, i))
    prev_t = pl.BlockSpec((None, KV_WIDTH, BLOCK), lambda b, i: (b, 0, jnp.maximum(i * bpt - 1, 0)))
    x = pl.pallas_call(
        _attn_kernel,
        grid=(bsz, n_tiles),
        in_specs=[pl.BlockSpec(memory_space=pltpu.SMEM),
                  _tile(tm, 4 * LANES), _tile(tm, 4 * LANES), prev, tile_t, prev_t,
                  _tile(tm, 4 * LANES), _tile(tm, d), in_hbm, _resident((ATTN_WIDTH, 1))],
        out_specs=slab_tile,
        out_shape=jax.ShapeDtypeStruct((bsz, d // LANES, seq, LANES), x.dtype),
        scratch_shapes=[pltpu.VMEM((bpt, 2 * N_KV_HEADS, 2, 2 * BLOCK, BLOCK), F32),
                        pltpu.VMEM((bpt, 2 * N_KV_HEADS, 2, 2 * BLOCK, BLOCK), BF16),
                        pltpu.VMEM((bpt, 2 * N_KV_HEADS, SUBLANES, 2 * BLOCK), F32),
                        pltpu.VMEM((ATTN_WIDTH, tm), F32),
                        pltpu.VMEM((tm, ATTN_WIDTH), BF16),
                        pltpu.VMEM((2, 2 * BLOCK, BLOCK), jnp.int32),
                        pltpu.VMEM((ATTN_WIDTH + CONV_WIDTH, d), BF16),
                        pltpu.SemaphoreType.DMA((bpt * 2 * N_KV_HEADS * 2,))],
        compiler_params=_params(),
        name="attn_out_proj",
    )(sinks, qn, kk, kk, vt, vt, cn, x, w_out, attn_out_norm_w[:, None])

    x = pl.pallas_call(
        _ffn_kernel,
        grid=(bsz, n_tiles),
        in_specs=[slab_tile, _resident((1, d)), in_hbm, _resident((CONV_TAPS, 2 * d_ff)), in_hbm],
        out_specs=_tile(tm, d),
        out_shape=jax.ShapeDtypeStruct((bsz, seq, d), x.dtype),
        scratch_shapes=[pltpu.VMEM((d // LANES, tm, LANES), F32),
                        pltpu.VMEM((nc, tm + 2 * (tm // STRIDE + HALO), 2 * MXU_DIM), F32),
                        pltpu.VMEM((nc, tm, MXU_DIM), BF16),
                        pltpu.VMEM((d, 2 * d_ff), BF16), pltpu.VMEM((d_ff, d), BF16),
                        pltpu.SemaphoreType.DMA((nc,))],
        compiler_params=_params(),
        name="conv_ffn",
    )(x, ffn_norm_w[None, :], w_up, ffn_conv_w, w_down)
    return x


def kernel(x, attn_norm_w, w_in, q_norm_w, k_norm_w, sinks, conv_mix_w, attn_out_norm_w,
           conv_out_norm_w, w_out, ffn_norm_w, w_up, ffn_conv_w, w_down):
    for l in range(attn_norm_w.shape[0]):
        x = _layer(x, attn_norm_w[l], w_in[l], q_norm_w[l], k_norm_w[l], sinks[l], conv_mix_w[l],
                   attn_out_norm_w[l], conv_out_norm_w[l], w_out[l], ffn_norm_w[l], w_up[l],
                   ffn_conv_w[l], w_down[l])
    return x
```

```python
import jax
import jax.numpy as jnp
from jax import lax
from jax.experimental import pallas as pl
from jax.experimental.pallas import tpu as pltpu

HEAD_DIM = 64
N_Q_HEADS = 8
N_KV_HEADS = 2
WINDOW = 128
BLOCK = 128
ATTN_WIDTH = N_Q_HEADS * HEAD_DIM
KV_WIDTH = N_KV_HEADS * HEAD_DIM
CONV_WIDTH = 512
CONV_TAPS = 3
EPS = 1e-6
NEG_INF = -1e30
LOG2_E = 1.4426950408889634

LANES = 128
SUBLANES = 8
MXU_DIM = 256
VMEM_LIMIT_BYTES = 56 * 1024 * 1024

TOKEN_TILE = 512
HALO = SUBLANES
STRIDE = 4
GATE_ROWS = 64
QKV_ROWS = 64
PIECE_ROWS = 32
PACKED_ROWS = 2 * SUBLANES

F32 = jnp.float32
BF16 = jnp.bfloat16


def _rms(x, w):
    ms = jnp.mean(x * x, axis=-1, keepdims=True)
    return x * lax.rsqrt(ms + EPS) * w


def _zero_tile(packed):
    words = pltpu.bitcast(packed, jnp.uint32)
    while words.shape[1] > LANES:
        half = words.shape[1] // 2
        words = words[:, :half] | words[:, half:]
    while words.shape[0] > SUBLANES:
        half = words.shape[0] // 2
        words = words[:half] | words[half:]
    return ((words >> 16) >> 16).astype(F32)


def _chained(x, zero_tile, width=LANES):
    if zero_tile is None:
        return x
    zeros = jnp.concatenate([jnp.concatenate([zero_tile] * (x.shape[0] // SUBLANES), axis=0)] * (width // LANES),
                            axis=1)
    if width == x.shape[1]:
        return x + zeros
    return jnp.concatenate([x[:, :width] + zeros, x[:, width:]], axis=1)


def _fetch_bf16(w_hbm, w_bf16, slots, sems):
    rows, cols = slots[0].shape
    blocks = [(r, c) for r in range(0, w_hbm.shape[0], rows) for c in range(0, w_hbm.shape[1], cols)]
    assert w_hbm.shape[0] % rows == 0 and w_hbm.shape[1] % cols == 0

    def copy(b):
        r, c = blocks[b]
        s = b % len(slots)
        return pltpu.make_async_copy(w_hbm.at[pl.ds(r, rows), pl.ds(c, cols)], slots[s], sems.at[s])

    for b in range(min(len(slots), len(blocks))):
        copy(b).start()
    for b, (r, c) in enumerate(blocks):
        copy(b).wait()
        w_bf16[pl.ds(r, rows), pl.ds(c, cols)] = slots[b % len(slots)][...].astype(BF16)
        if b + len(slots) < len(blocks):
            copy(b + len(slots)).start()


def _first_step():
    return (pl.program_id(0) == 0) & (pl.program_id(1) == 0)


def _token_mixer_kernel(sinks_ref, x_ref, xl_ref, nw_ref, win_hbm, qkw_ref, cw_ref, cnw_ref, wo_hbm, anw_ref,
                        o_ref, proj_a, proj_b, cu_scr, q_scr, kv_scr, vt_scr, cn_scr,
                        s_scr, p_scr, inv_scr, attn_t, an_scr, mask_scr, win_ref, wo_ref, w_sems):
    tm = x_ref.shape[0]
    qkv_w = ATTN_WIDTH + 2 * KV_WIDTH
    step = pl.program_id(1)

    @pl.when(_first_step())
    def _():
        _fetch_bf16(win_hbm, win_ref, [proj_a, proj_b], w_sems)
        slots = [s_scr.at[j, ce, g] for j in range(s_scr.shape[0]) for ce in range(s_scr.shape[1])
                 for g in range(s_scr.shape[2])]
        _fetch_bf16(wo_hbm, wo_ref, slots, w_sems)

    @pl.when(step == 0)
    def _():
        proj_b[...] = jnp.zeros(proj_b.shape, F32)
        cu_scr[pl.ds(0, HALO), :] = jnp.zeros((HALO, CONV_WIDTH), F32)
        kv_scr[pl.ds(0, BLOCK), :] = jnp.zeros((BLOCK, kv_scr.shape[1]), BF16)
        vt_scr[:, pl.ds(0, BLOCK)] = jnp.zeros((KV_WIDTH, BLOCK), BF16)

    def body(fill, drain):
        h = _rms(x_ref[...], nw_ref[...]).astype(BF16)
        fill[...] = jnp.dot(h, win_ref[...], preferred_element_type=F32)

        low = lax.broadcasted_iota(jnp.int32, (QKV_ROWS, LANES), 1) < HEAD_DIM
        token = None
        for r in range(0, tm, QKV_ROWS):
            rows = pl.ds(r, QKV_ROWS)
            qk = _chained(drain[rows, pl.ds(0, ATTN_WIDTH + KV_WIDTH)], token, ATTN_WIDTH + KV_WIDTH)
            sums = []
            for t in range(0, ATTN_WIDTH + KV_WIDTH, LANES):
                sq = qk[:, t:t + LANES] * qk[:, t:t + LANES]
                sums.append(jnp.where(low, jnp.sum(jnp.where(low, sq, 0.0), axis=-1, keepdims=True),
                                      jnp.sum(jnp.where(low, 0.0, sq), axis=-1, keepdims=True)))
            qkn = qk * lax.rsqrt(jnp.concatenate(sums, axis=1) * (1.0 / HEAD_DIM) + EPS) * qkw_ref[...]
            qn = qkn[:, :ATTN_WIDTH].astype(BF16)
            q_scr[rows, :] = qn
            k = qkn[:, ATTN_WIDTH:]
            swapped = pltpu.roll(k, HEAD_DIM, axis=1)
            kk = jnp.concatenate([jnp.where(low, k, 0.0), jnp.where(low, 0.0, swapped),
                                  jnp.where(low, swapped, 0.0), jnp.where(low, 0.0, k)], axis=1).astype(BF16)
            kv_scr[pl.ds(BLOCK + r, QKV_ROWS), :] = kk
            token = _zero_tile(qn) + _zero_tile(kk)
        vt_scr[:, pl.ds(BLOCK, tm)] = drain[:, pl.ds(ATTN_WIDTH + KV_WIDTH, KV_WIDTH)].T.astype(BF16)

        cw = cw_ref[...]
        token = None
        for r in range(0, tm, PIECE_ROWS):
            rows = pl.ds(r, PIECE_ROWS)
            bg = _chained(drain[rows, pl.ds(qkv_w, CONV_WIDTH)], token)
            cu_scr[pl.ds(HALO + r, PIECE_ROWS), :] = (drain[rows, pl.ds(qkv_w + CONV_WIDTH, CONV_WIDTH)]
                                                      * drain[rows, pl.ds(qkv_w + 2 * CONV_WIDTH, CONV_WIDTH)])
            y = cw[0:1, :] * cu_scr[pl.ds(HALO + r - 2, PIECE_ROWS), :]
            y = y + cw[1:2, :] * cu_scr[pl.ds(HALO + r - 1, PIECE_ROWS), :]
            y = y + cw[2:3, :] * cu_scr[pl.ds(HALO + r, PIECE_ROWS), :]
            cn = _rms(bg * y, cnw_ref[...]).astype(BF16)
            cn_scr[rows, :] = cn
            token = _zero_tile(cn)
        cu_scr[pl.ds(0, HALO), :] = cu_scr[pl.ds(tm, HALO), :]

        _attend(step > 1, sinks_ref, q_scr, kv_scr, vt_scr, s_scr, p_scr, inv_scr, attn_t, an_scr, mask_scr,
                anw_ref)
        y = jnp.dot(an_scr[...], wo_ref[pl.ds(0, ATTN_WIDTH), :], preferred_element_type=F32)
        y = y + jnp.dot(cn_scr[...], wo_ref[pl.ds(ATTN_WIDTH, CONV_WIDTH), :], preferred_element_type=F32)
        y = xl_ref[...] + y
        for k in range(o_ref.shape[0]):
            o_ref[k] = y[:, k * LANES:(k + 1) * LANES]
        kv_scr[pl.ds(0, BLOCK), :] = kv_scr[pl.ds(tm, BLOCK), :]
        vt_scr[:, pl.ds(0, BLOCK)] = vt_scr[:, pl.ds(tm, BLOCK)]

    @pl.when(step % 2 == 0)
    def _():
        body(proj_a, proj_b)

    @pl.when(step % 2 == 1)
    def _():
        body(proj_b, proj_a)


def _all_sublanes(x, op):
    for shift in (4, 2, 1):
        x = op(x, pltpu.roll(x, shift, axis=0))
    return x


def _attend(has_prev, sinks_ref, q_ref, kv_ref, vt_ref, s_scr, p_scr, inv_scr, attn_t, an_scr, mask_scr, anw_ref):
    tq = q_ref.shape[0]
    nb = tq // BLOCK
    group = N_Q_HEADS // N_KV_HEADS
    group_pairs = group // 2
    key_tiles = 2 * BLOCK // SUBLANES
    nt = (((1,), (1,)), ((), ()))

    sj = lax.broadcasted_iota(jnp.int32, (2 * BLOCK, BLOCK), 0)
    qi = lax.broadcasted_iota(jnp.int32, (2 * BLOCK, BLOCK), 1)
    rel = qi + BLOCK - sj
    band = (rel >= 0) & (rel < WINDOW)
    band0 = band & ((sj >= BLOCK) | has_prev)
    mask_scr[0] = band0.astype(jnp.int32)
    mask_scr[1] = band.astype(jnp.int32)

    def scores(j):
        rows = pl.ds(j * BLOCK, BLOCK)
        for c in range(N_KV_HEADS):
            qs = jnp.concatenate(
                [q_ref[rows, pl.ds((group_pairs * c + g) * LANES, LANES)] for g in range(group_pairs)], axis=0)
            for e in range(2):
                k = kv_ref[pl.ds(j * BLOCK, 2 * BLOCK), pl.ds((2 * c + e) * LANES, LANES)]
                s = lax.dot_general(k, qs, nt, preferred_element_type=F32)
                for g in range(group_pairs):
                    s_scr[j, 2 * c + e, g] = s[:, g * BLOCK:(g + 1) * BLOCK]

    def softmax(j):
        for ce, g in [(ce, g) for ce in range(2 * N_KV_HEADS) for g in range(group_pairs)]:
            c, e = divmod(ce, 2)
            cols = pl.ds(g * BLOCK, BLOCK)
            sink = sinks_ref[group * c + 2 * g + e]
            m = None
            for r in range(key_tiles):
                rows = pl.ds(r * SUBLANES, SUBLANES)
                s = jnp.where(mask_scr[min(j, 1), rows, :] > 0, s_scr[j, ce, g, rows, :], NEG_INF)
                s_scr[j, ce, g, rows, :] = s
                m = s if m is None else jnp.maximum(m, s)
            m = jnp.maximum(_all_sublanes(m, jnp.maximum), sink)
            m2 = jnp.concatenate([m, m], axis=0)
            total = None
            for r in range(0, 2 * BLOCK, PACKED_ROWS):
                rows = pl.ds(r, PACKED_ROWS)
                pe = jnp.exp(s_scr[j, ce, g, rows, :] - m2)
                p_scr[j, ce, g, rows, :] = pe.astype(BF16)
                part = pe[:SUBLANES] + pe[SUBLANES:]
                total = part if total is None else total + part
            denom = _all_sublanes(total, jnp.add) + jnp.exp(sink - m)
            inv_scr[j, ce, :, cols] = 1.0 / denom

    def values(j):
        for c in range(N_KV_HEADS):
            vt = vt_ref[pl.ds(c * HEAD_DIM, HEAD_DIM), pl.ds(j * BLOCK, 2 * BLOCK)]
            for e in range(2):
                inv = jnp.concatenate([inv_scr[j, 2 * c + e]] * (HEAD_DIM // SUBLANES), axis=0)
                p = jnp.concatenate([p_scr[j, 2 * c + e, g] for g in range(group_pairs)], axis=1)
                o = jnp.dot(vt, p, preferred_element_type=F32) * inv
                for g in range(group_pairs):
                    attn_t[pl.ds((group * c + 2 * g + e) * HEAD_DIM, HEAD_DIM), pl.ds(j * BLOCK, BLOCK)] = (
                        o[:, g * BLOCK:(g + 1) * BLOCK])
        a = attn_t[:, pl.ds(j * BLOCK, BLOCK)]
        ms = jnp.mean(a * a, axis=0, keepdims=True)
        an_scr[pl.ds(j * BLOCK, BLOCK), :] = (a * lax.rsqrt(ms + EPS) * anw_ref[...]).T.astype(BF16)

    for t in range(nb + 2):
        if t < nb:
            scores(t)
        if 0 <= t - 1 < nb:
            softmax(t - 1)
        if 0 <= t - 2 < nb:
            values(t - 2)


def _ffn_kernel(x_ref, nw_ref, wup_hbm, cw_ref, wd_hbm, o_ref, o_slab, up_scr, a_scr,
                wup_ref, wd_ref, w_sems):
    slabs, tm, _ = x_ref.shape
    m = tm // STRIDE
    d_ff = wd_ref.shape[0]
    nc = d_ff // MXU_DIM
    moved_start = [STRIDE * m + k * (m + HALO) for k in range(2)]

    @pl.when(_first_step())
    def _():
        _fetch_bf16(wup_hbm, wup_ref, [up_scr.at[c, pl.ds(0, tm), :] for c in range(nc)], w_sems)
        _fetch_bf16(wd_hbm, wd_ref, [up_scr.at[c, pl.ds(0, MXU_DIM), :] for c in range(nc)], w_sems)

    @pl.when(pl.program_id(1) == 0)
    def _():
        for s in moved_start:
            up_scr[:, pl.ds(s, SUBLANES), :] = jnp.zeros((nc, SUBLANES, 2 * MXU_DIM), F32)

    x = jnp.concatenate(
        [jnp.concatenate([x_ref[k, pl.ds(v, m, stride=STRIDE), :] for k in range(slabs)], axis=1)
         for v in range(STRIDE)], axis=0)
    inv_rms = lax.rsqrt(jnp.mean(x * x, axis=-1, keepdims=True) + EPS)
    h = (x * nw_ref[...]).astype(BF16)

    def pair(ref, c):
        return jnp.concatenate([ref[:, pl.ds(half + c * MXU_DIM, MXU_DIM)] for half in (0, d_ff)], axis=1)

    def up(c):
        r = jnp.dot(h, pair(wup_ref, c), preferred_element_type=F32) * inv_rms
        up_scr[c, pl.ds(0, tm), :] = r
        for k, s in enumerate(moved_start):
            v = STRIDE - 2 + k
            up_scr[c, pl.ds(s + 1, m), :] = r[v * m:(v + 1) * m]

    def gate(c, token):
        cw = pair(cw_ref, c)
        first = [v * m for v in range(STRIDE)]
        s1 = moved_start[1:] + first[:STRIDE - 1]
        s2 = moved_start + first[:STRIDE - 2]
        for v in range(STRIDE):
            for r in range(0, m, GATE_ROWS):
                z = _chained(up_scr[c, pl.ds(first[v] + r, GATE_ROWS), :], token, MXU_DIM)
                y = (cw[0:1, :] * up_scr[c, pl.ds(s2[v] + r, GATE_ROWS), :]
                     + cw[1:2, :] * up_scr[c, pl.ds(s1[v] + r, GATE_ROWS), :] + cw[2:3, :] * z)
                g, u = y[:, :MXU_DIM], y[:, MXU_DIM:]
                a = (g / (1.0 + jnp.exp2(g * -LOG2_E)) * u).astype(BF16)
                a_scr[c, pl.ds(v * m + r, GATE_ROWS), :] = a
                token = _zero_tile(a)
        return token

    token = None
    for c in range(nc):
        up(c)
        if c >= 1:
            token = gate(c - 1, token)
    gate(nc - 1, token)

    split = (nc - 1) * MXU_DIM
    a_head = jnp.concatenate([a_scr[c] for c in range(nc - 1)], axis=1)
    y = x + jnp.dot(a_head, wd_ref[pl.ds(0, split), :], preferred_element_type=F32)
    y = y + jnp.dot(a_scr[nc - 1], wd_ref[pl.ds(split, MXU_DIM), :], preferred_element_type=F32)

    for s in moved_start:
        up_scr[:, pl.ds(s, 1), :] = up_scr[:, pl.ds(s + m, 1), :]

    for k in range(slabs):
        for v in range(STRIDE):
            o_slab[k, pl.ds(v, m, stride=STRIDE), :] = y[v * m:(v + 1) * m, k * LANES:(k + 1) * LANES]
        o_ref[:, pl.ds(k * LANES, LANES)] = o_slab[k]


def _resident(shape):
    return pl.BlockSpec(shape, lambda b, i: (0,) * len(shape), pipeline_mode=pl.Buffered(1))


def _tile(tm, width):
    return pl.BlockSpec((None, tm, width), lambda b, i: (b, i, 0))


def _params(vmem_limit_bytes=VMEM_LIMIT_BYTES):
    return pltpu.CompilerParams(dimension_semantics=("arbitrary", "arbitrary"), vmem_limit_bytes=vmem_limit_bytes)


def _layer(x, attn_norm_w, w_in, q_norm_w, k_norm_w, sinks, conv_mix_w, attn_out_norm_w,
           conv_out_norm_w, w_out, ffn_norm_w, w_up, ffn_conv_w, w_down):
    bsz, seq, d = x.shape
    d_ff = w_down.shape[0]
    tm = TOKEN_TILE
    assert seq % tm == 0 and tm % BLOCK == 0 and d_ff % MXU_DIM == 0
    n_tiles = seq // tm
    nc = d_ff // MXU_DIM

    scale = HEAD_DIM ** -0.5
    qkw = jnp.concatenate([jnp.tile(q_norm_w, N_Q_HEADS) * scale, jnp.tile(k_norm_w, N_KV_HEADS)])[None, :]
    in_hbm = pl.BlockSpec(memory_space=pl.ANY)


    this_tile = pl.BlockSpec((None, tm, d), lambda b, i: (b, jnp.minimum(i, n_tiles - 1), 0))
    last_tile = pl.BlockSpec((None, tm, d), lambda b, i: (b, jnp.maximum(i - 1, 0), 0))
    last_slab_tile = pl.BlockSpec((None, d // LANES, tm, LANES), lambda b, i: (b, 0, jnp.maximum(i - 1, 0), 0))
    slab_tile = pl.BlockSpec((None, d // LANES, tm, LANES), lambda b, i: (b, 0, i, 0))
    in_w = w_in.shape[1]
    bpt = tm // BLOCK
    score_units = (bpt, 2 * N_KV_HEADS, N_Q_HEADS // N_KV_HEADS // 2)
    x = pl.pallas_call(
        _token_mixer_kernel,
        grid=(bsz, n_tiles + 1),
        in_specs=[pl.BlockSpec(memory_space=pltpu.SMEM), this_tile, last_tile, _resident((1, d)), in_hbm,
                  _resident((1, ATTN_WIDTH + KV_WIDTH)), _resident((CONV_TAPS, CONV_WIDTH)),
                  _resident((1, CONV_WIDTH)), in_hbm, _resident((ATTN_WIDTH, 1))],
        out_specs=last_slab_tile,
        out_shape=jax.ShapeDtypeStruct((bsz, d // LANES, seq, LANES), x.dtype),
        scratch_shapes=[pltpu.VMEM((tm, in_w), F32), pltpu.VMEM((tm, in_w), F32),
                        pltpu.VMEM((tm + HALO, CONV_WIDTH), F32),
                        pltpu.VMEM((tm, ATTN_WIDTH), BF16),
                        pltpu.VMEM((BLOCK + tm, 2 * N_KV_HEADS * LANES), BF16),
                        pltpu.VMEM((KV_WIDTH, BLOCK + tm), BF16),
                        pltpu.VMEM((tm, CONV_WIDTH), BF16),
                        pltpu.VMEM(score_units + (2 * BLOCK, BLOCK), F32),
                        pltpu.VMEM(score_units + (2 * BLOCK, BLOCK), BF16),
                        pltpu.VMEM((bpt, 2 * N_KV_HEADS, SUBLANES, 2 * BLOCK), F32),
                        pltpu.VMEM((ATTN_WIDTH, tm), F32),
                        pltpu.VMEM((tm, ATTN_WIDTH), BF16),
                        pltpu.VMEM((2, 2 * BLOCK, BLOCK), jnp.int32),
                        pltpu.VMEM((d, in_w), BF16),
                        pltpu.VMEM((ATTN_WIDTH + CONV_WIDTH, d), BF16),
                        pltpu.SemaphoreType.DMA((bpt * 2 * N_KV_HEADS * 2,))],
        compiler_params=_params(),
        name="token_mixer",
    )(sinks, x, x, attn_norm_w[None, :], w_in, qkw, conv_mix_w, conv_out_norm_w[None, :], w_out,
      attn_out_norm_w[:, None])

    x = pl.pallas_call(
        _ffn_kernel,
        grid=(bsz, n_tiles),
        in_specs=[slab_tile, _resident((1, d)), in_hbm, _resident((CONV_TAPS, 2 * d_ff)), in_hbm],
        out_specs=_tile(tm, d),
        out_shape=jax.ShapeDtypeStruct((bsz, seq, d), x.dtype),
        scratch_shapes=[pltpu.VMEM((d // LANES, tm, LANES), F32),
                        pltpu.VMEM((nc, tm + 2 * (tm // STRIDE + HALO), 2 * MXU_DIM), F32),
                        pltpu.VMEM((nc, tm, MXU_DIM), BF16),
                        pltpu.VMEM((d, 2 * d_ff), BF16), pltpu.VMEM((d_ff, d), BF16),
                        pltpu.SemaphoreType.DMA((nc,))],
        compiler_params=_params(),
        name="conv_ffn",
    )(x, ffn_norm_w[None, :], w_up, ffn_conv_w, w_down)
    return x


def kernel(x, attn_norm_w, w_in, q_norm_w, k_norm_w, sinks, conv_mix_w, attn_out_norm_w,
           conv_out_norm_w, w_out, ffn_norm_w, w_up, ffn_conv_w, w_down):
    for l in range(attn_norm_w.shape[0]):
        x = _layer(x, attn_norm_w[l], w_in[l], q_norm_w[l], k_norm_w[l], sinks[l], conv_mix_w[l],
                   attn_out_norm_w[l], conv_out_norm_w[l], w_out[l], ffn_norm_w[l], w_up[l],
                   ffn_conv_w[l], w_down[l])
    return x
```

```python
import jax
import jax.numpy as jnp
from jax import lax
from jax.experimental import pallas as pl
from jax.experimental.pallas import tpu as pltpu

HEAD_DIM = 64
N_Q_HEADS = 8
N_KV_HEADS = 2
WINDOW = 128
BLOCK = 128
ATTN_WIDTH = N_Q_HEADS * HEAD_DIM
KV_WIDTH = N_KV_HEADS * HEAD_DIM
CONV_WIDTH = 512
CONV_TAPS = 3
EPS = 1e-6
NEG_INF = -1e30
LOG2_E = 1.4426950408889634

LANES = 128
SUBLANES = 8
MXU_DIM = 256
VMEM_LIMIT_BYTES = 56 * 1024 * 1024

TOKEN_TILE = 512
HALO = SUBLANES
STRIDE = 4
GATE_ROWS = 64
QKV_ROWS = 64
PIECE_ROWS = 32
PACKED_ROWS = 2 * SUBLANES

F32 = jnp.float32
BF16 = jnp.bfloat16


def _rms(x, w):
    ms = jnp.mean(x * x, axis=-1, keepdims=True)
    return x * lax.rsqrt(ms + EPS) * w


def _zero_tile(packed):
    words = pltpu.bitcast(packed, jnp.uint32)
    while words.shape[1] > LANES:
        half = words.shape[1] // 2
        words = words[:, :half] | words[:, half:]
    while words.shape[0] > SUBLANES:
        half = words.shape[0] // 2
        words = words[:half] | words[half:]
    return ((words >> 16) >> 16).astype(F32)


def _chained(x, zero_tile, width=LANES):
    if zero_tile is None:
        return x
    zeros = jnp.concatenate([jnp.concatenate([zero_tile] * (x.shape[0] // SUBLANES), axis=0)] * (width // LANES),
                            axis=1)
    if width == x.shape[1]:
        return x + zeros
    return jnp.concatenate([x[:, :width] + zeros, x[:, width:]], axis=1)


def _fetch_bf16(w_hbm, w_bf16, slots, sems):
    rows, cols = slots[0].shape
    blocks = [(r, c) for r in range(0, w_hbm.shape[0], rows) for c in range(0, w_hbm.shape[1], cols)]
    assert w_hbm.shape[0] % rows == 0 and w_hbm.shape[1] % cols == 0

    def copy(b):
        r, c = blocks[b]
        s = b % len(slots)
        return pltpu.make_async_copy(w_hbm.at[pl.ds(r, rows), pl.ds(c, cols)], slots[s], sems.at[s])

    for b in range(min(len(slots), len(blocks))):
        copy(b).start()
    for b, (r, c) in enumerate(blocks):
        copy(b).wait()
        w_bf16[pl.ds(r, rows), pl.ds(c, cols)] = slots[b % len(slots)][...].astype(BF16)
        if b + len(slots) < len(blocks):
            copy(b + len(slots)).start()


def _first_step():
    return (pl.program_id(0) == 0) & (pl.program_id(1) == 0)


def _token_mixer_kernel(sinks_ref, x_ref, xl_ref, nw_ref, win_hbm, qkw_ref, cw_ref, cnw_ref, wo_hbm, anw_ref,
                        o_ref, proj_a, proj_b, cu_scr, q_scr, kv_scr, vt_scr, cn_scr,
                        s_scr, p_scr, inv_scr, attn_t, an_scr, mask_scr, win_ref, wo_ref, w_sems):
    tm = x_ref.shape[0]
    qkv_w = ATTN_WIDTH + 2 * KV_WIDTH
    step = pl.program_id(1)

    @pl.when(_first_step())
    def _():
        _fetch_bf16(win_hbm, win_ref, [proj_a, proj_b], w_sems)
        slots = [s_scr.at[j, ce, g] for j in range(s_scr.shape[0]) for ce in range(s_scr.shape[1])
                 for g in range(s_scr.shape[2])]
        _fetch_bf16(wo_hbm, wo_ref, slots, w_sems)

    @pl.when(step == 0)
    def _():
        proj_b[...] = jnp.zeros(proj_b.shape, F32)
        cu_scr[pl.ds(0, HALO), :] = jnp.zeros((HALO, CONV_WIDTH), F32)
        kv_scr[pl.ds(0, BLOCK), :] = jnp.zeros((BLOCK, kv_scr.shape[1]), BF16)
        vt_scr[:, pl.ds(0, BLOCK)] = jnp.zeros((KV_WIDTH, BLOCK), BF16)

    def body(fill, drain):
        h = _rms(x_ref[...], nw_ref[...]).astype(BF16)
        fill[...] = jnp.dot(h, win_ref[...], preferred_element_type=F32)

        low = lax.broadcasted_iota(jnp.int32, (QKV_ROWS, LANES), 1) < HEAD_DIM
        for r in range(0, tm, QKV_ROWS):
            rows = pl.ds(r, QKV_ROWS)
            qk = drain[rows, pl.ds(0, ATTN_WIDTH + KV_WIDTH)]
            sums = []
            for t in range(0, ATTN_WIDTH + KV_WIDTH, LANES):
                sq = qk[:, t:t + LANES] * qk[:, t:t + LANES]
                sums.append(jnp.where(low, jnp.sum(jnp.where(low, sq, 0.0), axis=-1, keepdims=True),
                                      jnp.sum(jnp.where(low, 0.0, sq), axis=-1, keepdims=True)))
            qkn = qk * lax.rsqrt(jnp.concatenate(sums, axis=1) * (1.0 / HEAD_DIM) + EPS) * qkw_ref[...]
            qn = qkn[:, :ATTN_WIDTH].astype(BF16)
            q_scr[rows, :] = qn
            k = qkn[:, ATTN_WIDTH:]
            swapped = pltpu.roll(k, HEAD_DIM, axis=1)
            kk = jnp.concatenate([jnp.where(low, k, 0.0), jnp.where(low, 0.0, swapped),
                                  jnp.where(low, swapped, 0.0), jnp.where(low, 0.0, k)], axis=1).astype(BF16)
            kv_scr[pl.ds(BLOCK + r, QKV_ROWS), :] = kk
        vt_scr[:, pl.ds(BLOCK, tm)] = drain[:, pl.ds(ATTN_WIDTH + KV_WIDTH, KV_WIDTH)].T.astype(BF16)

        cw = cw_ref[...]
        token = None
        for r in range(0, tm, PIECE_ROWS):
            rows = pl.ds(r, PIECE_ROWS)
            bg = _chained(drain[rows, pl.ds(qkv_w, CONV_WIDTH)], token)
            cu_scr[pl.ds(HALO + r, PIECE_ROWS), :] = (drain[rows, pl.ds(qkv_w + CONV_WIDTH, CONV_WIDTH)]
                                                      * drain[rows, pl.ds(qkv_w + 2 * CONV_WIDTH, CONV_WIDTH)])
            y = cw[0:1, :] * cu_scr[pl.ds(HALO + r - 2, PIECE_ROWS), :]
            y = y + cw[1:2, :] * cu_scr[pl.ds(HALO + r - 1, PIECE_ROWS), :]
            y = y + cw[2:3, :] * cu_scr[pl.ds(HALO + r, PIECE_ROWS), :]
            cn = _rms(bg * y, cnw_ref[...]).astype(BF16)
            cn_scr[rows, :] = cn
            token = _zero_tile(cn)
        cu_scr[pl.ds(0, HALO), :] = cu_scr[pl.ds(tm, HALO), :]

        _attend(step > 1, sinks_ref, q_scr, kv_scr, vt_scr, s_scr, p_scr, inv_scr, attn_t, an_scr, mask_scr,
                anw_ref)
        y = jnp.dot(an_scr[...], wo_ref[pl.ds(0, ATTN_WIDTH), :], preferred_element_type=F32)
        y = y + jnp.dot(cn_scr[...], wo_ref[pl.ds(ATTN_WIDTH, CONV_WIDTH), :], preferred_element_type=F32)
        y = xl_ref[...] + y
        for k in range(o_ref.shape[0]):
            o_ref[k] = y[:, k * LANES:(k + 1) * LANES]
        kv_scr[pl.ds(0, BLOCK), :] = kv_scr[pl.ds(tm, BLOCK), :]
        vt_scr[:, pl.ds(0, BLOCK)] = vt_scr[:, pl.ds(tm, BLOCK)]

    @pl.when(step % 2 == 0)
    def _():
        body(proj_a, proj_b)

    @pl.when(step % 2 == 1)
    def _():
        body(proj_b, proj_a)


def _all_sublanes(x, op):
    for shift in (4, 2, 1):
        x = op(x, pltpu.roll(x, shift, axis=0))
    return x


def _attend(has_prev, sinks_ref, q_ref, kv_ref, vt_ref, s_scr, p_scr, inv_scr, attn_t, an_scr, mask_scr, anw_ref):
    tq = q_ref.shape[0]
    nb = tq // BLOCK
    group = N_Q_HEADS // N_KV_HEADS
    group_pairs = group // 2
    key_tiles = 2 * BLOCK // SUBLANES
    nt = (((1,), (1,)), ((), ()))

    sj = lax.broadcasted_iota(jnp.int32, (2 * BLOCK, BLOCK), 0)
    qi = lax.broadcasted_iota(jnp.int32, (2 * BLOCK, BLOCK), 1)
    rel = qi + BLOCK - sj
    band = (rel >= 0) & (rel < WINDOW)
    band0 = band & ((sj >= BLOCK) | has_prev)
    mask_scr[0] = band0.astype(jnp.int32)
    mask_scr[1] = band.astype(jnp.int32)

    def scores(j):
        rows = pl.ds(j * BLOCK, BLOCK)
        for c in range(N_KV_HEADS):
            qs = jnp.concatenate(
                [q_ref[rows, pl.ds((group_pairs * c + g) * LANES, LANES)] for g in range(group_pairs)], axis=0)
            for e in range(2):
                k = kv_ref[pl.ds(j * BLOCK, 2 * BLOCK), pl.ds((2 * c + e) * LANES, LANES)]
                s = lax.dot_general(k, qs, nt, preferred_element_type=F32)
                for g in range(group_pairs):
                    s_scr[j, 2 * c + e, g] = s[:, g * BLOCK:(g + 1) * BLOCK]

    def softmax(j):
        for ce, g in [(ce, g) for ce in range(2 * N_KV_HEADS) for g in range(group_pairs)]:
            c, e = divmod(ce, 2)
            cols = pl.ds(g * BLOCK, BLOCK)
            sink = sinks_ref[group * c + 2 * g + e]
            m = None
            for r in range(key_tiles):
                rows = pl.ds(r * SUBLANES, SUBLANES)
                s = jnp.where(mask_scr[min(j, 1), rows, :] > 0, s_scr[j, ce, g, rows, :], NEG_INF)
                s_scr[j, ce, g, rows, :] = s
                m = s if m is None else jnp.maximum(m, s)
            m = jnp.maximum(_all_sublanes(m, jnp.maximum), sink)
            m2 = jnp.concatenate([m, m], axis=0)
            total = None
            for r in range(0, 2 * BLOCK, PACKED_ROWS):
                rows = pl.ds(r, PACKED_ROWS)
                pe = jnp.exp(s_scr[j, ce, g, rows, :] - m2)
                p_scr[j, ce, g, rows, :] = pe.astype(BF16)
                part = pe[:SUBLANES] + pe[SUBLANES:]
                total = part if total is None else total + part
            denom = _all_sublanes(total, jnp.add) + jnp.exp(sink - m)
            inv_scr[j, ce, :, cols] = 1.0 / denom

    def values(j):
        for c in range(N_KV_HEADS):
            vt = vt_ref[pl.ds(c * HEAD_DIM, HEAD_DIM), pl.ds(j * BLOCK, 2 * BLOCK)]
            for e in range(2):
                inv = jnp.concatenate([inv_scr[j, 2 * c + e]] * (HEAD_DIM // SUBLANES), axis=0)
                p = jnp.concatenate([p_scr[j, 2 * c + e, g] for g in range(group_pairs)], axis=1)
                o = jnp.dot(vt, p, preferred_element_type=F32) * inv
                for g in range(group_pairs):
                    attn_t[pl.ds((group * c + 2 * g + e) * HEAD_DIM, HEAD_DIM), pl.ds(j * BLOCK, BLOCK)] = (
                        o[:, g * BLOCK:(g + 1) * BLOCK])
        a = attn_t[:, pl.ds(j * BLOCK, BLOCK)]
        ms = jnp.mean(a * a, axis=0, keepdims=True)
        an_scr[pl.ds(j * BLOCK, BLOCK), :] = (a * lax.rsqrt(ms + EPS) * anw_ref[...]).T.astype(BF16)

    for t in range(nb + 2):
        if t < nb:
            scores(t)
        if 0 <= t - 1 < nb:
            softmax(t - 1)
        if 0 <= t - 2 < nb:
            values(t - 2)


def _ffn_kernel(x_ref, nw_ref, wup_hbm, cw_ref, wd_hbm, o_ref, o_slab, up_scr, a_scr,
                wup_ref, wd_ref, w_sems):
    slabs, tm, _ = x_ref.shape
    m = tm // STRIDE
    d_ff = wd_ref.shape[0]
    nc = d_ff // MXU_DIM
    moved_start = [STRIDE * m + k * (m + HALO) for k in range(2)]

    @pl.when(_first_step())
    def _():
        _fetch_bf16(wup_hbm, wup_ref, [up_scr.at[c, pl.ds(0, tm), :] for c in range(nc)], w_sems)
        _fetch_bf16(wd_hbm, wd_ref, [up_scr.at[c, pl.ds(0, MXU_DIM), :] for c in range(nc)], w_sems)

    @pl.when(pl.program_id(1) == 0)
    def _():
        for s in moved_start:
            up_scr[:, pl.ds(s, SUBLANES), :] = jnp.zeros((nc, SUBLANES, 2 * MXU_DIM), F32)

    x = jnp.concatenate(
        [jnp.concatenate([x_ref[k, pl.ds(v, m, stride=STRIDE), :] for k in range(slabs)], axis=1)
         for v in range(STRIDE)], axis=0)
    inv_rms = lax.rsqrt(jnp.mean(x * x, axis=-1, keepdims=True) + EPS)
    h = (x * nw_ref[...]).astype(BF16)

    def pair(ref, c):
        return jnp.concatenate([ref[:, pl.ds(half + c * MXU_DIM, MXU_DIM)] for half in (0, d_ff)], axis=1)

    def up(c):
        r = jnp.dot(h, pair(wup_ref, c), preferred_element_type=F32) * inv_rms
        up_scr[c, pl.ds(0, tm), :] = r
        for k, s in enumerate(moved_start):
            v = STRIDE - 2 + k
            up_scr[c, pl.ds(s + 1, m), :] = r[v * m:(v + 1) * m]

    def gate(c, token):
        cw = pair(cw_ref, c)
        first = [v * m for v in range(STRIDE)]
        s1 = moved_start[1:] + first[:STRIDE - 1]
        s2 = moved_start + first[:STRIDE - 2]
        for v in range(STRIDE):
            for r in range(0, m, GATE_ROWS):
                z = _chained(up_scr[c, pl.ds(first[v] + r, GATE_ROWS), :], token, MXU_DIM)
                y = (cw[0:1, :] * up_scr[c, pl.ds(s2[v] + r, GATE_ROWS), :]
                     + cw[1:2, :] * up_scr[c, pl.ds(s1[v] + r, GATE_ROWS), :] + cw[2:3, :] * z)
                g, u = y[:, :MXU_DIM], y[:, MXU_DIM:]
                a = (g / (1.0 + jnp.exp2(g * -LOG2_E)) * u).astype(BF16)
                a_scr[c, pl.ds(v * m + r, GATE_ROWS), :] = a
                token = _zero_tile(a)
        return token

    token = None
    for c in range(nc):
        up(c)
        if c >= 1:
            token = gate(c - 1, token)
    gate(nc - 1, token)

    split = (nc - 1) * MXU_DIM
    a_head = jnp.concatenate([a_scr[c] for c in range(nc - 1)], axis=1)
    y = x + jnp.dot(a_head, wd_ref[pl.ds(0, split), :], preferred_element_type=F32)
    y = y + jnp.dot(a_scr[nc - 1], wd_ref[pl.ds(split, MXU_DIM), :], preferred_element_type=F32)

    for s in moved_start:
        up_scr[:, pl.ds(s, 1), :] = up_scr[:, pl.ds(s + m, 1), :]

    for k in range(slabs):
        for v in range(STRIDE):
            o_slab[k, pl.ds(v, m, stride=STRIDE), :] = y[v * m:(v + 1) * m, k * LANES:(k + 1) * LANES]
        o_ref[:, pl.ds(k * LANES, LANES)] = o_slab[k]


def _resident(shape):
    return pl.BlockSpec(shape, lambda b, i: (0,) * len(shape), pipeline_mode=pl.Buffered(1))


def _tile(tm, width):
    return pl.BlockSpec((None, tm, width), lambda b, i: (b, i, 0))


def _params(vmem_limit_bytes=VMEM_LIMIT_BYTES):
    return pltpu.CompilerParams(dimension_semantics=("arbitrary", "arbitrary"), vmem_limit_bytes=vmem_limit_bytes)


def _layer(x, attn_norm_w, w_in, q_norm_w, k_norm_w, sinks, conv_mix_w, attn_out_norm_w,
           conv_out_norm_w, w_out, ffn_norm_w, w_up, ffn_conv_w, w_down):
    bsz, seq, d = x.shape
    d_ff = w_down.shape[0]
    tm = TOKEN_TILE
    assert seq % tm == 0 and tm % BLOCK == 0 and d_ff % MXU_DIM == 0
    n_tiles = seq // tm
    nc = d_ff // MXU_DIM

    scale = HEAD_DIM ** -0.5
    qkw = jnp.concatenate([jnp.tile(q_norm_w, N_Q_HEADS) * scale, jnp.tile(k_norm_w, N_KV_HEADS)])[None, :]
    in_hbm = pl.BlockSpec(memory_space=pl.ANY)


    this_tile = pl.BlockSpec((None, tm, d), lambda b, i: (b, jnp.minimum(i, n_tiles - 1), 0))
    last_tile = pl.BlockSpec((None, tm, d), lambda b, i: (b, jnp.maximum(i - 1, 0), 0))
    last_slab_tile = pl.BlockSpec((None, d // LANES, tm, LANES), lambda b, i: (b, 0, jnp.maximum(i - 1, 0), 0))
    slab_tile = pl.BlockSpec((None, d // LANES, tm, LANES), lambda b, i: (b, 0, i, 0))
    in_w = w_in.shape[1]
    bpt = tm // BLOCK
    score_units = (bpt, 2 * N_KV_HEADS, N_Q_HEADS // N_KV_HEADS // 2)
    x = pl.pallas_call(
        _token_mixer_kernel,
        grid=(bsz, n_tiles + 1),
        in_specs=[pl.BlockSpec(memory_space=pltpu.SMEM), this_tile, last_tile, _resident((1, d)), in_hbm,
                  _resident((1, ATTN_WIDTH + KV_WIDTH)), _resident((CONV_TAPS, CONV_WIDTH)),
                  _resident((1, CONV_WIDTH)), in_hbm, _resident((ATTN_WIDTH, 1))],
        out_specs=last_slab_tile,
        out_shape=jax.ShapeDtypeStruct((bsz, d // LANES, seq, LANES), x.dtype),
        scratch_shapes=[pltpu.VMEM((tm, in_w), F32), pltpu.VMEM((tm, in_w), F32),
                        pltpu.VMEM((tm + HALO, CONV_WIDTH), F32),
                        pltpu.VMEM((tm, ATTN_WIDTH), BF16),
                        pltpu.VMEM((BLOCK + tm, 2 * N_KV_HEADS * LANES), BF16),
                        pltpu.VMEM((KV_WIDTH, BLOCK + tm), BF16),
                        pltpu.VMEM((tm, CONV_WIDTH), BF16),
                        pltpu.VMEM(score_units + (2 * BLOCK, BLOCK), F32),
                        pltpu.VMEM(score_units + (2 * BLOCK, BLOCK), BF16),
                        pltpu.VMEM((bpt, 2 * N_KV_HEADS, SUBLANES, 2 * BLOCK), F32),
                        pltpu.VMEM((ATTN_WIDTH, tm), F32),
                        pltpu.VMEM((tm, ATTN_WIDTH), BF16),
                        pltpu.VMEM((2, 2 * BLOCK, BLOCK), jnp.int32),
                        pltpu.VMEM((d, in_w), BF16),
                        pltpu.VMEM((ATTN_WIDTH + CONV_WIDTH, d), BF16),
                        pltpu.SemaphoreType.DMA((bpt * 2 * N_KV_HEADS * 2,))],
        compiler_params=_params(),
        name="token_mixer",
    )(sinks, x, x, attn_norm_w[None, :], w_in, qkw, conv_mix_w, conv_out_norm_w[None, :], w_out,
      attn_out_norm_w[:, None])

    x = pl.pallas_call(
        _ffn_kernel,
        grid=(bsz, n_tiles),
        in_specs=[slab_tile, _resident((1, d)), in_hbm, _resident((CONV_TAPS, 2 * d_ff)), in_hbm],
        out_specs=_tile(tm, d),
        out_shape=jax.ShapeDtypeStruct((bsz, seq, d), x.dtype),
        scratch_shapes=[pltpu.VMEM((d // LANES, tm, LANES), F32),
                        pltpu.VMEM((nc, tm + 2 * (tm // STRIDE + HALO), 2 * MXU_DIM), F32),
                        pltpu.VMEM((nc, tm, MXU_DIM), BF16),
                        pltpu.VMEM((d, 2 * d_ff), BF16), pltpu.VMEM((d_ff, d), BF16),
                        pltpu.SemaphoreType.DMA((nc,))],
        compiler_params=_params(),
        name="conv_ffn",
    )(x, ffn_norm_w[None, :], w_up, ffn_conv_w, w_down)
    return x


def kernel(x, attn_norm_w, w_in, q_norm_w, k_norm_w, sinks, conv_mix_w, attn_out_norm_w,
           conv_out_norm_w, w_out, ffn_norm_w, w_up, ffn_conv_w, w_down):
    for l in range(attn_norm_w.shape[0]):
        x = _layer(x, attn_norm_w[l], w_in[l], q_norm_w[l], k_norm_w[l], sinks[l], conv_mix_w[l],
                   attn_out_norm_w[l], conv_out_norm_w[l], w_out[l], ffn_norm_w[l], w_up[l],
                   ffn_conv_w[l], w_down[l])
    return x
```

```python
import jax
import jax.numpy as jnp
from jax import lax
from jax.experimental import pallas as pl
from jax.experimental.pallas import tpu as pltpu

HEAD_DIM = 64
N_Q_HEADS = 8
N_KV_HEADS = 2
WINDOW = 128
BLOCK = 128
ATTN_WIDTH = N_Q_HEADS * HEAD_DIM
KV_WIDTH = N_KV_HEADS * HEAD_DIM
CONV_WIDTH = 512
CONV_TAPS = 3
EPS = 1e-6
NEG_INF = -1e30
LOG2_E = 1.4426950408889634

LANES = 128
SUBLANES = 8
MXU_DIM = 256
VMEM_LIMIT_BYTES = 56 * 1024 * 1024

TOKEN_TILE = 512
HALO = SUBLANES
STRIDE = 4
GATE_ROWS = 64
QKV_ROWS = 64
PIECE_ROWS = 32
PACKED_ROWS = 2 * SUBLANES

F32 = jnp.float32
BF16 = jnp.bfloat16


def _rms(x, w):
    ms = jnp.mean(x * x, axis=-1, keepdims=True)
    return x * lax.rsqrt(ms + EPS) * w


def _zero_tile(packed):
    words = pltpu.bitcast(packed, jnp.uint32)
    while words.shape[1] > LANES:
        half = words.shape[1] // 2
        words = words[:, :half] | words[:, half:]
    while words.shape[0] > SUBLANES:
        half = words.shape[0] // 2
        words = words[:half] | words[half:]
    return ((words >> 16) >> 16).astype(F32)


def _chained(x, zero_tile, width=LANES):
    if zero_tile is None:
        return x
    zeros = jnp.concatenate([jnp.concatenate([zero_tile] * (x.shape[0] // SUBLANES), axis=0)] * (width // LANES),
                            axis=1)
    if width == x.shape[1]:
        return x + zeros
    return jnp.concatenate([x[:, :width] + zeros, x[:, width:]], axis=1)


def _fetch_bf16(w_hbm, w_bf16, slots, sems):
    rows, cols = slots[0].shape
    blocks = [(r, c) for r in range(0, w_hbm.shape[0], rows) for c in range(0, w_hbm.shape[1], cols)]
    assert w_hbm.shape[0] % rows == 0 and w_hbm.shape[1] % cols == 0

    def copy(b):
        r, c = blocks[b]
        s = b % len(slots)
        return pltpu.make_async_copy(w_hbm.at[pl.ds(r, rows), pl.ds(c, cols)], slots[s], sems.at[s])

    for b in range(min(len(slots), len(blocks))):
        copy(b).start()
    for b, (r, c) in enumerate(blocks):
        copy(b).wait()
        w_bf16[pl.ds(r, rows), pl.ds(c, cols)] = slots[b % len(slots)][...].astype(BF16)
        if b + len(slots) < len(blocks):
            copy(b + len(slots)).start()


def _first_step():
    return (pl.program_id(0) == 0) & (pl.program_id(1) == 0)


def _token_mixer_kernel(sinks_ref, x_ref, nw_ref, win_hbm, qkw_ref, cw_ref, cnw_ref, wo_hbm, anw_ref,
                        o_ref, proj_a, proj_b, x_a, x_b, cu_scr, q_scr, kv_scr, vt_scr, cn_scr,
                        s_scr, p_scr, inv_scr, attn_t, an_scr, mask_scr, win_ref, wo_ref, w_sems):
    tm = x_ref.shape[0]
    qkv_w = ATTN_WIDTH + 2 * KV_WIDTH
    step = pl.program_id(1)

    @pl.when(_first_step())
    def _():
        _fetch_bf16(win_hbm, win_ref, [proj_a, proj_b], w_sems)
        slots = [s_scr.at[j, ce, g] for j in range(s_scr.shape[0]) for ce in range(s_scr.shape[1])
                 for g in range(s_scr.shape[2])]
        _fetch_bf16(wo_hbm, wo_ref, slots, w_sems)

    @pl.when(step == 0)
    def _():
        proj_b[...] = jnp.zeros(proj_b.shape, F32)
        x_b[...] = jnp.zeros(x_b.shape, F32)
        cu_scr[pl.ds(0, HALO), :] = jnp.zeros((HALO, CONV_WIDTH), F32)
        kv_scr[pl.ds(0, BLOCK), :] = jnp.zeros((BLOCK, kv_scr.shape[1]), BF16)
        vt_scr[:, pl.ds(0, BLOCK)] = jnp.zeros((KV_WIDTH, BLOCK), BF16)

    def body(fill, drain, x_fill, x_drain):
        x_fill[...] = x_ref[...]
        h = _rms(x_ref[...], nw_ref[...]).astype(BF16)
        fill[...] = jnp.dot(h, win_ref[...], preferred_element_type=F32)

        low = lax.broadcasted_iota(jnp.int32, (QKV_ROWS, LANES), 1) < HEAD_DIM
        for r in range(0, tm, QKV_ROWS):
            rows = pl.ds(r, QKV_ROWS)
            qk = drain[rows, pl.ds(0, ATTN_WIDTH + KV_WIDTH)]
            sums = []
            for t in range(0, ATTN_WIDTH + KV_WIDTH, LANES):
                sq = qk[:, t:t + LANES] * qk[:, t:t + LANES]
                sums.append(jnp.where(low, jnp.sum(jnp.where(low, sq, 0.0), axis=-1, keepdims=True),
                                      jnp.sum(jnp.where(low, 0.0, sq), axis=-1, keepdims=True)))
            qkn = qk * lax.rsqrt(jnp.concatenate(sums, axis=1) * (1.0 / HEAD_DIM) + EPS) * qkw_ref[...]
            qn = qkn[:, :ATTN_WIDTH].astype(BF16)
            q_scr[rows, :] = qn
            k = qkn[:, ATTN_WIDTH:]
            swapped = pltpu.roll(k, HEAD_DIM, axis=1)
            kk = jnp.concatenate([jnp.where(low, k, 0.0), jnp.where(low, 0.0, swapped),
                                  jnp.where(low, swapped, 0.0), jnp.where(low, 0.0, k)], axis=1).astype(BF16)
            kv_scr[pl.ds(BLOCK + r, QKV_ROWS), :] = kk
        vt_scr[:, pl.ds(BLOCK, tm)] = drain[:, pl.ds(ATTN_WIDTH + KV_WIDTH, KV_WIDTH)].T.astype(BF16)

        cw = cw_ref[...]
        token = None
        for r in range(0, tm, PIECE_ROWS):
            rows = pl.ds(r, PIECE_ROWS)
            bg = _chained(drain[rows, pl.ds(qkv_w, CONV_WIDTH)], token)
            cu_scr[pl.ds(HALO + r, PIECE_ROWS), :] = (drain[rows, pl.ds(qkv_w + CONV_WIDTH, CONV_WIDTH)]
                                                      * drain[rows, pl.ds(qkv_w + 2 * CONV_WIDTH, CONV_WIDTH)])
            y = cw[0:1, :] * cu_scr[pl.ds(HALO + r - 2, PIECE_ROWS), :]
            y = y + cw[1:2, :] * cu_scr[pl.ds(HALO + r - 1, PIECE_ROWS), :]
            y = y + cw[2:3, :] * cu_scr[pl.ds(HALO + r, PIECE_ROWS), :]
            cn = _rms(bg * y, cnw_ref[...]).astype(BF16)
            cn_scr[rows, :] = cn
            token = _zero_tile(cn)
        cu_scr[pl.ds(0, HALO), :] = cu_scr[pl.ds(tm, HALO), :]

        _attend(step > 1, sinks_ref, q_scr, kv_scr, vt_scr, s_scr, p_scr, inv_scr, attn_t, an_scr, mask_scr,
                anw_ref)
        y = jnp.dot(an_scr[...], wo_ref[pl.ds(0, ATTN_WIDTH), :], preferred_element_type=F32)
        y = y + jnp.dot(cn_scr[...], wo_ref[pl.ds(ATTN_WIDTH, CONV_WIDTH), :], preferred_element_type=F32)
        y = x_drain[...] + y
        for k in range(o_ref.shape[0]):
            o_ref[k] = y[:, k * LANES:(k + 1) * LANES]
        kv_scr[pl.ds(0, BLOCK), :] = kv_scr[pl.ds(tm, BLOCK), :]
        vt_scr[:, pl.ds(0, BLOCK)] = vt_scr[:, pl.ds(tm, BLOCK)]

    @pl.when(step % 2 == 0)
    def _():
        body(proj_a, proj_b, x_a, x_b)

    @pl.when(step % 2 == 1)
    def _():
        body(proj_b, proj_a, x_b, x_a)


def _all_sublanes(x, op):
    for shift in (4, 2, 1):
        x = op(x, pltpu.roll(x, shift, axis=0))
    return x


def _attend(has_prev, sinks_ref, q_ref, kv_ref, vt_ref, s_scr, p_scr, inv_scr, attn_t, an_scr, mask_scr, anw_ref):
    tq = q_ref.shape[0]
    nb = tq // BLOCK
    group = N_Q_HEADS // N_KV_HEADS
    group_pairs = group // 2
    key_tiles = 2 * BLOCK // SUBLANES
    nt = (((1,), (1,)), ((), ()))

    sj = lax.broadcasted_iota(jnp.int32, (2 * BLOCK, BLOCK), 0)
    qi = lax.broadcasted_iota(jnp.int32, (2 * BLOCK, BLOCK), 1)
    rel = qi + BLOCK - sj
    band = (rel >= 0) & (rel < WINDOW)
    band0 = band & ((sj >= BLOCK) | has_prev)
    mask_scr[0] = band0.astype(jnp.int32)
    mask_scr[1] = band.astype(jnp.int32)

    def scores(j):
        rows = pl.ds(j * BLOCK, BLOCK)
        for c in range(N_KV_HEADS):
            qs = jnp.concatenate(
                [q_ref[rows, pl.ds((group_pairs * c + g) * LANES, LANES)] for g in range(group_pairs)], axis=0)
            for e in range(2):
                k = kv_ref[pl.ds(j * BLOCK, 2 * BLOCK), pl.ds((2 * c + e) * LANES, LANES)]
                s = lax.dot_general(k, qs, nt, preferred_element_type=F32)
                for g in range(group_pairs):
                    s_scr[j, 2 * c + e, g] = s[:, g * BLOCK:(g + 1) * BLOCK]

    def softmax(j):
        for ce, g in [(ce, g) for ce in range(2 * N_KV_HEADS) for g in range(group_pairs)]:
            c, e = divmod(ce, 2)
            cols = pl.ds(g * BLOCK, BLOCK)
            sink = sinks_ref[group * c + 2 * g + e]
            m = None
            for r in range(key_tiles):
                rows = pl.ds(r * SUBLANES, SUBLANES)
                s = jnp.where(mask_scr[min(j, 1), rows, :] > 0, s_scr[j, ce, g, rows, :], NEG_INF)
                s_scr[j, ce, g, rows, :] = s
                m = s if m is None else jnp.maximum(m, s)
            m = jnp.maximum(_all_sublanes(m, jnp.maximum), sink)
            m2 = jnp.concatenate([m, m], axis=0)
            total = None
            for r in range(0, 2 * BLOCK, PACKED_ROWS):
                rows = pl.ds(r, PACKED_ROWS)
                pe = jnp.exp(s_scr[j, ce, g, rows, :] - m2)
                p_scr[j, ce, g, rows, :] = pe.astype(BF16)
                part = pe[:SUBLANES] + pe[SUBLANES:]
                total = part if total is None else total + part
            denom = _all_sublanes(total, jnp.add) + jnp.exp(sink - m)
            inv_scr[j, ce, :, cols] = 1.0 / denom

    def values(j):
        for c in range(N_KV_HEADS):
            vt = vt_ref[pl.ds(c * HEAD_DIM, HEAD_DIM), pl.ds(j * BLOCK, 2 * BLOCK)]
            for e in range(2):
                inv = jnp.concatenate([inv_scr[j, 2 * c + e]] * (HEAD_DIM // SUBLANES), axis=0)
                p = jnp.concatenate([p_scr[j, 2 * c + e, g] for g in range(group_pairs)], axis=1)
                o = jnp.dot(vt, p, preferred_element_type=F32) * inv
                for g in range(group_pairs):
                    attn_t[pl.ds((group * c + 2 * g + e) * HEAD_DIM, HEAD_DIM), pl.ds(j * BLOCK, BLOCK)] = (
                        o[:, g * BLOCK:(g + 1) * BLOCK])
        a = attn_t[:, pl.ds(j * BLOCK, BLOCK)]
        ms = jnp.mean(a * a, axis=0, keepdims=True)
        an_scr[pl.ds(j * BLOCK, BLOCK), :] = (a * lax.rsqrt(ms + EPS) * anw_ref[...]).T.astype(BF16)

    for t in range(nb + 2):
        if t < nb:
            scores(t)
        if 0 <= t - 1 < nb:
            softmax(t - 1)
        if 0 <= t - 2 < nb:
            values(t - 2)


def _ffn_kernel(x_ref, nw_ref, wup_hbm, cw_ref, wd_hbm, o_ref, o_slab, up_scr, a_scr,
                wup_ref, wd_ref, w_sems):
    slabs, tm, _ = x_ref.shape
    m = tm // STRIDE
    d_ff = wd_ref.shape[0]
    nc = d_ff // MXU_DIM
    moved_start = [STRIDE * m + k * (m + HALO) for k in range(2)]

    @pl.when(_first_step())
    def _():
        _fetch_bf16(wup_hbm, wup_ref, [up_scr.at[c, pl.ds(0, tm), :] for c in range(nc)], w_sems)
        _fetch_bf16(wd_hbm, wd_ref, [up_scr.at[c, pl.ds(0, MXU_DIM), :] for c in range(nc)], w_sems)

    @pl.when(pl.program_id(1) == 0)
    def _():
        for s in moved_start:
            up_scr[:, pl.ds(s, SUBLANES), :] = jnp.zeros((nc, SUBLANES, 2 * MXU_DIM), F32)

    x = jnp.concatenate(
        [jnp.concatenate([x_ref[k, pl.ds(v, m, stride=STRIDE), :] for k in range(slabs)], axis=1)
         for v in range(STRIDE)], axis=0)
    inv_rms = lax.rsqrt(jnp.mean(x * x, axis=-1, keepdims=True) + EPS)
    h = (x * nw_ref[...]).astype(BF16)

    def pair(ref, c):
        return jnp.concatenate([ref[:, pl.ds(half + c * MXU_DIM, MXU_DIM)] for half in (0, d_ff)], axis=1)

    def up(c):
        r = jnp.dot(h, pair(wup_ref, c), preferred_element_type=F32) * inv_rms
        up_scr[c, pl.ds(0, tm), :] = r
        for k, s in enumerate(moved_start):
            v = STRIDE - 2 + k
            up_scr[c, pl.ds(s + 1, m), :] = r[v * m:(v + 1) * m]

    def gate(c, token):
        cw = pair(cw_ref, c)
        first = [v * m for v in range(STRIDE)]
        s1 = moved_start[1:] + first[:STRIDE - 1]
        s2 = moved_start + first[:STRIDE - 2]
        for v in range(STRIDE):
            for r in range(0, m, GATE_ROWS):
                z = _chained(up_scr[c, pl.ds(first[v] + r, GATE_ROWS), :], token, MXU_DIM)
                y = (cw[0:1, :] * up_scr[c, pl.ds(s2[v] + r, GATE_ROWS), :]
                     + cw[1:2, :] * up_scr[c, pl.ds(s1[v] + r, GATE_ROWS), :] + cw[2:3, :] * z)
                g, u = y[:, :MXU_DIM], y[:, MXU_DIM:]
                a = (g / (1.0 + jnp.exp2(g * -LOG2_E)) * u).astype(BF16)
                a_scr[c, pl.ds(v * m + r, GATE_ROWS), :] = a
                token = _zero_tile(a)
        return token

    token = None
    for c in range(nc):
        up(c)
        if c >= 1:
            token = gate(c - 1, token)
    gate(nc - 1, token)

    split = (nc - 1) * MXU_DIM
    a_head = jnp.concatenate([a_scr[c] for c in range(nc - 1)], axis=1)
    y = x + jnp.dot(a_head, wd_ref[pl.ds(0, split), :], preferred_element_type=F32)
    y = y + jnp.dot(a_scr[nc - 1], wd_ref[pl.ds(split, MXU_DIM), :], preferred_element_type=F32)

    for s in moved_start:
        up_scr[:, pl.ds(s, 1), :] = up_scr[:, pl.ds(s + m, 1), :]

    for k in range(slabs):
        for v in range(STRIDE):
            o_slab[k, pl.ds(v, m, stride=STRIDE), :] = y[v * m:(v + 1) * m, k * LANES:(k + 1) * LANES]
        o_ref[:, pl.ds(k * LANES, LANES)] = o_slab[k]


def _resident(shape):
    return pl.BlockSpec(shape, lambda b, i: (0,) * len(shape), pipeline_mode=pl.Buffered(1))


def _tile(tm, width):
    return pl.BlockSpec((None, tm, width), lambda b, i: (b, i, 0))


def _params(vmem_limit_bytes=VMEM_LIMIT_BYTES):
    return pltpu.CompilerParams(dimension_semantics=("arbitrary", "arbitrary"), vmem_limit_bytes=vmem_limit_bytes)


def _layer(x, attn_norm_w, w_in, q_norm_w, k_norm_w, sinks, conv_mix_w, attn_out_norm_w,
           conv_out_norm_w, w_out, ffn_norm_w, w_up, ffn_conv_w, w_down):
    bsz, seq, d = x.shape
    d_ff = w_down.shape[0]
    tm = TOKEN_TILE
    assert seq % tm == 0 and tm % BLOCK == 0 and d_ff % MXU_DIM == 0
    n_tiles = seq // tm
    nc = d_ff // MXU_DIM

    scale = HEAD_DIM ** -0.5
    qkw = jnp.concatenate([jnp.tile(q_norm_w, N_Q_HEADS) * scale, jnp.tile(k_norm_w, N_KV_HEADS)])[None, :]
    in_hbm = pl.BlockSpec(memory_space=pl.ANY)


    this_tile = pl.BlockSpec((None, tm, d), lambda b, i: (b, jnp.minimum(i, n_tiles - 1), 0))
    last_slab_tile = pl.BlockSpec((None, d // LANES, tm, LANES), lambda b, i: (b, 0, jnp.maximum(i - 1, 0), 0))
    slab_tile = pl.BlockSpec((None, d // LANES, tm, LANES), lambda b, i: (b, 0, i, 0))
    in_w = w_in.shape[1]
    bpt = tm // BLOCK
    score_units = (bpt, 2 * N_KV_HEADS, N_Q_HEADS // N_KV_HEADS // 2)
    x = pl.pallas_call(
        _token_mixer_kernel,
        grid=(bsz, n_tiles + 1),
        in_specs=[pl.BlockSpec(memory_space=pltpu.SMEM), this_tile, _resident((1, d)), in_hbm,
                  _resident((1, ATTN_WIDTH + KV_WIDTH)), _resident((CONV_TAPS, CONV_WIDTH)),
                  _resident((1, CONV_WIDTH)), in_hbm, _resident((ATTN_WIDTH, 1))],
        out_specs=last_slab_tile,
        out_shape=jax.ShapeDtypeStruct((bsz, d // LANES, seq, LANES), x.dtype),
        scratch_shapes=[pltpu.VMEM((tm, in_w), F32), pltpu.VMEM((tm, in_w), F32),
                        pltpu.VMEM((tm, d), F32), pltpu.VMEM((tm, d), F32),
                        pltpu.VMEM((tm + HALO, CONV_WIDTH), F32),
                        pltpu.VMEM((tm, ATTN_WIDTH), BF16),
                        pltpu.VMEM((BLOCK + tm, 2 * N_KV_HEADS * LANES), BF16),
                        pltpu.VMEM((KV_WIDTH, BLOCK + tm), BF16),
                        pltpu.VMEM((tm, CONV_WIDTH), BF16),
                        pltpu.VMEM(score_units + (2 * BLOCK, BLOCK), F32),
                        pltpu.VMEM(score_units + (2 * BLOCK, BLOCK), BF16),
                        pltpu.VMEM((bpt, 2 * N_KV_HEADS, SUBLANES, 2 * BLOCK), F32),
                        pltpu.VMEM((ATTN_WIDTH, tm), F32),
                        pltpu.VMEM((tm, ATTN_WIDTH), BF16),
                        pltpu.VMEM((2, 2 * BLOCK, BLOCK), jnp.int32),
                        pltpu.VMEM((d, in_w), BF16),
                        pltpu.VMEM((ATTN_WIDTH + CONV_WIDTH, d), BF16),
                        pltpu.SemaphoreType.DMA((bpt * 2 * N_KV_HEADS * 2,))],
        compiler_params=_params(),
        name="token_mixer",
    )(sinks, x, attn_norm_w[None, :], w_in, qkw, conv_mix_w, conv_out_norm_w[None, :], w_out,
      attn_out_norm_w[:, None])

    x = pl.pallas_call(
        _ffn_kernel,
        grid=(bsz, n_tiles),
        in_specs=[slab_tile, _resident((1, d)), in_hbm, _resident((CONV_TAPS, 2 * d_ff)), in_hbm],
        out_specs=_tile(tm, d),
        out_shape=jax.ShapeDtypeStruct((bsz, seq, d), x.dtype),
        scratch_shapes=[pltpu.VMEM((d // LANES, tm, LANES), F32),
                        pltpu.VMEM((nc, tm + 2 * (tm // STRIDE + HALO), 2 * MXU_DIM), F32),
                        pltpu.VMEM((nc, tm, MXU_DIM), BF16),
                        pltpu.VMEM((d, 2 * d_ff), BF16), pltpu.VMEM((d_ff, d), BF16),
                        pltpu.SemaphoreType.DMA((nc,))],
        compiler_params=_params(),
        name="conv_ffn",
    )(x, ffn_norm_w[None, :], w_up, ffn_conv_w, w_down)
    return x


def kernel(x, attn_norm_w, w_in, q_norm_w, k_norm_w, sinks, conv_mix_w, attn_out_norm_w,
           conv_out_norm_w, w_out, ffn_norm_w, w_up, ffn_conv_w, w_down):
    for l in range(attn_norm_w.shape[0]):
        x = _layer(x, attn_norm_w[l], w_in[l], q_norm_w[l], k_norm_w[l], sinks[l], conv_mix_w[l],
                   attn_out_norm_w[l], conv_out_norm_w[l], w_out[l], ffn_norm_w[l], w_up[l],
                   ffn_conv_w[l], w_down[l])
    return x
```

```python
import functools

import jax
import jax.numpy as jnp
from jax import lax
from jax.experimental import pallas as pl
from jax.experimental.pallas import tpu as pltpu

HEAD_DIM = 64
N_Q_HEADS = 8
N_KV_HEADS = 2
WINDOW = 128
BLOCK = 128
ATTN_WIDTH = N_Q_HEADS * HEAD_DIM
KV_WIDTH = N_KV_HEADS * HEAD_DIM
CONV_WIDTH = 512
CONV_TAPS = 3
EPS = 1e-6
NEG_INF = -1e30
LOG2_E = 1.4426950408889634

LANES = 128
SUBLANES = 8
MXU_DIM = 256
VMEM_LIMIT_BYTES = 56 * 1024 * 1024

TOKEN_TILE = 512
HALO = SUBLANES
STRIDE = 4
GATE_ROWS = 64
QKV_ROWS = 64
PIECE_ROWS = 32
PACKED_ROWS = 2 * SUBLANES
WEIGHT_COPIES = 8

F32 = jnp.float32
BF16 = jnp.bfloat16


def _rms(x, w):
    ms = jnp.mean(x * x, axis=-1, keepdims=True)
    return x * lax.rsqrt(ms + EPS) * w


def _zero_tile(packed):
    words = pltpu.bitcast(packed, jnp.uint32)
    while words.shape[1] > LANES:
        half = words.shape[1] // 2
        words = words[:, :half] | words[:, half:]
    while words.shape[0] > SUBLANES:
        half = words.shape[0] // 2
        words = words[:half] | words[half:]
    return ((words >> 16) >> 16).astype(F32)


def _chained(x, zero_tile, width=LANES):
    if zero_tile is None:
        return x
    zeros = jnp.concatenate([jnp.concatenate([zero_tile] * (x.shape[0] // SUBLANES), axis=0)] * (width // LANES),
                            axis=1)
    if width == x.shape[1]:
        return x + zeros
    return jnp.concatenate([x[:, :width] + zeros, x[:, width:]], axis=1)


def _fetch_bf16(w_hbm, w_bf16, slots, sems):
    rows, cols = slots[0].shape
    blocks = [(r, c) for r in range(0, w_hbm.shape[0], rows) for c in range(0, w_hbm.shape[1], cols)]
    assert w_hbm.shape[0] % rows == 0 and w_hbm.shape[1] % cols == 0

    def copy(b):
        r, c = blocks[b]
        s = b % len(slots)
        return pltpu.make_async_copy(w_hbm.at[pl.ds(r, rows), pl.ds(c, cols)], slots[s], sems.at[s])

    for b in range(min(len(slots), len(blocks))):
        copy(b).start()
    for b, (r, c) in enumerate(blocks):
        copy(b).wait()
        w_bf16[pl.ds(r, rows), pl.ds(c, cols)] = slots[b % len(slots)][...].astype(BF16)
        if b + len(slots) < len(blocks):
            copy(b + len(slots)).start()


def _first_step():
    return (pl.program_id(0) == 0) & (pl.program_id(1) == 0)


def _token_mixer_kernel(n_chunks, sinks_ref, x_ref, xl_ref, nw_ref, win_hbm, qkw_ref, cw_ref, cnw_ref, wo_hbm,
                        anw_ref, wup_hbm, wd_hbm, o_ref, wup_bf, wd_bf, proj_a, proj_b, cu_scr, q_scr, kv_scr,
                        vt_scr, cn_scr, s_scr, p_scr, inv_scr, attn_t, an_scr, mask_scr, win_ref, wo_ref, w_sems,
                        stage_up, stage_dn, cvt_up, cvt_dn, c_sems):
    tm = x_ref.shape[0]
    qkv_w = ATTN_WIDTH + 2 * KV_WIDTH
    step = pl.program_id(1)
    convert = pl.program_id(0) == 0

    def conversion(w):
        src, dst, stage, cvt, chunks = [(wup_hbm, wup_bf, stage_up, cvt_up, n_chunks),
                                        (wd_hbm, wd_bf, stage_dn, cvt_dn, n_chunks // 2)][w]
        rows = src.shape[0] // chunks

        def fetch(chunk):
            return pltpu.make_async_copy(src.at[pl.ds(chunk * rows, rows), :], stage.at[chunk % 2],
                                         c_sems.at[w, chunk % 2])

        def emit(chunk):
            return pltpu.make_async_copy(cvt, dst.at[pl.ds(chunk * rows, rows), :], c_sems.at[2, w])

        return fetch, emit, chunks

    for w in range(2):
        fetch, emit, chunks = conversion(w)

        @pl.when(convert & (step == 0))
        def _():
            fetch(0).start()

        @pl.when(convert & (step >= 1) & (step <= chunks))
        def _():
            emit(step - 1).wait()

        @pl.when(convert & (step < chunks))
        def _():
            fetch(step).wait()

        @pl.when(convert & (step + 1 < chunks))
        def _():
            fetch(step + 1).start()

    @pl.when(_first_step())
    def _():
        _fetch_bf16(win_hbm, win_ref, [proj_a, proj_b], w_sems)
        slots = [s_scr.at[j, ce, g] for j in range(s_scr.shape[0]) for ce in range(s_scr.shape[1])
                 for g in range(s_scr.shape[2])]
        _fetch_bf16(wo_hbm, wo_ref, slots, w_sems)

    @pl.when(step == 0)
    def _():
        proj_b[...] = jnp.zeros(proj_b.shape, F32)
        cu_scr[pl.ds(0, HALO), :] = jnp.zeros((HALO, CONV_WIDTH), F32)
        kv_scr[pl.ds(0, BLOCK), :] = jnp.zeros((BLOCK, kv_scr.shape[1]), BF16)
        vt_scr[:, pl.ds(0, BLOCK)] = jnp.zeros((KV_WIDTH, BLOCK), BF16)

    def body(fill, drain, cur):
        cvt_up[...] = stage_up[cur].astype(BF16)
        cvt_dn[...] = stage_dn[cur].astype(BF16)

        h = _rms(x_ref[...], nw_ref[...]).astype(BF16)
        fill[...] = jnp.dot(h, win_ref[...], preferred_element_type=F32)

        low = lax.broadcasted_iota(jnp.int32, (QKV_ROWS, LANES), 1) < HEAD_DIM
        for r in range(0, tm, QKV_ROWS):
            rows = pl.ds(r, QKV_ROWS)
            qk = drain[rows, pl.ds(0, ATTN_WIDTH + KV_WIDTH)]
            sums = []
            for t in range(0, ATTN_WIDTH + KV_WIDTH, LANES):
                sq = qk[:, t:t + LANES] * qk[:, t:t + LANES]
                sums.append(jnp.where(low, jnp.sum(jnp.where(low, sq, 0.0), axis=-1, keepdims=True),
                                      jnp.sum(jnp.where(low, 0.0, sq), axis=-1, keepdims=True)))
            qkn = qk * lax.rsqrt(jnp.concatenate(sums, axis=1) * (1.0 / HEAD_DIM) + EPS) * qkw_ref[...]
            qn = qkn[:, :ATTN_WIDTH].astype(BF16)
            q_scr[rows, :] = qn
            k = qkn[:, ATTN_WIDTH:]
            swapped = pltpu.roll(k, HEAD_DIM, axis=1)
            kk = jnp.concatenate([jnp.where(low, k, 0.0), jnp.where(low, 0.0, swapped),
                                  jnp.where(low, swapped, 0.0), jnp.where(low, 0.0, k)], axis=1).astype(BF16)
            kv_scr[pl.ds(BLOCK + r, QKV_ROWS), :] = kk
        vt_scr[:, pl.ds(BLOCK, tm)] = drain[:, pl.ds(ATTN_WIDTH + KV_WIDTH, KV_WIDTH)].T.astype(BF16)

        cw = cw_ref[...]
        token = None
        for r in range(0, tm, PIECE_ROWS):
            rows = pl.ds(r, PIECE_ROWS)
            bg = _chained(drain[rows, pl.ds(qkv_w, CONV_WIDTH)], token)
            cu_scr[pl.ds(HALO + r, PIECE_ROWS), :] = (drain[rows, pl.ds(qkv_w + CONV_WIDTH, CONV_WIDTH)]
                                                      * drain[rows, pl.ds(qkv_w + 2 * CONV_WIDTH, CONV_WIDTH)])
            y = cw[0:1, :] * cu_scr[pl.ds(HALO + r - 2, PIECE_ROWS), :]
            y = y + cw[1:2, :] * cu_scr[pl.ds(HALO + r - 1, PIECE_ROWS), :]
            y = y + cw[2:3, :] * cu_scr[pl.ds(HALO + r, PIECE_ROWS), :]
            cn = _rms(bg * y, cnw_ref[...]).astype(BF16)
            cn_scr[rows, :] = cn
            token = _zero_tile(cn)
        cu_scr[pl.ds(0, HALO), :] = cu_scr[pl.ds(tm, HALO), :]

        _attend(step > 1, sinks_ref, q_scr, kv_scr, vt_scr, s_scr, p_scr, inv_scr, attn_t, an_scr, mask_scr,
                anw_ref)
        y = jnp.dot(an_scr[...], wo_ref[pl.ds(0, ATTN_WIDTH), :], preferred_element_type=F32)
        y = y + jnp.dot(cn_scr[...], wo_ref[pl.ds(ATTN_WIDTH, CONV_WIDTH), :], preferred_element_type=F32)
        y = xl_ref[...] + y
        for k in range(o_ref.shape[0]):
            o_ref[k] = y[:, k * LANES:(k + 1) * LANES]
        kv_scr[pl.ds(0, BLOCK), :] = kv_scr[pl.ds(tm, BLOCK), :]
        vt_scr[:, pl.ds(0, BLOCK)] = vt_scr[:, pl.ds(tm, BLOCK)]

    @pl.when(step % 2 == 0)
    def _():
        body(proj_a, proj_b, 0)

    @pl.when(step % 2 == 1)
    def _():
        body(proj_b, proj_a, 1)

    for w in range(2):
        _, emit, chunks = conversion(w)

        @pl.when(convert & (step < chunks))
        def _():
            emit(step).start()


def _all_sublanes(x, op):
    for shift in (4, 2, 1):
        x = op(x, pltpu.roll(x, shift, axis=0))
    return x


def _attend(has_prev, sinks_ref, q_ref, kv_ref, vt_ref, s_scr, p_scr, inv_scr, attn_t, an_scr, mask_scr, anw_ref):
    tq = q_ref.shape[0]
    nb = tq // BLOCK
    group = N_Q_HEADS // N_KV_HEADS
    group_pairs = group // 2
    key_tiles = 2 * BLOCK // SUBLANES
    nt = (((1,), (1,)), ((), ()))

    sj = lax.broadcasted_iota(jnp.int32, (2 * BLOCK, BLOCK), 0)
    qi = lax.broadcasted_iota(jnp.int32, (2 * BLOCK, BLOCK), 1)
    rel = qi + BLOCK - sj
    band = (rel >= 0) & (rel < WINDOW)
    band0 = band & ((sj >= BLOCK) | has_prev)
    mask_scr[0] = band0.astype(jnp.int32)
    mask_scr[1] = band.astype(jnp.int32)

    def scores(j):
        rows = pl.ds(j * BLOCK, BLOCK)
        for c in range(N_KV_HEADS):
            qs = jnp.concatenate(
                [q_ref[rows, pl.ds((group_pairs * c + g) * LANES, LANES)] for g in range(group_pairs)], axis=0)
            for e in range(2):
                k = kv_ref[pl.ds(j * BLOCK, 2 * BLOCK), pl.ds((2 * c + e) * LANES, LANES)]
                s = lax.dot_general(k, qs, nt, preferred_element_type=F32)
                for g in range(group_pairs):
                    s_scr[j, 2 * c + e, g] = s[:, g * BLOCK:(g + 1) * BLOCK]

    def softmax(j):
        for ce, g in [(ce, g) for ce in range(2 * N_KV_HEADS) for g in range(group_pairs)]:
            c, e = divmod(ce, 2)
            cols = pl.ds(g * BLOCK, BLOCK)
            sink = sinks_ref[group * c + 2 * g + e]
            m = None
            for r in range(key_tiles):
                rows = pl.ds(r * SUBLANES, SUBLANES)
                s = jnp.where(mask_scr[min(j, 1), rows, :] > 0, s_scr[j, ce, g, rows, :], NEG_INF)
                s_scr[j, ce, g, rows, :] = s
                m = s if m is None else jnp.maximum(m, s)
            m = jnp.maximum(_all_sublanes(m, jnp.maximum), sink)
            m2 = jnp.concatenate([m, m], axis=0)
            total = None
            for r in range(0, 2 * BLOCK, PACKED_ROWS):
                rows = pl.ds(r, PACKED_ROWS)
                pe = jnp.exp(s_scr[j, ce, g, rows, :] - m2)
                p_scr[j, ce, g, rows, :] = pe.astype(BF16)
                part = pe[:SUBLANES] + pe[SUBLANES:]
                total = part if total is None else total + part
            denom = _all_sublanes(total, jnp.add) + jnp.exp(sink - m)
            inv_scr[j, ce, :, cols] = 1.0 / denom

    def values(j):
        for c in range(N_KV_HEADS):
            vt = vt_ref[pl.ds(c * HEAD_DIM, HEAD_DIM), pl.ds(j * BLOCK, 2 * BLOCK)]
            for e in range(2):
                inv = jnp.concatenate([inv_scr[j, 2 * c + e]] * (HEAD_DIM // SUBLANES), axis=0)
                p = jnp.concatenate([p_scr[j, 2 * c + e, g] for g in range(group_pairs)], axis=1)
                o = jnp.dot(vt, p, preferred_element_type=F32) * inv
                for g in range(group_pairs):
                    attn_t[pl.ds((group * c + 2 * g + e) * HEAD_DIM, HEAD_DIM), pl.ds(j * BLOCK, BLOCK)] = (
                        o[:, g * BLOCK:(g + 1) * BLOCK])
        a = attn_t[:, pl.ds(j * BLOCK, BLOCK)]
        ms = jnp.mean(a * a, axis=0, keepdims=True)
        an_scr[pl.ds(j * BLOCK, BLOCK), :] = (a * lax.rsqrt(ms + EPS) * anw_ref[...]).T.astype(BF16)

    for t in range(nb + 2):
        if t < nb:
            scores(t)
        if 0 <= t - 1 < nb:
            softmax(t - 1)
        if 0 <= t - 2 < nb:
            values(t - 2)


def _ffn_kernel(x_ref, nw_ref, wup_hbm, cw_ref, wd_hbm, o_ref, o_slab, up_scr, a_scr,
                wup_ref, wd_ref, w_sems):
    slabs, tm, _ = x_ref.shape
    m = tm // STRIDE
    d_ff = wd_ref.shape[0]
    nc = d_ff // MXU_DIM
    moved_start = [STRIDE * m + k * (m + HALO) for k in range(2)]

    @pl.when(_first_step())
    def _():
        copies = []
        for i, (src, dst) in enumerate([(wup_hbm, wup_ref), (wd_hbm, wd_ref)]):
            rows = src.shape[0] // WEIGHT_COPIES
            copies += [pltpu.make_async_copy(src.at[pl.ds(k * rows, rows), :], dst.at[pl.ds(k * rows, rows), :],
                                             w_sems.at[i, k]) for k in range(WEIGHT_COPIES)]
        for copy in copies:
            copy.start()
        for copy in copies:
            copy.wait()

    @pl.when(pl.program_id(1) == 0)
    def _():
        for s in moved_start:
            up_scr[:, pl.ds(s, SUBLANES), :] = jnp.zeros((nc, SUBLANES, 2 * MXU_DIM), F32)

    x = jnp.concatenate(
        [jnp.concatenate([x_ref[k, pl.ds(v, m, stride=STRIDE), :] for k in range(slabs)], axis=1)
         for v in range(STRIDE)], axis=0)
    inv_rms = lax.rsqrt(jnp.mean(x * x, axis=-1, keepdims=True) + EPS)
    h = (x * nw_ref[...]).astype(BF16)

    def pair(ref, c):
        return jnp.concatenate([ref[:, pl.ds(half + c * MXU_DIM, MXU_DIM)] for half in (0, d_ff)], axis=1)

    def up(c):
        r = jnp.dot(h, pair(wup_ref, c), preferred_element_type=F32) * inv_rms
        up_scr[c, pl.ds(0, tm), :] = r
        for k, s in enumerate(moved_start):
            v = STRIDE - 2 + k
            up_scr[c, pl.ds(s + 1, m), :] = r[v * m:(v + 1) * m]

    def gate(c, token):
        cw = pair(cw_ref, c)
        first = [v * m for v in range(STRIDE)]
        s1 = moved_start[1:] + first[:STRIDE - 1]
        s2 = moved_start + first[:STRIDE - 2]
        for v in range(STRIDE):
            for r in range(0, m, GATE_ROWS):
                z = _chained(up_scr[c, pl.ds(first[v] + r, GATE_ROWS), :], token, MXU_DIM)
                y = (cw[0:1, :] * up_scr[c, pl.ds(s2[v] + r, GATE_ROWS), :]
                     + cw[1:2, :] * up_scr[c, pl.ds(s1[v] + r, GATE_ROWS), :] + cw[2:3, :] * z)
                g, u = y[:, :MXU_DIM], y[:, MXU_DIM:]
                a = (g / (1.0 + jnp.exp2(g * -LOG2_E)) * u).astype(BF16)
                a_scr[c, pl.ds(v * m + r, GATE_ROWS), :] = a
                token = _zero_tile(a)
        return token

    token = None
    for c in range(nc):
        up(c)
        if c >= 1:
            token = gate(c - 1, token)
    gate(nc - 1, token)

    split = (nc - 1) * MXU_DIM
    a_head = jnp.concatenate([a_scr[c] for c in range(nc - 1)], axis=1)
    y = x + jnp.dot(a_head, wd_ref[pl.ds(0, split), :], preferred_element_type=F32)
    y = y + jnp.dot(a_scr[nc - 1], wd_ref[pl.ds(split, MXU_DIM), :], preferred_element_type=F32)

    for s in moved_start:
        up_scr[:, pl.ds(s, 1), :] = up_scr[:, pl.ds(s + m, 1), :]

    for k in range(slabs):
        for v in range(STRIDE):
            o_slab[k, pl.ds(v, m, stride=STRIDE), :] = y[v * m:(v + 1) * m, k * LANES:(k + 1) * LANES]
        o_ref[:, pl.ds(k * LANES, LANES)] = o_slab[k]


def _resident(shape):
    return pl.BlockSpec(shape, lambda b, i: (0,) * len(shape), pipeline_mode=pl.Buffered(1))


def _tile(tm, width):
    return pl.BlockSpec((None, tm, width), lambda b, i: (b, i, 0))


def _params(vmem_limit_bytes=VMEM_LIMIT_BYTES):
    return pltpu.CompilerParams(dimension_semantics=("arbitrary", "arbitrary"), vmem_limit_bytes=vmem_limit_bytes)


def _layer(x, attn_norm_w, w_in, q_norm_w, k_norm_w, sinks, conv_mix_w, attn_out_norm_w,
           conv_out_norm_w, w_out, ffn_norm_w, w_up, ffn_conv_w, w_down):
    bsz, seq, d = x.shape
    d_ff = w_down.shape[0]
    tm = TOKEN_TILE
    assert seq % tm == 0 and tm % BLOCK == 0 and d_ff % MXU_DIM == 0
    n_tiles = seq // tm
    assert n_tiles % 2 == 0 and d % (n_tiles * PACKED_ROWS) == 0 and d_ff % (n_tiles // 2 * PACKED_ROWS) == 0
    assert d % (WEIGHT_COPIES * PACKED_ROWS) == 0 and d_ff % (WEIGHT_COPIES * PACKED_ROWS) == 0
    nc = d_ff // MXU_DIM

    scale = HEAD_DIM ** -0.5
    qkw = jnp.concatenate([jnp.tile(q_norm_w, N_Q_HEADS) * scale, jnp.tile(k_norm_w, N_KV_HEADS)])[None, :]
    in_hbm = pl.BlockSpec(memory_space=pl.ANY)


    this_tile = pl.BlockSpec((None, tm, d), lambda b, i: (b, jnp.minimum(i, n_tiles - 1), 0))
    last_tile = pl.BlockSpec((None, tm, d), lambda b, i: (b, jnp.maximum(i - 1, 0), 0))
    last_slab_tile = pl.BlockSpec((None, d // LANES, tm, LANES), lambda b, i: (b, 0, jnp.maximum(i - 1, 0), 0))
    slab_tile = pl.BlockSpec((None, d // LANES, tm, LANES), lambda b, i: (b, 0, i, 0))
    in_w = w_in.shape[1]
    bpt = tm // BLOCK
    score_units = (bpt, 2 * N_KV_HEADS, N_Q_HEADS // N_KV_HEADS // 2)
    x, wup_bf, wd_bf = pl.pallas_call(
        functools.partial(_token_mixer_kernel, n_tiles),
        grid=(bsz, n_tiles + 1),
        in_specs=[pl.BlockSpec(memory_space=pltpu.SMEM), this_tile, last_tile, _resident((1, d)), in_hbm,
                  _resident((1, ATTN_WIDTH + KV_WIDTH)), _resident((CONV_TAPS, CONV_WIDTH)),
                  _resident((1, CONV_WIDTH)), in_hbm, _resident((ATTN_WIDTH, 1)), in_hbm, in_hbm],
        out_specs=[last_slab_tile, in_hbm, in_hbm],
        out_shape=[jax.ShapeDtypeStruct((bsz, d // LANES, seq, LANES), x.dtype),
                   jax.ShapeDtypeStruct(w_up.shape, BF16), jax.ShapeDtypeStruct(w_down.shape, BF16)],
        scratch_shapes=[pltpu.VMEM((tm, in_w), F32), pltpu.VMEM((tm, in_w), F32),
                        pltpu.VMEM((tm + HALO, CONV_WIDTH), F32),
                        pltpu.VMEM((tm, ATTN_WIDTH), BF16),
                        pltpu.VMEM((BLOCK + tm, 2 * N_KV_HEADS * LANES), BF16),
                        pltpu.VMEM((KV_WIDTH, BLOCK + tm), BF16),
                        pltpu.VMEM((tm, CONV_WIDTH), BF16),
                        pltpu.VMEM(score_units + (2 * BLOCK, BLOCK), F32),
                        pltpu.VMEM(score_units + (2 * BLOCK, BLOCK), BF16),
                        pltpu.VMEM((bpt, 2 * N_KV_HEADS, SUBLANES, 2 * BLOCK), F32),
                        pltpu.VMEM((ATTN_WIDTH, tm), F32),
                        pltpu.VMEM((tm, ATTN_WIDTH), BF16),
                        pltpu.VMEM((2, 2 * BLOCK, BLOCK), jnp.int32),
                        pltpu.VMEM((d, in_w), BF16),
                        pltpu.VMEM((ATTN_WIDTH + CONV_WIDTH, d), BF16),
                        pltpu.SemaphoreType.DMA((bpt * 2 * N_KV_HEADS * 2,)),
                        pltpu.VMEM((2, d // n_tiles, 2 * d_ff), F32),
                        pltpu.VMEM((2, d_ff // (n_tiles // 2), d), F32),
                        pltpu.VMEM((d // n_tiles, 2 * d_ff), BF16), pltpu.VMEM((d_ff // (n_tiles // 2), d), BF16),
                        pltpu.SemaphoreType.DMA((3, 2))],
        compiler_params=_params(),
        name="token_mixer",
    )(sinks, x, x, attn_norm_w[None, :], w_in, qkw, conv_mix_w, conv_out_norm_w[None, :], w_out,
      attn_out_norm_w[:, None], w_up, w_down)

    x = pl.pallas_call(
        _ffn_kernel,
        grid=(bsz, n_tiles),
        in_specs=[slab_tile, _resident((1, d)), in_hbm, _resident((CONV_TAPS, 2 * d_ff)), in_hbm],
        out_specs=_tile(tm, d),
        out_shape=jax.ShapeDtypeStruct((bsz, seq, d), x.dtype),
        scratch_shapes=[pltpu.VMEM((d // LANES, tm, LANES), F32),
                        pltpu.VMEM((nc, tm + 2 * (tm // STRIDE + HALO), 2 * MXU_DIM), F32),
                        pltpu.VMEM((nc, tm, MXU_DIM), BF16),
                        pltpu.VMEM((d, 2 * d_ff), BF16), pltpu.VMEM((d_ff, d), BF16),
                        pltpu.SemaphoreType.DMA((2, WEIGHT_COPIES))],
        compiler_params=_params(),
        name="conv_ffn",
    )(x, ffn_norm_w[None, :], wup_bf, ffn_conv_w, wd_bf)
    return x


def kernel(x, attn_norm_w, w_in, q_norm_w, k_norm_w, sinks, conv_mix_w, attn_out_norm_w,
           conv_out_norm_w, w_out, ffn_norm_w, w_up, ffn_conv_w, w_down):
    for l in range(attn_norm_w.shape[0]):
        x = _layer(x, attn_norm_w[l], w_in[l], q_norm_w[l], k_norm_w[l], sinks[l], conv_mix_w[l],
                   attn_out_norm_w[l], conv_out_norm_w[l], w_out[l], ffn_norm_w[l], w_up[l],
                   ffn_conv_w[l], w_down[l])
    return x
```

```python
import jax
import jax.numpy as jnp
from jax import lax
from jax.experimental import pallas as pl
from jax.experimental.pallas import tpu as pltpu

HEAD_DIM = 64
N_Q_HEADS = 8
N_KV_HEADS = 2
WINDOW = 128
BLOCK = 128
ATTN_WIDTH = N_Q_HEADS * HEAD_DIM
KV_WIDTH = N_KV_HEADS * HEAD_DIM
CONV_WIDTH = 512
CONV_TAPS = 3
EPS = 1e-6
NEG_INF = -1e30
LOG2_E = 1.4426950408889634

LANES = 128
SUBLANES = 8
MXU_DIM = 256
VMEM_LIMIT_BYTES = 56 * 1024 * 1024

TOKEN_TILE = 512
HALO = SUBLANES
STRIDE = 4
GATE_ROWS = 64
QKV_ROWS = 64
PIECE_ROWS = 32
PACKED_ROWS = 2 * SUBLANES

F32 = jnp.float32
BF16 = jnp.bfloat16


def _rms(x, w):
    ms = jnp.mean(x * x, axis=-1, keepdims=True)
    return x * lax.rsqrt(ms + EPS) * w


def _zero_tile(packed):
    words = pltpu.bitcast(packed, jnp.uint32)
    while words.shape[1] > LANES:
        half = words.shape[1] // 2
        words = words[:, :half] | words[:, half:]
    while words.shape[0] > SUBLANES:
        half = words.shape[0] // 2
        words = words[:half] | words[half:]
    return ((words >> 16) >> 16).astype(F32)


def _chained(x, zero_tile, width=LANES):
    if zero_tile is None:
        return x
    zeros = jnp.concatenate([jnp.concatenate([zero_tile] * (x.shape[0] // SUBLANES), axis=0)] * (width // LANES),
                            axis=1)
    if width == x.shape[1]:
        return x + zeros
    return jnp.concatenate([x[:, :width] + zeros, x[:, width:]], axis=1)


def _fetch_bf16(w_hbm, w_bf16, slots, sems):
    rows, cols = slots[0].shape
    blocks = [(r, c) for r in range(0, w_hbm.shape[0], rows) for c in range(0, w_hbm.shape[1], cols)]
    assert w_hbm.shape[0] % rows == 0 and w_hbm.shape[1] % cols == 0

    def copy(b):
        r, c = blocks[b]
        s = b % len(slots)
        return pltpu.make_async_copy(w_hbm.at[pl.ds(r, rows), pl.ds(c, cols)], slots[s], sems.at[s])

    for b in range(min(len(slots), len(blocks))):
        copy(b).start()
    for b, (r, c) in enumerate(blocks):
        copy(b).wait()
        w_bf16[pl.ds(r, rows), pl.ds(c, cols)] = slots[b % len(slots)][...].astype(BF16)
        if b + len(slots) < len(blocks):
            copy(b + len(slots)).start()


def _first_step():
    return (pl.program_id(0) == 0) & (pl.program_id(1) == 0)


def _token_mixer_kernel(sinks_ref, x_ref, xl_ref, nw_ref, win_hbm, qkw_ref, cw_ref, cnw_ref, wo_hbm, anw_ref,
                        o_ref, proj_a, proj_b, cu_scr, q_scr, kv_scr, vt_scr, cn_scr,
                        s_scr, p_scr, inv_scr, attn_t, an_scr, mask_scr, win_ref, wo_ref, w_sems):
    tm = x_ref.shape[0]
    qkv_w = ATTN_WIDTH + 2 * KV_WIDTH
    step = pl.program_id(1)

    @pl.when(_first_step())
    def _():
        _fetch_bf16(win_hbm, win_ref, [proj_a, proj_b], w_sems)
        slots = [s_scr.at[j, ce, g] for j in range(s_scr.shape[0]) for ce in range(s_scr.shape[1])
                 for g in range(s_scr.shape[2])]
        _fetch_bf16(wo_hbm, wo_ref, slots, w_sems)

    @pl.when(step == 0)
    def _():
        proj_b[...] = jnp.zeros(proj_b.shape, F32)
        cu_scr[pl.ds(0, HALO), :] = jnp.zeros((HALO, CONV_WIDTH), F32)
        kv_scr[pl.ds(0, BLOCK), :] = jnp.zeros((BLOCK, kv_scr.shape[1]), BF16)
        vt_scr[:, pl.ds(0, BLOCK)] = jnp.zeros((KV_WIDTH, BLOCK), BF16)

    def body(fill, drain):
        h = _rms(x_ref[...], nw_ref[...]).astype(BF16)
        fill[...] = jnp.dot(h, win_ref[...], preferred_element_type=F32)

        low = lax.broadcasted_iota(jnp.int32, (QKV_ROWS, LANES), 1) < HEAD_DIM
        for r in range(0, tm, QKV_ROWS):
            rows = pl.ds(r, QKV_ROWS)
            qk = drain[rows, pl.ds(0, ATTN_WIDTH + KV_WIDTH)]
            sums = []
            for t in range(0, ATTN_WIDTH + KV_WIDTH, LANES):
                sq = qk[:, t:t + LANES] * qk[:, t:t + LANES]
                sums.append(jnp.where(low, jnp.sum(jnp.where(low, sq, 0.0), axis=-1, keepdims=True),
                                      jnp.sum(jnp.where(low, 0.0, sq), axis=-1, keepdims=True)))
            qkn = qk * lax.rsqrt(jnp.concatenate(sums, axis=1) * (1.0 / HEAD_DIM) + EPS) * qkw_ref[...]
            qn = qkn[:, :ATTN_WIDTH].astype(BF16)
            q_scr[rows, :] = qn
            k = qkn[:, ATTN_WIDTH:]
            swapped = pltpu.roll(k, HEAD_DIM, axis=1)
            kk = jnp.concatenate([jnp.where(low, k, 0.0), jnp.where(low, 0.0, swapped),
                                  jnp.where(low, swapped, 0.0), jnp.where(low, 0.0, k)], axis=1).astype(BF16)
            kv_scr[pl.ds(BLOCK + r, QKV_ROWS), :] = kk
        vt_scr[:, pl.ds(BLOCK, tm)] = drain[:, pl.ds(ATTN_WIDTH + KV_WIDTH, KV_WIDTH)].T.astype(BF16)

        cw = cw_ref[...]
        token = None
        for r in range(0, tm, PIECE_ROWS):
            rows = pl.ds(r, PIECE_ROWS)
            bg = _chained(drain[rows, pl.ds(qkv_w, CONV_WIDTH)], token)
            cu_scr[pl.ds(HALO + r, PIECE_ROWS), :] = (drain[rows, pl.ds(qkv_w + CONV_WIDTH, CONV_WIDTH)]
                                                      * drain[rows, pl.ds(qkv_w + 2 * CONV_WIDTH, CONV_WIDTH)])
            y = cw[0:1, :] * cu_scr[pl.ds(HALO + r - 2, PIECE_ROWS), :]
            y = y + cw[1:2, :] * cu_scr[pl.ds(HALO + r - 1, PIECE_ROWS), :]
            y = y + cw[2:3, :] * cu_scr[pl.ds(HALO + r, PIECE_ROWS), :]
            cn = _rms(bg * y, cnw_ref[...]).astype(BF16)
            cn_scr[rows, :] = cn
            token = _zero_tile(cn)
        cu_scr[pl.ds(0, HALO), :] = cu_scr[pl.ds(tm, HALO), :]

        _attend(step > 1, sinks_ref, q_scr, kv_scr, vt_scr, s_scr, p_scr, inv_scr, attn_t, an_scr, mask_scr,
                anw_ref)
        y = jnp.dot(an_scr[...], wo_ref[pl.ds(0, ATTN_WIDTH), :], preferred_element_type=F32)
        y = y + jnp.dot(cn_scr[...], wo_ref[pl.ds(ATTN_WIDTH, CONV_WIDTH), :], preferred_element_type=F32)
        y = xl_ref[...] + y
        for k in range(o_ref.shape[0]):
            o_ref[k] = y[:, k * LANES:(k + 1) * LANES]
        kv_scr[pl.ds(0, BLOCK), :] = kv_scr[pl.ds(tm, BLOCK), :]
        vt_scr[:, pl.ds(0, BLOCK)] = vt_scr[:, pl.ds(tm, BLOCK)]

    @pl.when(step % 2 == 0)
    def _():
        body(proj_a, proj_b)

    @pl.when(step % 2 == 1)
    def _():
        body(proj_b, proj_a)


def _all_sublanes(x, op):
    for shift in (4, 2, 1):
        x = op(x, pltpu.roll(x, shift, axis=0))
    return x


def _attend(has_prev, sinks_ref, q_ref, kv_ref, vt_ref, s_scr, p_scr, inv_scr, attn_t, an_scr, mask_scr, anw_ref):
    tq = q_ref.shape[0]
    nb = tq // BLOCK
    group = N_Q_HEADS // N_KV_HEADS
    group_pairs = group // 2
    key_tiles = 2 * BLOCK // SUBLANES
    nt = (((1,), (1,)), ((), ()))

    sj = lax.broadcasted_iota(jnp.int32, (2 * BLOCK, BLOCK), 0)
    qi = lax.broadcasted_iota(jnp.int32, (2 * BLOCK, BLOCK), 1)
    rel = qi + BLOCK - sj
    band = (rel >= 0) & (rel < WINDOW)
    band0 = band & ((sj >= BLOCK) | has_prev)
    mask_scr[0] = band0.astype(jnp.int32)
    mask_scr[1] = band.astype(jnp.int32)

    def scores(j):
        rows = pl.ds(j * BLOCK, BLOCK)
        for c in range(N_KV_HEADS):
            qs = jnp.concatenate(
                [q_ref[rows, pl.ds((group_pairs * c + g) * LANES, LANES)] for g in range(group_pairs)], axis=0)
            for e in range(2):
                k = kv_ref[pl.ds(j * BLOCK, 2 * BLOCK), pl.ds((2 * c + e) * LANES, LANES)]
                s = lax.dot_general(k, qs, nt, preferred_element_type=F32)
                for g in range(group_pairs):
                    s_scr[j, 2 * c + e, g] = s[:, g * BLOCK:(g + 1) * BLOCK]

    def softmax(j):
        for ce, g in [(ce, g) for ce in range(2 * N_KV_HEADS) for g in range(group_pairs)]:
            c, e = divmod(ce, 2)
            cols = pl.ds(g * BLOCK, BLOCK)
            sink = sinks_ref[group * c + 2 * g + e]
            m = None
            for r in range(key_tiles):
                rows = pl.ds(r * SUBLANES, SUBLANES)
                s = jnp.where(mask_scr[min(j, 1), rows, :] > 0, s_scr[j, ce, g, rows, :], NEG_INF)
                s_scr[j, ce, g, rows, :] = s
                m = s if m is None else jnp.maximum(m, s)
            m = jnp.maximum(_all_sublanes(m, jnp.maximum), sink)
            m2 = jnp.concatenate([m, m], axis=0)
            total = None
            for r in range(0, 2 * BLOCK, PACKED_ROWS):
                rows = pl.ds(r, PACKED_ROWS)
                pe = jnp.exp(s_scr[j, ce, g, rows, :] - m2)
                p_scr[j, ce, g, rows, :] = pe.astype(BF16)
                part = pe[:SUBLANES] + pe[SUBLANES:]
                total = part if total is None else total + part
            denom = _all_sublanes(total, jnp.add) + jnp.exp(sink - m)
            inv_scr[j, ce, :, cols] = 1.0 / denom

    def values(j):
        for c in range(N_KV_HEADS):
            vt = vt_ref[pl.ds(c * HEAD_DIM, HEAD_DIM), pl.ds(j * BLOCK, 2 * BLOCK)]
            for e in range(2):
                inv = jnp.concatenate([inv_scr[j, 2 * c + e]] * (HEAD_DIM // SUBLANES), axis=0)
                p = jnp.concatenate([p_scr[j, 2 * c + e, g] for g in range(group_pairs)], axis=1)
                o = jnp.dot(vt, p, preferred_element_type=F32) * inv
                for g in range(group_pairs):
                    attn_t[pl.ds((group * c + 2 * g + e) * HEAD_DIM, HEAD_DIM), pl.ds(j * BLOCK, BLOCK)] = (
                        o[:, g * BLOCK:(g + 1) * BLOCK])
        a = attn_t[:, pl.ds(j * BLOCK, BLOCK)]
        ms = jnp.mean(a * a, axis=0, keepdims=True)
        an_scr[pl.ds(j * BLOCK, BLOCK), :] = ((a * lax.rsqrt(ms + EPS)).T * anw_ref[...]).astype(BF16)

    for t in range(nb + 2):
        if t < nb:
            scores(t)
        if 0 <= t - 1 < nb:
            softmax(t - 1)
        if 0 <= t - 2 < nb:
            values(t - 2)


def _ffn_kernel(x_ref, nw_ref, wup_hbm, cw_ref, wd_hbm, o_ref, o_slab, up_scr, a_scr,
                wup_ref, wd_ref, w_sems):
    slabs, tm, _ = x_ref.shape
    m = tm // STRIDE
    d_ff = wd_ref.shape[0]
    nc = d_ff // MXU_DIM
    moved_start = [STRIDE * m + k * (m + HALO) for k in range(2)]

    @pl.when(_first_step())
    def _():
        _fetch_bf16(wup_hbm, wup_ref, [up_scr.at[c, pl.ds(0, tm), :] for c in range(nc)], w_sems)
        _fetch_bf16(wd_hbm, wd_ref, [up_scr.at[c, pl.ds(0, MXU_DIM), :] for c in range(nc)], w_sems)

    @pl.when(pl.program_id(1) == 0)
    def _():
        for s in moved_start:
            up_scr[:, pl.ds(s, SUBLANES), :] = jnp.zeros((nc, SUBLANES, 2 * MXU_DIM), F32)

    x = jnp.concatenate(
        [jnp.concatenate([x_ref[k, pl.ds(v, m, stride=STRIDE), :] for k in range(slabs)], axis=1)
         for v in range(STRIDE)], axis=0)
    inv_rms = lax.rsqrt(jnp.mean(x * x, axis=-1, keepdims=True) + EPS)
    h = (x * nw_ref[...]).astype(BF16)

    def pair(ref, c):
        return jnp.concatenate([ref[:, pl.ds(half + c * MXU_DIM, MXU_DIM)] for half in (0, d_ff)], axis=1)

    def up(c):
        r = jnp.dot(h, pair(wup_ref, c), preferred_element_type=F32) * inv_rms
        up_scr[c, pl.ds(0, tm), :] = r
        for k, s in enumerate(moved_start):
            v = STRIDE - 2 + k
            up_scr[c, pl.ds(s + 1, m), :] = r[v * m:(v + 1) * m]

    def gate(c, token):
        cw = pair(cw_ref, c)
        first = [v * m for v in range(STRIDE)]
        s1 = moved_start[1:] + first[:STRIDE - 1]
        s2 = moved_start + first[:STRIDE - 2]
        for v in range(STRIDE):
            for r in range(0, m, GATE_ROWS):
                z = _chained(up_scr[c, pl.ds(first[v] + r, GATE_ROWS), :], token, MXU_DIM)
                y = (cw[0:1, :] * up_scr[c, pl.ds(s2[v] + r, GATE_ROWS), :]
                     + cw[1:2, :] * up_scr[c, pl.ds(s1[v] + r, GATE_ROWS), :] + cw[2:3, :] * z)
                g, u = y[:, :MXU_DIM], y[:, MXU_DIM:]
                a = (g / (1.0 + jnp.exp2(g * -LOG2_E)) * u).astype(BF16)
                a_scr[c, pl.ds(v * m + r, GATE_ROWS), :] = a
                token = _zero_tile(a)
        return token

    token = None
    for c in range(nc):
        up(c)
        if c >= 1:
            token = gate(c - 1, token)
    gate(nc - 1, token)

    split = (nc - 1) * MXU_DIM
    a_head = jnp.concatenate([a_scr[c] for c in range(nc - 1)], axis=1)
    y = x + jnp.dot(a_head, wd_ref[pl.ds(0, split), :], preferred_element_type=F32)
    y = y + jnp.dot(a_scr[nc - 1], wd_ref[pl.ds(split, MXU_DIM), :], preferred_element_type=F32)

    for s in moved_start:
        up_scr[:, pl.ds(s, 1), :] = up_scr[:, pl.ds(s + m, 1), :]

    for k in range(slabs):
        for v in range(STRIDE):
            o_slab[k, pl.ds(v, m, stride=STRIDE), :] = y[v * m:(v + 1) * m, k * LANES:(k + 1) * LANES]
        o_ref[:, pl.ds(k * LANES, LANES)] = o_slab[k]


def _resident(shape):
    return pl.BlockSpec(shape, lambda b, i: (0,) * len(shape), pipeline_mode=pl.Buffered(1))


def _tile(tm, width):
    return pl.BlockSpec((None, tm, width), lambda b, i: (b, i, 0))


def _params(vmem_limit_bytes=VMEM_LIMIT_BYTES):
    return pltpu.CompilerParams(dimension_semantics=("arbitrary", "arbitrary"), vmem_limit_bytes=vmem_limit_bytes)


def _layer(x, attn_norm_w, w_in, q_norm_w, k_norm_w, sinks, conv_mix_w, attn_out_norm_w,
           conv_out_norm_w, w_out, ffn_norm_w, w_up, ffn_conv_w, w_down):
    bsz, seq, d = x.shape
    d_ff = w_down.shape[0]
    tm = TOKEN_TILE
    assert seq % tm == 0 and tm % BLOCK == 0 and d_ff % MXU_DIM == 0
    n_tiles = seq // tm
    nc = d_ff // MXU_DIM

    scale = HEAD_DIM ** -0.5
    qkw = jnp.concatenate([jnp.tile(q_norm_w, N_Q_HEADS) * scale, jnp.tile(k_norm_w, N_KV_HEADS)])[None, :]
    in_hbm = pl.BlockSpec(memory_space=pl.ANY)


    this_tile = pl.BlockSpec((None, tm, d), lambda b, i: (b, jnp.minimum(i, n_tiles - 1), 0))
    last_tile = pl.BlockSpec((None, tm, d), lambda b, i: (b, jnp.maximum(i - 1, 0), 0))
    last_slab_tile = pl.BlockSpec((None, d // LANES, tm, LANES), lambda b, i: (b, 0, jnp.maximum(i - 1, 0), 0))
    slab_tile = pl.BlockSpec((None, d // LANES, tm, LANES), lambda b, i: (b, 0, i, 0))
    in_w = w_in.shape[1]
    bpt = tm // BLOCK
    score_units = (bpt, 2 * N_KV_HEADS, N_Q_HEADS // N_KV_HEADS // 2)
    x = pl.pallas_call(
        _token_mixer_kernel,
        grid=(bsz, n_tiles + 1),
        in_specs=[pl.BlockSpec(memory_space=pltpu.SMEM), this_tile, last_tile, _resident((1, d)), in_hbm,
                  _resident((1, ATTN_WIDTH + KV_WIDTH)), _resident((CONV_TAPS, CONV_WIDTH)),
                  _resident((1, CONV_WIDTH)), in_hbm, _resident((1, ATTN_WIDTH))],
        out_specs=last_slab_tile,
        out_shape=jax.ShapeDtypeStruct((bsz, d // LANES, seq, LANES), x.dtype),
        scratch_shapes=[pltpu.VMEM((tm, in_w), F32), pltpu.VMEM((tm, in_w), F32),
                        pltpu.VMEM((tm + HALO, CONV_WIDTH), F32),
                        pltpu.VMEM((tm, ATTN_WIDTH), BF16),
                        pltpu.VMEM((BLOCK + tm, 2 * N_KV_HEADS * LANES), BF16),
                        pltpu.VMEM((KV_WIDTH, BLOCK + tm), BF16),
                        pltpu.VMEM((tm, CONV_WIDTH), BF16),
                        pltpu.VMEM(score_units + (2 * BLOCK, BLOCK), F32),
                        pltpu.VMEM(score_units + (2 * BLOCK, BLOCK), BF16),
                        pltpu.VMEM((bpt, 2 * N_KV_HEADS, SUBLANES, 2 * BLOCK), F32),
                        pltpu.VMEM((ATTN_WIDTH, tm), F32),
                        pltpu.VMEM((tm, ATTN_WIDTH), BF16),
                        pltpu.VMEM((2, 2 * BLOCK, BLOCK), jnp.int32),
                        pltpu.VMEM((d, in_w), BF16),
                        pltpu.VMEM((ATTN_WIDTH + CONV_WIDTH, d), BF16),
                        pltpu.SemaphoreType.DMA((bpt * 2 * N_KV_HEADS * 2,))],
        compiler_params=_params(),
        name="token_mixer",
    )(sinks, x, x, attn_norm_w[None, :], w_in, qkw, conv_mix_w, conv_out_norm_w[None, :], w_out,
      attn_out_norm_w[None, :])

    x = pl.pallas_call(
        _ffn_kernel,
        grid=(bsz, n_tiles),
        in_specs=[slab_tile, _resident((1, d)), in_hbm, _resident((CONV_TAPS, 2 * d_ff)), in_hbm],
        out_specs=_tile(tm, d),
        out_shape=jax.ShapeDtypeStruct((bsz, seq, d), x.dtype),
        scratch_shapes=[pltpu.VMEM((d // LANES, tm, LANES), F32),
                        pltpu.VMEM((nc, tm + 2 * (tm // STRIDE + HALO), 2 * MXU_DIM), F32),
                        pltpu.VMEM((nc, tm, MXU_DIM), BF16),
                        pltpu.VMEM((d, 2 * d_ff), BF16), pltpu.VMEM((d_ff, d), BF16),
                        pltpu.SemaphoreType.DMA((nc,))],
        compiler_params=_params(),
        name="conv_ffn",
    )(x, ffn_norm_w[None, :], w_up, ffn_conv_w, w_down)
    return x


def kernel(x, attn_norm_w, w_in, q_norm_w, k_norm_w, sinks, conv_mix_w, attn_out_norm_w,
           conv_out_norm_w, w_out, ffn_norm_w, w_up, ffn_conv_w, w_down):
    for l in range(attn_norm_w.shape[0]):
        x = _layer(x, attn_norm_w[l], w_in[l], q_norm_w[l], k_norm_w[l], sinks[l], conv_mix_w[l],
                   attn_out_norm_w[l], conv_out_norm_w[l], w_out[l], ffn_norm_w[l], w_up[l],
                   ffn_conv_w[l], w_down[l])
    return x
```

```python
import jax
import jax.numpy as jnp
from jax import lax
from jax.experimental import pallas as pl
from jax.experimental.pallas import tpu as pltpu

HEAD_DIM = 64
N_Q_HEADS = 8
N_KV_HEADS = 2
WINDOW = 128
BLOCK = 128
ATTN_WIDTH = N_Q_HEADS * HEAD_DIM
KV_WIDTH = N_KV_HEADS * HEAD_DIM
CONV_WIDTH = 512
CONV_TAPS = 3
EPS = 1e-6
NEG_INF = -1e30
LOG2_E = 1.4426950408889634

LANES = 128
SUBLANES = 8
MXU_DIM = 256
VMEM_LIMIT_BYTES = 56 * 1024 * 1024

TOKEN_TILE = 512
HALO = SUBLANES
STRIDE = 4
GATE_ROWS = 64
QKV_ROWS = 64
PIECE_ROWS = 32
PACKED_ROWS = 2 * SUBLANES

F32 = jnp.float32
BF16 = jnp.bfloat16


def _rms(x, w):
    ms = jnp.mean(x * x, axis=-1, keepdims=True)
    return x * lax.rsqrt(ms + EPS) * w


def _zero_tile(packed):
    words = pltpu.bitcast(packed, jnp.uint32)
    while words.shape[1] > LANES:
        half = words.shape[1] // 2
        words = words[:, :half] | words[:, half:]
    while words.shape[0] > SUBLANES:
        half = words.shape[0] // 2
        words = words[:half] | words[half:]
    return ((words >> 16) >> 16).astype(F32)


def _chained(x, zero_tile, width=LANES):
    if zero_tile is None:
        return x
    zeros = jnp.concatenate([jnp.concatenate([zero_tile] * (x.shape[0] // SUBLANES), axis=0)] * (width // LANES),
                            axis=1)
    if width == x.shape[1]:
        return x + zeros
    return jnp.concatenate([x[:, :width] + zeros, x[:, width:]], axis=1)


def _fetch_bf16(w_hbm, w_bf16, slots, sems):
    rows, cols = slots[0].shape
    blocks = [(r, c) for r in range(0, w_hbm.shape[0], rows) for c in range(0, w_hbm.shape[1], cols)]
    assert w_hbm.shape[0] % rows == 0 and w_hbm.shape[1] % cols == 0

    def copy(b):
        r, c = blocks[b]
        s = b % len(slots)
        return pltpu.make_async_copy(w_hbm.at[pl.ds(r, rows), pl.ds(c, cols)], slots[s], sems.at[s])

    for b in range(min(len(slots), len(blocks))):
        copy(b).start()
    for b, (r, c) in enumerate(blocks):
        copy(b).wait()
        w_bf16[pl.ds(r, rows), pl.ds(c, cols)] = slots[b % len(slots)][...].astype(BF16)
        if b + len(slots) < len(blocks):
            copy(b + len(slots)).start()


def _first_step():
    return (pl.program_id(0) == 0) & (pl.program_id(1) == 0)


def _token_mixer_kernel(sinks_ref, x_ref, xl_ref, nw_ref, win_hbm, qkw_ref, cw_ref, cnw_ref, wo_hbm, anw_ref,
                        o_ref, proj_a, proj_b, cu_scr, q_scr, kv_scr, vt_scr, cn_scr,
                        s_scr, p_scr, inv_scr, attn_t, an_scr, mask_scr, win_ref, wo_ref, w_sems):
    tm = x_ref.shape[0]
    qkv_w = ATTN_WIDTH + 2 * KV_WIDTH
    step = pl.program_id(1)

    @pl.when(_first_step())
    def _():
        _fetch_bf16(win_hbm, win_ref, [proj_a, proj_b], w_sems)
        slots = [s_scr.at[j, ce, g] for j in range(s_scr.shape[0]) for ce in range(s_scr.shape[1])
                 for g in range(s_scr.shape[2])]
        _fetch_bf16(wo_hbm, wo_ref, slots, w_sems)

    @pl.when(step == 0)
    def _():
        cu_scr[pl.ds(0, HALO), :] = jnp.zeros((HALO, CONV_WIDTH), F32)
        kv_scr[pl.ds(0, BLOCK), :] = jnp.zeros((BLOCK, kv_scr.shape[1]), BF16)
        vt_scr[:, pl.ds(0, BLOCK)] = jnp.zeros((KV_WIDTH, BLOCK), BF16)

    def body(fill, drain, project=True, finish=True):
        if project:
            h = _rms(x_ref[...], nw_ref[...]).astype(BF16)
            fill[...] = jnp.dot(h, win_ref[...], preferred_element_type=F32)
        if not finish:
            return

        low = lax.broadcasted_iota(jnp.int32, (QKV_ROWS, LANES), 1) < HEAD_DIM
        for r in range(0, tm, QKV_ROWS):
            rows = pl.ds(r, QKV_ROWS)
            qk = drain[rows, pl.ds(0, ATTN_WIDTH + KV_WIDTH)]
            sums = []
            for t in range(0, ATTN_WIDTH + KV_WIDTH, LANES):
                sq = qk[:, t:t + LANES] * qk[:, t:t + LANES]
                sums.append(jnp.where(low, jnp.sum(jnp.where(low, sq, 0.0), axis=-1, keepdims=True),
                                      jnp.sum(jnp.where(low, 0.0, sq), axis=-1, keepdims=True)))
            qkn = qk * lax.rsqrt(jnp.concatenate(sums, axis=1) * (1.0 / HEAD_DIM) + EPS) * qkw_ref[...]
            qn = qkn[:, :ATTN_WIDTH].astype(BF16)
            q_scr[rows, :] = qn
            k = qkn[:, ATTN_WIDTH:]
            swapped = pltpu.roll(k, HEAD_DIM, axis=1)
            kk = jnp.concatenate([jnp.where(low, k, 0.0), jnp.where(low, 0.0, swapped),
                                  jnp.where(low, swapped, 0.0), jnp.where(low, 0.0, k)], axis=1).astype(BF16)
            kv_scr[pl.ds(BLOCK + r, QKV_ROWS), :] = kk
        vt_scr[:, pl.ds(BLOCK, tm)] = drain[:, pl.ds(ATTN_WIDTH + KV_WIDTH, KV_WIDTH)].T.astype(BF16)

        cw = cw_ref[...]
        token = None
        for r in range(0, tm, PIECE_ROWS):
            rows = pl.ds(r, PIECE_ROWS)
            bg = _chained(drain[rows, pl.ds(qkv_w, CONV_WIDTH)], token)
            cu_scr[pl.ds(HALO + r, PIECE_ROWS), :] = (drain[rows, pl.ds(qkv_w + CONV_WIDTH, CONV_WIDTH)]
                                                      * drain[rows, pl.ds(qkv_w + 2 * CONV_WIDTH, CONV_WIDTH)])
            y = cw[0:1, :] * cu_scr[pl.ds(HALO + r - 2, PIECE_ROWS), :]
            y = y + cw[1:2, :] * cu_scr[pl.ds(HALO + r - 1, PIECE_ROWS), :]
            y = y + cw[2:3, :] * cu_scr[pl.ds(HALO + r, PIECE_ROWS), :]
            cn = _rms(bg * y, cnw_ref[...]).astype(BF16)
            cn_scr[rows, :] = cn
            token = _zero_tile(cn)
        cu_scr[pl.ds(0, HALO), :] = cu_scr[pl.ds(tm, HALO), :]

        _attend(step > 1, sinks_ref, q_scr, kv_scr, vt_scr, s_scr, p_scr, inv_scr, attn_t, an_scr, mask_scr,
                anw_ref)
        y = jnp.dot(an_scr[...], wo_ref[pl.ds(0, ATTN_WIDTH), :], preferred_element_type=F32)
        y = y + jnp.dot(cn_scr[...], wo_ref[pl.ds(ATTN_WIDTH, CONV_WIDTH), :], preferred_element_type=F32)
        y = xl_ref[...] + y
        for k in range(o_ref.shape[0]):
            o_ref[k] = y[:, k * LANES:(k + 1) * LANES]
        kv_scr[pl.ds(0, BLOCK), :] = kv_scr[pl.ds(tm, BLOCK), :]
        vt_scr[:, pl.ds(0, BLOCK)] = vt_scr[:, pl.ds(tm, BLOCK)]

    last = pl.num_programs(1) - 1

    @pl.when(step == 0)
    def _():
        body(proj_a, proj_b, finish=False)

    @pl.when((step % 2 == 0) & (step > 0) & (step < last))
    def _():
        body(proj_a, proj_b)

    @pl.when((step % 2 == 1) & (step < last))
    def _():
        body(proj_b, proj_a)

    @pl.when((step % 2 == 0) & (step == last))
    def _():
        body(proj_a, proj_b, project=False)

    @pl.when((step % 2 == 1) & (step == last))
    def _():
        body(proj_b, proj_a, project=False)


def _all_sublanes(x, op):
    for shift in (4, 2, 1):
        x = op(x, pltpu.roll(x, shift, axis=0))
    return x


def _attend(has_prev, sinks_ref, q_ref, kv_ref, vt_ref, s_scr, p_scr, inv_scr, attn_t, an_scr, mask_scr, anw_ref):
    tq = q_ref.shape[0]
    nb = tq // BLOCK
    group = N_Q_HEADS // N_KV_HEADS
    group_pairs = group // 2
    key_tiles = 2 * BLOCK // SUBLANES
    nt = (((1,), (1,)), ((), ()))

    sj = lax.broadcasted_iota(jnp.int32, (2 * BLOCK, BLOCK), 0)
    qi = lax.broadcasted_iota(jnp.int32, (2 * BLOCK, BLOCK), 1)
    rel = qi + BLOCK - sj
    band = (rel >= 0) & (rel < WINDOW)
    band0 = band & ((sj >= BLOCK) | has_prev)
    mask_scr[0] = band0.astype(jnp.int32)
    mask_scr[1] = band.astype(jnp.int32)

    def scores(j):
        rows = pl.ds(j * BLOCK, BLOCK)
        for c in range(N_KV_HEADS):
            qs = jnp.concatenate(
                [q_ref[rows, pl.ds((group_pairs * c + g) * LANES, LANES)] for g in range(group_pairs)], axis=0)
            for e in range(2):
                k = kv_ref[pl.ds(j * BLOCK, 2 * BLOCK), pl.ds((2 * c + e) * LANES, LANES)]
                s = lax.dot_general(k, qs, nt, preferred_element_type=F32)
                for g in range(group_pairs):
                    s_scr[j, 2 * c + e, g] = s[:, g * BLOCK:(g + 1) * BLOCK]

    def softmax(j):
        for ce, g in [(ce, g) for ce in range(2 * N_KV_HEADS) for g in range(group_pairs)]:
            c, e = divmod(ce, 2)
            cols = pl.ds(g * BLOCK, BLOCK)
            sink = sinks_ref[group * c + 2 * g + e]
            m = None
            for r in range(key_tiles):
                rows = pl.ds(r * SUBLANES, SUBLANES)
                s = jnp.where(mask_scr[min(j, 1), rows, :] > 0, s_scr[j, ce, g, rows, :], NEG_INF)
                s_scr[j, ce, g, rows, :] = s
                m = s if m is None else jnp.maximum(m, s)
            m = jnp.maximum(_all_sublanes(m, jnp.maximum), sink)
            m2 = jnp.concatenate([m, m], axis=0)
            total = None
            for r in range(0, 2 * BLOCK, PACKED_ROWS):
                rows = pl.ds(r, PACKED_ROWS)
                pe = jnp.exp(s_scr[j, ce, g, rows, :] - m2)
                p_scr[j, ce, g, rows, :] = pe.astype(BF16)
                part = pe[:SUBLANES] + pe[SUBLANES:]
                total = part if total is None else total + part
            denom = _all_sublanes(total, jnp.add) + jnp.exp(sink - m)
            inv_scr[j, ce, :, cols] = 1.0 / denom

    def values(j):
        for c in range(N_KV_HEADS):
            vt = vt_ref[pl.ds(c * HEAD_DIM, HEAD_DIM), pl.ds(j * BLOCK, 2 * BLOCK)]
            for e in range(2):
                inv = jnp.concatenate([inv_scr[j, 2 * c + e]] * (HEAD_DIM // SUBLANES), axis=0)
                p = jnp.concatenate([p_scr[j, 2 * c + e, g] for g in range(group_pairs)], axis=1)
                o = jnp.dot(vt, p, preferred_element_type=F32) * inv
                for g in range(group_pairs):
                    attn_t[pl.ds((group * c + 2 * g + e) * HEAD_DIM, HEAD_DIM), pl.ds(j * BLOCK, BLOCK)] = (
                        o[:, g * BLOCK:(g + 1) * BLOCK])
        a = attn_t[:, pl.ds(j * BLOCK, BLOCK)]
        ms = jnp.mean(a * a, axis=0, keepdims=True)
        an_scr[pl.ds(j * BLOCK, BLOCK), :] = ((a * lax.rsqrt(ms + EPS)).T * anw_ref[...]).astype(BF16)

    for t in range(nb + 2):
        if t < nb:
            scores(t)
        if 0 <= t - 1 < nb:
            softmax(t - 1)
        if 0 <= t - 2 < nb:
            values(t - 2)


def _ffn_kernel(x_ref, nw_ref, wup_hbm, cw_ref, wd_hbm, o_ref, o_slab, up_scr, a_scr,
                wup_ref, wd_ref, w_sems):
    slabs, tm, _ = x_ref.shape
    m = tm // STRIDE
    d_ff = wd_ref.shape[0]
    nc = d_ff // MXU_DIM
    moved_start = [STRIDE * m + k * (m + HALO) for k in range(2)]

    @pl.when(_first_step())
    def _():
        _fetch_bf16(wup_hbm, wup_ref, [up_scr.at[c, pl.ds(0, tm), :] for c in range(nc)], w_sems)
        _fetch_bf16(wd_hbm, wd_ref, [up_scr.at[c, pl.ds(0, MXU_DIM), :] for c in range(nc)], w_sems)

    @pl.when(pl.program_id(1) == 0)
    def _():
        for s in moved_start:
            up_scr[:, pl.ds(s, SUBLANES), :] = jnp.zeros((nc, SUBLANES, 2 * MXU_DIM), F32)

    x = jnp.concatenate(
        [jnp.concatenate([x_ref[k, pl.ds(v, m, stride=STRIDE), :] for k in range(slabs)], axis=1)
         for v in range(STRIDE)], axis=0)
    inv_rms = lax.rsqrt(jnp.mean(x * x, axis=-1, keepdims=True) + EPS)
    h = (x * nw_ref[...]).astype(BF16)

    def pair(ref, c):
        return jnp.concatenate([ref[:, pl.ds(half + c * MXU_DIM, MXU_DIM)] for half in (0, d_ff)], axis=1)

    def up(c):
        r = jnp.dot(h, pair(wup_ref, c), preferred_element_type=F32) * inv_rms
        up_scr[c, pl.ds(0, tm), :] = r
        for k, s in enumerate(moved_start):
            v = STRIDE - 2 + k
            up_scr[c, pl.ds(s + 1, m), :] = r[v * m:(v + 1) * m]

    def gate(c, token):
        cw = pair(cw_ref, c)
        first = [v * m for v in range(STRIDE)]
        s1 = moved_start[1:] + first[:STRIDE - 1]
        s2 = moved_start + first[:STRIDE - 2]
        for v in range(STRIDE):
            for r in range(0, m, GATE_ROWS):
                z = _chained(up_scr[c, pl.ds(first[v] + r, GATE_ROWS), :], token, MXU_DIM)
                y = (cw[0:1, :] * up_scr[c, pl.ds(s2[v] + r, GATE_ROWS), :]
                     + cw[1:2, :] * up_scr[c, pl.ds(s1[v] + r, GATE_ROWS), :] + cw[2:3, :] * z)
                g, u = y[:, :MXU_DIM], y[:, MXU_DIM:]
                a = (g / (1.0 + jnp.exp2(g * -LOG2_E)) * u).astype(BF16)
                a_scr[c, pl.ds(v * m + r, GATE_ROWS), :] = a
                token = _zero_tile(a)
        return token

    token = None
    for c in range(nc):
        up(c)
        if c >= 1:
            token = gate(c - 1, token)
    gate(nc - 1, token)

    split = (nc - 1) * MXU_DIM
    a_head = jnp.concatenate([a_scr[c] for c in range(nc - 1)], axis=1)
    y = x + jnp.dot(a_head, wd_ref[pl.ds(0, split), :], preferred_element_type=F32)
    y = y + jnp.dot(a_scr[nc - 1], wd_ref[pl.ds(split, MXU_DIM), :], preferred_element_type=F32)

    for s in moved_start:
        up_scr[:, pl.ds(s, 1), :] = up_scr[:, pl.ds(s + m, 1), :]

    for k in range(slabs):
        for v in range(STRIDE):
            o_slab[k, pl.ds(v, m, stride=STRIDE), :] = y[v * m:(v + 1) * m, k * LANES:(k + 1) * LANES]
        o_ref[:, pl.ds(k * LANES, LANES)] = o_slab[k]


def _resident(shape):
    return pl.BlockSpec(shape, lambda b, i: (0,) * len(shape), pipeline_mode=pl.Buffered(1))


def _tile(tm, width):
    return pl.BlockSpec((None, tm, width), lambda b, i: (b, i, 0))


def _params(vmem_limit_bytes=VMEM_LIMIT_BYTES):
    return pltpu.CompilerParams(dimension_semantics=("arbitrary", "arbitrary"), vmem_limit_bytes=vmem_limit_bytes)


def _layer(x, attn_norm_w, w_in, q_norm_w, k_norm_w, sinks, conv_mix_w, attn_out_norm_w,
           conv_out_norm_w, w_out, ffn_norm_w, w_up, ffn_conv_w, w_down):
    bsz, seq, d = x.shape
    d_ff = w_down.shape[0]
    tm = TOKEN_TILE
    assert seq % tm == 0 and tm % BLOCK == 0 and d_ff % MXU_DIM == 0
    n_tiles = seq // tm
    nc = d_ff // MXU_DIM

    scale = HEAD_DIM ** -0.5
    qkw = jnp.concatenate([jnp.tile(q_norm_w, N_Q_HEADS) * scale, jnp.tile(k_norm_w, N_KV_HEADS)])[None, :]
    in_hbm = pl.BlockSpec(memory_space=pl.ANY)


    this_tile = pl.BlockSpec((None, tm, d), lambda b, i: (b, jnp.minimum(i, n_tiles - 1), 0))
    last_tile = pl.BlockSpec((None, tm, d), lambda b, i: (b, jnp.maximum(i - 1, 0), 0))
    last_slab_tile = pl.BlockSpec((None, d // LANES, tm, LANES), lambda b, i: (b, 0, jnp.maximum(i - 1, 0), 0))
    slab_tile = pl.BlockSpec((None, d // LANES, tm, LANES), lambda b, i: (b, 0, i, 0))
    in_w = w_in.shape[1]
    bpt = tm // BLOCK
    score_units = (bpt, 2 * N_KV_HEADS, N_Q_HEADS // N_KV_HEADS // 2)
    x = pl.pallas_call(
        _token_mixer_kernel,
        grid=(bsz, n_tiles + 1),
        in_specs=[pl.BlockSpec(memory_space=pltpu.SMEM), this_tile, last_tile, _resident((1, d)), in_hbm,
                  _resident((1, ATTN_WIDTH + KV_WIDTH)), _resident((CONV_TAPS, CONV_WIDTH)),
                  _resident((1, CONV_WIDTH)), in_hbm, _resident((1, ATTN_WIDTH))],
        out_specs=last_slab_tile,
        out_shape=jax.ShapeDtypeStruct((bsz, d // LANES, seq, LANES), x.dtype),
        scratch_shapes=[pltpu.VMEM((tm, in_w), F32), pltpu.VMEM((tm, in_w), F32),
                        pltpu.VMEM((tm + HALO, CONV_WIDTH), F32),
                        pltpu.VMEM((tm, ATTN_WIDTH), BF16),
                        pltpu.VMEM((BLOCK + tm, 2 * N_KV_HEADS * LANES), BF16),
                        pltpu.VMEM((KV_WIDTH, BLOCK + tm), BF16),
                        pltpu.VMEM((tm, CONV_WIDTH), BF16),
                        pltpu.VMEM(score_units + (2 * BLOCK, BLOCK), F32),
                        pltpu.VMEM(score_units + (2 * BLOCK, BLOCK), BF16),
                        pltpu.VMEM((bpt, 2 * N_KV_HEADS, SUBLANES, 2 * BLOCK), F32),
                        pltpu.VMEM((ATTN_WIDTH, tm), F32),
                        pltpu.VMEM((tm, ATTN_WIDTH), BF16),
                        pltpu.VMEM((2, 2 * BLOCK, BLOCK), jnp.int32),
                        pltpu.VMEM((d, in_w), BF16),
                        pltpu.VMEM((ATTN_WIDTH + CONV_WIDTH, d), BF16),
                        pltpu.SemaphoreType.DMA((bpt * 2 * N_KV_HEADS * 2,))],
        compiler_params=_params(),
        name="token_mixer",
    )(sinks, x, x, attn_norm_w[None, :], w_in, qkw, conv_mix_w, conv_out_norm_w[None, :], w_out,
      attn_out_norm_w[None, :])

    x = pl.pallas_call(
        _ffn_kernel,
        grid=(bsz, n_tiles),
        in_specs=[slab_tile, _resident((1, d)), in_hbm, _resident((CONV_TAPS, 2 * d_ff)), in_hbm],
        out_specs=_tile(tm, d),
        out_shape=jax.ShapeDtypeStruct((bsz, seq, d), x.dtype),
        scratch_shapes=[pltpu.VMEM((d // LANES, tm, LANES), F32),
                        pltpu.VMEM((nc, tm + 2 * (tm // STRIDE + HALO), 2 * MXU_DIM), F32),
                        pltpu.VMEM((nc, tm, MXU_DIM), BF16),
                        pltpu.VMEM((d, 2 * d_ff), BF16), pltpu.VMEM((d_ff, d), BF16),
                        pltpu.SemaphoreType.DMA((nc,))],
        compiler_params=_params(),
        name="conv_ffn",
    )(x, ffn_norm_w[None, :], w_up, ffn_conv_w, w_down)
    return x


def kernel(x, attn_norm_w, w_in, q_norm_w, k_norm_w, sinks, conv_mix_w, attn_out_norm_w,
           conv_out_norm_w, w_out, ffn_norm_w, w_up, ffn_conv_w, w_down):
    for l in range(attn_norm_w.shape[0]):
        x = _layer(x, attn_norm_w[l], w_in[l], q_norm_w[l], k_norm_w[l], sinks[l], conv_mix_w[l],
                   attn_out_norm_w[l], conv_out_norm_w[l], w_out[l], ffn_norm_w[l], w_up[l],
                   ffn_conv_w[l], w_down[l])
    return x
```

```python
import jax
import jax.numpy as jnp
from jax import lax
from jax.experimental import pallas as pl
from jax.experimental.pallas import tpu as pltpu

HEAD_DIM = 64
N_Q_HEADS = 8
N_KV_HEADS = 2
WINDOW = 128
BLOCK = 128
ATTN_WIDTH = N_Q_HEADS * HEAD_DIM
KV_WIDTH = N_KV_HEADS * HEAD_DIM
CONV_WIDTH = 512
CONV_TAPS = 3
EPS = 1e-6
NEG_INF = -1e30
LOG2_E = 1.4426950408889634

LANES = 128
SUBLANES = 8
MXU_DIM = 256
VMEM_LIMIT_BYTES = 56 * 1024 * 1024

TOKEN_TILE = 512
HALO = SUBLANES
STRIDE = 4
GATE_ROWS = 64
QKV_ROWS = 64
PIECE_ROWS = 32
PACKED_ROWS = 2 * SUBLANES

F32 = jnp.float32
BF16 = jnp.bfloat16


def _rms(x, w):
    ms = jnp.mean(x * x, axis=-1, keepdims=True)
    return x * lax.rsqrt(ms + EPS) * w


def _zero_tile(packed):
    words = pltpu.bitcast(packed, jnp.uint32)
    while words.shape[1] > LANES:
        half = words.shape[1] // 2
        words = words[:, :half] | words[:, half:]
    while words.shape[0] > SUBLANES:
        half = words.shape[0] // 2
        words = words[:half] | words[half:]
    return ((words >> 16) >> 16).astype(F32)


def _chained(x, zero_tile, width=LANES):
    if zero_tile is None:
        return x
    zeros = jnp.concatenate([jnp.concatenate([zero_tile] * (x.shape[0] // SUBLANES), axis=0)] * (width // LANES),
                            axis=1)
    if width == x.shape[1]:
        return x + zeros
    return jnp.concatenate([x[:, :width] + zeros, x[:, width:]], axis=1)


def _fetch_bf16(w_hbm, w_bf16, slots, sems, meanwhile=None):
    rows, cols = slots[0].shape
    blocks = [(r, c) for r in range(0, w_hbm.shape[0], rows) for c in range(0, w_hbm.shape[1], cols)]
    assert w_hbm.shape[0] % rows == 0 and w_hbm.shape[1] % cols == 0

    def copy(b):
        r, c = blocks[b]
        s = b % len(slots)
        return pltpu.make_async_copy(w_hbm.at[pl.ds(r, rows), pl.ds(c, cols)], slots[s], sems.at[s])

    for b in range(min(len(slots), len(blocks))):
        copy(b).start()
    if meanwhile is not None:
        meanwhile()
    for b, (r, c) in enumerate(blocks):
        copy(b).wait()
        w_bf16[pl.ds(r, rows), pl.ds(c, cols)] = slots[b % len(slots)][...].astype(BF16)
        if b + len(slots) < len(blocks):
            copy(b + len(slots)).start()


def _first_step():
    return (pl.program_id(0) == 0) & (pl.program_id(1) == 0)


def _token_mixer_kernel(sinks_ref, x_ref, xl_ref, nw_ref, win_hbm, qkw_ref, cw_ref, cnw_ref, wo_hbm, anw_ref,
                        o_ref, proj_a, proj_b, cu_scr, q_scr, kv_scr, vt_scr, cn_scr,
                        s_scr, p_scr, inv_scr, attn_t, an_scr, mask_scr, win_ref, wo_ref, w_sems):
    tm = x_ref.shape[0]
    qkv_w = ATTN_WIDTH + 2 * KV_WIDTH
    step = pl.program_id(1)

    @pl.when(_first_step())
    def _():
        _fetch_bf16(win_hbm, win_ref, [proj_a, proj_b], w_sems)

    @pl.when(step == 0)
    def _():
        cu_scr[pl.ds(0, HALO), :] = jnp.zeros((HALO, CONV_WIDTH), F32)
        kv_scr[pl.ds(0, BLOCK), :] = jnp.zeros((BLOCK, kv_scr.shape[1]), BF16)
        vt_scr[:, pl.ds(0, BLOCK)] = jnp.zeros((KV_WIDTH, BLOCK), BF16)

    def body(fill, drain, project=True, finish=True):
        if project:
            h = _rms(x_ref[...], nw_ref[...]).astype(BF16)
            fill[...] = jnp.dot(h, win_ref[...], preferred_element_type=F32)
        if not finish:
            return

        low = lax.broadcasted_iota(jnp.int32, (QKV_ROWS, LANES), 1) < HEAD_DIM
        for r in range(0, tm, QKV_ROWS):
            rows = pl.ds(r, QKV_ROWS)
            qk = drain[rows, pl.ds(0, ATTN_WIDTH + KV_WIDTH)]
            sums = []
            for t in range(0, ATTN_WIDTH + KV_WIDTH, LANES):
                sq = qk[:, t:t + LANES] * qk[:, t:t + LANES]
                sums.append(jnp.where(low, jnp.sum(jnp.where(low, sq, 0.0), axis=-1, keepdims=True),
                                      jnp.sum(jnp.where(low, 0.0, sq), axis=-1, keepdims=True)))
            qkn = qk * lax.rsqrt(jnp.concatenate(sums, axis=1) * (1.0 / HEAD_DIM) + EPS) * qkw_ref[...]
            qn = qkn[:, :ATTN_WIDTH].astype(BF16)
            q_scr[rows, :] = qn
            k = qkn[:, ATTN_WIDTH:]
            swapped = pltpu.roll(k, HEAD_DIM, axis=1)
            kk = jnp.concatenate([jnp.where(low, k, 0.0), jnp.where(low, 0.0, swapped),
                                  jnp.where(low, swapped, 0.0), jnp.where(low, 0.0, k)], axis=1).astype(BF16)
            kv_scr[pl.ds(BLOCK + r, QKV_ROWS), :] = kk
        vt_scr[:, pl.ds(BLOCK, tm)] = drain[:, pl.ds(ATTN_WIDTH + KV_WIDTH, KV_WIDTH)].T.astype(BF16)

        cw = cw_ref[...]
        token = None
        for r in range(0, tm, PIECE_ROWS):
            rows = pl.ds(r, PIECE_ROWS)
            bg = _chained(drain[rows, pl.ds(qkv_w, CONV_WIDTH)], token)
            cu_scr[pl.ds(HALO + r, PIECE_ROWS), :] = (drain[rows, pl.ds(qkv_w + CONV_WIDTH, CONV_WIDTH)]
                                                      * drain[rows, pl.ds(qkv_w + 2 * CONV_WIDTH, CONV_WIDTH)])
            y = cw[0:1, :] * cu_scr[pl.ds(HALO + r - 2, PIECE_ROWS), :]
            y = y + cw[1:2, :] * cu_scr[pl.ds(HALO + r - 1, PIECE_ROWS), :]
            y = y + cw[2:3, :] * cu_scr[pl.ds(HALO + r, PIECE_ROWS), :]
            cn = _rms(bg * y, cnw_ref[...]).astype(BF16)
            cn_scr[rows, :] = cn
            token = _zero_tile(cn)
        cu_scr[pl.ds(0, HALO), :] = cu_scr[pl.ds(tm, HALO), :]

        _attend(step > 1, sinks_ref, q_scr, kv_scr, vt_scr, s_scr, p_scr, inv_scr, attn_t, an_scr, mask_scr,
                anw_ref)
        y = jnp.dot(an_scr[...], wo_ref[pl.ds(0, ATTN_WIDTH), :], preferred_element_type=F32)
        y = y + jnp.dot(cn_scr[...], wo_ref[pl.ds(ATTN_WIDTH, CONV_WIDTH), :], preferred_element_type=F32)
        y = xl_ref[...] + y
        for k in range(o_ref.shape[0]):
            o_ref[k] = y[:, k * LANES:(k + 1) * LANES]
        kv_scr[pl.ds(0, BLOCK), :] = kv_scr[pl.ds(tm, BLOCK), :]
        vt_scr[:, pl.ds(0, BLOCK)] = vt_scr[:, pl.ds(tm, BLOCK)]

    last = pl.num_programs(1) - 1

    @pl.when(_first_step())
    def _():
        slots = [s_scr.at[j, ce, g] for j in range(s_scr.shape[0]) for ce in range(s_scr.shape[1])
                 for g in range(s_scr.shape[2])]
        _fetch_bf16(wo_hbm, wo_ref, slots, w_sems, meanwhile=lambda: body(proj_a, proj_b, finish=False))

    @pl.when((step == 0) & (pl.program_id(0) > 0))
    def _():
        body(proj_a, proj_b, finish=False)

    @pl.when((step % 2 == 0) & (step > 0) & (step < last))
    def _():
        body(proj_a, proj_b)

    @pl.when((step % 2 == 1) & (step < last))
    def _():
        body(proj_b, proj_a)

    @pl.when((step % 2 == 0) & (step == last))
    def _():
        body(proj_a, proj_b, project=False)

    @pl.when((step % 2 == 1) & (step == last))
    def _():
        body(proj_b, proj_a, project=False)


def _all_sublanes(x, op):
    for shift in (4, 2, 1):
        x = op(x, pltpu.roll(x, shift, axis=0))
    return x


def _attend(has_prev, sinks_ref, q_ref, kv_ref, vt_ref, s_scr, p_scr, inv_scr, attn_t, an_scr, mask_scr, anw_ref):
    tq = q_ref.shape[0]
    nb = tq // BLOCK
    group = N_Q_HEADS // N_KV_HEADS
    group_pairs = group // 2
    key_tiles = 2 * BLOCK // SUBLANES
    nt = (((1,), (1,)), ((), ()))

    sj = lax.broadcasted_iota(jnp.int32, (2 * BLOCK, BLOCK), 0)
    qi = lax.broadcasted_iota(jnp.int32, (2 * BLOCK, BLOCK), 1)
    rel = qi + BLOCK - sj
    band = (rel >= 0) & (rel < WINDOW)
    band0 = band & ((sj >= BLOCK) | has_prev)
    mask_scr[0] = band0.astype(jnp.int32)
    mask_scr[1] = band.astype(jnp.int32)

    def scores(j):
        rows = pl.ds(j * BLOCK, BLOCK)
        for c in range(N_KV_HEADS):
            qs = jnp.concatenate(
                [q_ref[rows, pl.ds((group_pairs * c + g) * LANES, LANES)] for g in range(group_pairs)], axis=0)
            for e in range(2):
                k = kv_ref[pl.ds(j * BLOCK, 2 * BLOCK), pl.ds((2 * c + e) * LANES, LANES)]
                s = lax.dot_general(k, qs, nt, preferred_element_type=F32)
                for g in range(group_pairs):
                    s_scr[j, 2 * c + e, g] = s[:, g * BLOCK:(g + 1) * BLOCK]

    def softmax(j):
        for ce, g in [(ce, g) for ce in range(2 * N_KV_HEADS) for g in range(group_pairs)]:
            c, e = divmod(ce, 2)
            cols = pl.ds(g * BLOCK, BLOCK)
            sink = sinks_ref[group * c + 2 * g + e]
            m = None
            for r in range(key_tiles):
                rows = pl.ds(r * SUBLANES, SUBLANES)
                s = jnp.where(mask_scr[min(j, 1), rows, :] > 0, s_scr[j, ce, g, rows, :], NEG_INF)
                s_scr[j, ce, g, rows, :] = s
                m = s if m is None else jnp.maximum(m, s)
            m = jnp.maximum(_all_sublanes(m, jnp.maximum), sink)
            m2 = jnp.concatenate([m, m], axis=0)
            total = None
            for r in range(0, 2 * BLOCK, PACKED_ROWS):
                rows = pl.ds(r, PACKED_ROWS)
                pe = jnp.exp(s_scr[j, ce, g, rows, :] - m2)
                p_scr[j, ce, g, rows, :] = pe.astype(BF16)
                part = pe[:SUBLANES] + pe[SUBLANES:]
                total = part if total is None else total + part
            denom = _all_sublanes(total, jnp.add) + jnp.exp(sink - m)
            inv_scr[j, ce, :, cols] = 1.0 / denom

    def values(j):
        for c in range(N_KV_HEADS):
            vt = vt_ref[pl.ds(c * HEAD_DIM, HEAD_DIM), pl.ds(j * BLOCK, 2 * BLOCK)]
            for e in range(2):
                inv = jnp.concatenate([inv_scr[j, 2 * c + e]] * (HEAD_DIM // SUBLANES), axis=0)
                p = jnp.concatenate([p_scr[j, 2 * c + e, g] for g in range(group_pairs)], axis=1)
                o = jnp.dot(vt, p, preferred_element_type=F32) * inv
                for g in range(group_pairs):
                    attn_t[pl.ds((group * c + 2 * g + e) * HEAD_DIM, HEAD_DIM), pl.ds(j * BLOCK, BLOCK)] = (
                        o[:, g * BLOCK:(g + 1) * BLOCK])
        a = attn_t[:, pl.ds(j * BLOCK, BLOCK)]
        ms = jnp.mean(a * a, axis=0, keepdims=True)
        an_scr[pl.ds(j * BLOCK, BLOCK), :] = ((a * lax.rsqrt(ms + EPS)).T * anw_ref[...]).astype(BF16)

    for t in range(nb + 2):
        if t < nb:
            scores(t)
        if 0 <= t - 1 < nb:
            softmax(t - 1)
        if 0 <= t - 2 < nb:
            values(t - 2)


def _ffn_kernel(x_ref, nw_ref, wup_hbm, cw_ref, wd_hbm, o_ref, o_slab, up_scr, a_scr,
                wup_ref, wd_ref, w_sems):
    slabs, tm, _ = x_ref.shape
    m = tm // STRIDE
    d_ff = wd_ref.shape[0]
    nc = d_ff // MXU_DIM
    moved_start = [STRIDE * m + k * (m + HALO) for k in range(2)]

    @pl.when(_first_step())
    def _():
        _fetch_bf16(wup_hbm, wup_ref, [up_scr.at[c, pl.ds(0, tm), :] for c in range(nc)], w_sems)
        _fetch_bf16(wd_hbm, wd_ref, [up_scr.at[c, pl.ds(0, MXU_DIM), :] for c in range(nc)], w_sems)

    @pl.when(pl.program_id(1) == 0)
    def _():
        for s in moved_start:
            up_scr[:, pl.ds(s, SUBLANES), :] = jnp.zeros((nc, SUBLANES, 2 * MXU_DIM), F32)

    x = jnp.concatenate(
        [jnp.concatenate([x_ref[k, pl.ds(v, m, stride=STRIDE), :] for k in range(slabs)], axis=1)
         for v in range(STRIDE)], axis=0)
    inv_rms = lax.rsqrt(jnp.mean(x * x, axis=-1, keepdims=True) + EPS)
    h = (x * nw_ref[...]).astype(BF16)

    def pair(ref, c):
        return jnp.concatenate([ref[:, pl.ds(half + c * MXU_DIM, MXU_DIM)] for half in (0, d_ff)], axis=1)

    def up(c):
        r = jnp.dot(h, pair(wup_ref, c), preferred_element_type=F32) * inv_rms
        up_scr[c, pl.ds(0, tm), :] = r
        for k, s in enumerate(moved_start):
            v = STRIDE - 2 + k
            up_scr[c, pl.ds(s + 1, m), :] = r[v * m:(v + 1) * m]

    def gate(c, token):
        cw = pair(cw_ref, c)
        first = [v * m for v in range(STRIDE)]
        s1 = moved_start[1:] + first[:STRIDE - 1]
        s2 = moved_start + first[:STRIDE - 2]
        for v in range(STRIDE):
            for r in range(0, m, GATE_ROWS):
                z = _chained(up_scr[c, pl.ds(first[v] + r, GATE_ROWS), :], token, MXU_DIM)
                y = (cw[0:1, :] * up_scr[c, pl.ds(s2[v] + r, GATE_ROWS), :]
                     + cw[1:2, :] * up_scr[c, pl.ds(s1[v] + r, GATE_ROWS), :] + cw[2:3, :] * z)
                g, u = y[:, :MXU_DIM], y[:, MXU_DIM:]
                a = (g / (1.0 + jnp.exp2(g * -LOG2_E)) * u).astype(BF16)
                a_scr[c, pl.ds(v * m + r, GATE_ROWS), :] = a
                token = _zero_tile(a)
        return token

    token = None
    for c in range(nc):
        up(c)
        if c >= 1:
            token = gate(c - 1, token)
    gate(nc - 1, token)

    split = (nc - 1) * MXU_DIM
    a_head = jnp.concatenate([a_scr[c] for c in range(nc - 1)], axis=1)
    y = x + jnp.dot(a_head, wd_ref[pl.ds(0, split), :], preferred_element_type=F32)
    y = y + jnp.dot(a_scr[nc - 1], wd_ref[pl.ds(split, MXU_DIM), :], preferred_element_type=F32)

    for s in moved_start:
        up_scr[:, pl.ds(s, 1), :] = up_scr[:, pl.ds(s + m, 1), :]

    for k in range(slabs):
        for v in range(STRIDE):
            o_slab[k, pl.ds(v, m, stride=STRIDE), :] = y[v * m:(v + 1) * m, k * LANES:(k + 1) * LANES]
        o_ref[:, pl.ds(k * LANES, LANES)] = o_slab[k]


def _resident(shape):
    return pl.BlockSpec(shape, lambda b, i: (0,) * len(shape), pipeline_mode=pl.Buffered(1))


def _tile(tm, width):
    return pl.BlockSpec((None, tm, width), lambda b, i: (b, i, 0))


def _params(vmem_limit_bytes=VMEM_LIMIT_BYTES):
    return pltpu.CompilerParams(dimension_semantics=("arbitrary", "arbitrary"), vmem_limit_bytes=vmem_limit_bytes)


def _layer(x, attn_norm_w, w_in, q_norm_w, k_norm_w, sinks, conv_mix_w, attn_out_norm_w,
           conv_out_norm_w, w_out, ffn_norm_w, w_up, ffn_conv_w, w_down):
    bsz, seq, d = x.shape
    d_ff = w_down.shape[0]
    tm = TOKEN_TILE
    assert seq % tm == 0 and tm % BLOCK == 0 and d_ff % MXU_DIM == 0
    n_tiles = seq // tm
    nc = d_ff // MXU_DIM

    scale = HEAD_DIM ** -0.5
    qkw = jnp.concatenate([jnp.tile(q_norm_w, N_Q_HEADS) * scale, jnp.tile(k_norm_w, N_KV_HEADS)])[None, :]
    in_hbm = pl.BlockSpec(memory_space=pl.ANY)


    this_tile = pl.BlockSpec((None, tm, d), lambda b, i: (b, jnp.minimum(i, n_tiles - 1), 0))
    last_tile = pl.BlockSpec((None, tm, d), lambda b, i: (b, jnp.maximum(i - 1, 0), 0))
    last_slab_tile = pl.BlockSpec((None, d // LANES, tm, LANES), lambda b, i: (b, 0, jnp.maximum(i - 1, 0), 0))
    slab_tile = pl.BlockSpec((None, d // LANES, tm, LANES), lambda b, i: (b, 0, i, 0))
    in_w = w_in.shape[1]
    bpt = tm // BLOCK
    score_units = (bpt, 2 * N_KV_HEADS, N_Q_HEADS // N_KV_HEADS // 2)
    x = pl.pallas_call(
        _token_mixer_kernel,
        grid=(bsz, n_tiles + 1),
        in_specs=[pl.BlockSpec(memory_space=pltpu.SMEM), this_tile, last_tile, _resident((1, d)), in_hbm,
                  _resident((1, ATTN_WIDTH + KV_WIDTH)), _resident((CONV_TAPS, CONV_WIDTH)),
                  _resident((1, CONV_WIDTH)), in_hbm, _resident((1, ATTN_WIDTH))],
        out_specs=last_slab_tile,
        out_shape=jax.ShapeDtypeStruct((bsz, d // LANES, seq, LANES), x.dtype),
        scratch_shapes=[pltpu.VMEM((tm, in_w), F32), pltpu.VMEM((tm, in_w), F32),
                        pltpu.VMEM((tm + HALO, CONV_WIDTH), F32),
                        pltpu.VMEM((tm, ATTN_WIDTH), BF16),
                        pltpu.VMEM((BLOCK + tm, 2 * N_KV_HEADS * LANES), BF16),
                        pltpu.VMEM((KV_WIDTH, BLOCK + tm), BF16),
                        pltpu.VMEM((tm, CONV_WIDTH), BF16),
                        pltpu.VMEM(score_units + (2 * BLOCK, BLOCK), F32),
                        pltpu.VMEM(score_units + (2 * BLOCK, BLOCK), BF16),
                        pltpu.VMEM((bpt, 2 * N_KV_HEADS, SUBLANES, 2 * BLOCK), F32),
                        pltpu.VMEM((ATTN_WIDTH, tm), F32),
                        pltpu.VMEM((tm, ATTN_WIDTH), BF16),
                        pltpu.VMEM((2, 2 * BLOCK, BLOCK), jnp.int32),
                        pltpu.VMEM((d, in_w), BF16),
                        pltpu.VMEM((ATTN_WIDTH + CONV_WIDTH, d), BF16),
                        pltpu.SemaphoreType.DMA((bpt * 2 * N_KV_HEADS * 2,))],
        compiler_params=_params(),
        name="token_mixer",
    )(sinks, x, x, attn_norm_w[None, :], w_in, qkw, conv_mix_w, conv_out_norm_w[None, :], w_out,
      attn_out_norm_w[None, :])

    x = pl.pallas_call(
        _ffn_kernel,
        grid=(bsz, n_tiles),
        in_specs=[slab_tile, _resident((1, d)), in_hbm, _resident((CONV_TAPS, 2 * d_ff)), in_hbm],
        out_specs=_tile(tm, d),
        out_shape=jax.ShapeDtypeStruct((bsz, seq, d), x.dtype),
        scratch_shapes=[pltpu.VMEM((d // LANES, tm, LANES), F32),
                        pltpu.VMEM((nc, tm + 2 * (tm // STRIDE + HALO), 2 * MXU_DIM), F32),
                        pltpu.VMEM((nc, tm, MXU_DIM), BF16),
                        pltpu.VMEM((d, 2 * d_ff), BF16), pltpu.VMEM((d_ff, d), BF16),
                        pltpu.SemaphoreType.DMA((nc,))],
        compiler_params=_params(),
        name="conv_ffn",
    )(x, ffn_norm_w[None, :], w_up, ffn_conv_w, w_down)
    return x


def kernel(x, attn_norm_w, w_in, q_norm_w, k_norm_w, sinks, conv_mix_w, attn_out_norm_w,
           conv_out_norm_w, w_out, ffn_norm_w, w_up, ffn_conv_w, w_down):
    for l in range(attn_norm_w.shape[0]):
        x = _layer(x, attn_norm_w[l], w_in[l], q_norm_w[l], k_norm_w[l], sinks[l], conv_mix_w[l],
                   attn_out_norm_w[l], conv_out_norm_w[l], w_out[l], ffn_norm_w[l], w_up[l],
                   ffn_conv_w[l], w_down[l])
    return x
```

```python
import jax
import jax.numpy as jnp
from jax import lax
from jax.experimental import pallas as pl
from jax.experimental.pallas import tpu as pltpu

HEAD_DIM = 64
N_Q_HEADS = 8
N_KV_HEADS = 2
WINDOW = 128
BLOCK = 128
ATTN_WIDTH = N_Q_HEADS * HEAD_DIM
KV_WIDTH = N_KV_HEADS * HEAD_DIM
CONV_WIDTH = 512
CONV_TAPS = 3
EPS = 1e-6
NEG_INF = -1e30
LOG2_E = 1.4426950408889634

LANES = 128
SUBLANES = 8
MXU_DIM = 256
VMEM_LIMIT_BYTES = 56 * 1024 * 1024

TOKEN_TILE = 512
HALO = SUBLANES
STRIDE = 4
GATE_ROWS = 64
QKV_ROWS = 64
PIECE_ROWS = 32
PACKED_ROWS = 2 * SUBLANES
DMA_PRIORITIES = 2

F32 = jnp.float32
BF16 = jnp.bfloat16


def _rms(x, w):
    ms = jnp.mean(x * x, axis=-1, keepdims=True)
    return x * lax.rsqrt(ms + EPS) * w


def _zero_tile(packed):
    words = pltpu.bitcast(packed, jnp.uint32)
    while words.shape[1] > LANES:
        half = words.shape[1] // 2
        words = words[:, :half] | words[:, half:]
    while words.shape[0] > SUBLANES:
        half = words.shape[0] // 2
        words = words[:half] | words[half:]
    return ((words >> 16) >> 16).astype(F32)


def _chained(x, zero_tile, width=LANES):
    if zero_tile is None:
        return x
    zeros = jnp.concatenate([jnp.concatenate([zero_tile] * (x.shape[0] // SUBLANES), axis=0)] * (width // LANES),
                            axis=1)
    if width == x.shape[1]:
        return x + zeros
    return jnp.concatenate([x[:, :width] + zeros, x[:, width:]], axis=1)


def _fetch_bf16(w_hbm, w_bf16, slots, sems, meanwhile=None):
    rows, cols = slots[0].shape
    blocks = [(r, c) for r in range(0, w_hbm.shape[0], rows) for c in range(0, w_hbm.shape[1], cols)]
    assert w_hbm.shape[0] % rows == 0 and w_hbm.shape[1] % cols == 0

    def copy(b):
        r, c = blocks[b]
        s = b % len(slots)
        return pltpu.make_async_copy(w_hbm.at[pl.ds(r, rows), pl.ds(c, cols)], slots[s], sems.at[s])

    for b in range(min(len(slots), len(blocks))):
        copy(b).start(priority=b % DMA_PRIORITIES)
    if meanwhile is not None:
        meanwhile()
    for b, (r, c) in enumerate(blocks):
        copy(b).wait()
        w_bf16[pl.ds(r, rows), pl.ds(c, cols)] = slots[b % len(slots)][...].astype(BF16)
        if b + len(slots) < len(blocks):
            copy(b + len(slots)).start(priority=(b + len(slots)) % DMA_PRIORITIES)


def _first_step():
    return (pl.program_id(0) == 0) & (pl.program_id(1) == 0)


def _token_mixer_kernel(sinks_ref, x_ref, xl_ref, nw_ref, win_hbm, qkw_ref, cw_ref, cnw_ref, wo_hbm, anw_ref,
                        o_ref, proj_a, proj_b, cu_scr, q_scr, kv_scr, vt_scr, cn_scr,
                        s_scr, p_scr, inv_scr, attn_t, an_scr, mask_scr, win_ref, wo_ref, w_sems):
    tm = x_ref.shape[0]
    qkv_w = ATTN_WIDTH + 2 * KV_WIDTH
    step = pl.program_id(1)

    @pl.when(_first_step())
    def _():
        _fetch_bf16(win_hbm, win_ref, [proj_a, proj_b], w_sems)

    @pl.when(step == 0)
    def _():
        cu_scr[pl.ds(0, HALO), :] = jnp.zeros((HALO, CONV_WIDTH), F32)
        kv_scr[pl.ds(0, BLOCK), :] = jnp.zeros((BLOCK, kv_scr.shape[1]), BF16)
        vt_scr[:, pl.ds(0, BLOCK)] = jnp.zeros((KV_WIDTH, BLOCK), BF16)

    def body(fill, drain, project=True, finish=True):
        if project:
            h = _rms(x_ref[...], nw_ref[...]).astype(BF16)
            fill[...] = jnp.dot(h, win_ref[...], preferred_element_type=F32)
        if not finish:
            return

        low = lax.broadcasted_iota(jnp.int32, (QKV_ROWS, LANES), 1) < HEAD_DIM
        for r in range(0, tm, QKV_ROWS):
            rows = pl.ds(r, QKV_ROWS)
            qk = drain[rows, pl.ds(0, ATTN_WIDTH + KV_WIDTH)]
            sums = []
            for t in range(0, ATTN_WIDTH + KV_WIDTH, LANES):
                sq = qk[:, t:t + LANES] * qk[:, t:t + LANES]
                sums.append(jnp.where(low, jnp.sum(jnp.where(low, sq, 0.0), axis=-1, keepdims=True),
                                      jnp.sum(jnp.where(low, 0.0, sq), axis=-1, keepdims=True)))
            qkn = qk * lax.rsqrt(jnp.concatenate(sums, axis=1) * (1.0 / HEAD_DIM) + EPS) * qkw_ref[...]
            qn = qkn[:, :ATTN_WIDTH].astype(BF16)
            q_scr[rows, :] = qn
            k = qkn[:, ATTN_WIDTH:]
            swapped = pltpu.roll(k, HEAD_DIM, axis=1)
            kk = jnp.concatenate([jnp.where(low, k, 0.0), jnp.where(low, 0.0, swapped),
                                  jnp.where(low, swapped, 0.0), jnp.where(low, 0.0, k)], axis=1).astype(BF16)
            kv_scr[pl.ds(BLOCK + r, QKV_ROWS), :] = kk
        vt_scr[:, pl.ds(BLOCK, tm)] = drain[:, pl.ds(ATTN_WIDTH + KV_WIDTH, KV_WIDTH)].T.astype(BF16)

        cw = cw_ref[...]
        token = None
        for r in range(0, tm, PIECE_ROWS):
            rows = pl.ds(r, PIECE_ROWS)
            bg = _chained(drain[rows, pl.ds(qkv_w, CONV_WIDTH)], token)
            cu_scr[pl.ds(HALO + r, PIECE_ROWS), :] = (drain[rows, pl.ds(qkv_w + CONV_WIDTH, CONV_WIDTH)]
                                                      * drain[rows, pl.ds(qkv_w + 2 * CONV_WIDTH, CONV_WIDTH)])
            y = cw[0:1, :] * cu_scr[pl.ds(HALO + r - 2, PIECE_ROWS), :]
            y = y + cw[1:2, :] * cu_scr[pl.ds(HALO + r - 1, PIECE_ROWS), :]
            y = y + cw[2:3, :] * cu_scr[pl.ds(HALO + r, PIECE_ROWS), :]
            cn = _rms(bg * y, cnw_ref[...]).astype(BF16)
            cn_scr[rows, :] = cn
            token = _zero_tile(cn)
        cu_scr[pl.ds(0, HALO), :] = cu_scr[pl.ds(tm, HALO), :]

        _attend(step > 1, sinks_ref, q_scr, kv_scr, vt_scr, s_scr, p_scr, inv_scr, attn_t, an_scr, mask_scr,
                anw_ref)
        y = jnp.dot(an_scr[...], wo_ref[pl.ds(0, ATTN_WIDTH), :], preferred_element_type=F32)
        y = y + jnp.dot(cn_scr[...], wo_ref[pl.ds(ATTN_WIDTH, CONV_WIDTH), :], preferred_element_type=F32)
        y = xl_ref[...] + y
        for k in range(o_ref.shape[0]):
            o_ref[k] = y[:, k * LANES:(k + 1) * LANES]
        kv_scr[pl.ds(0, BLOCK), :] = kv_scr[pl.ds(tm, BLOCK), :]
        vt_scr[:, pl.ds(0, BLOCK)] = vt_scr[:, pl.ds(tm, BLOCK)]

    last = pl.num_programs(1) - 1

    @pl.when(_first_step())
    def _():
        slots = [s_scr.at[j, ce, g] for j in range(s_scr.shape[0]) for ce in range(s_scr.shape[1])
                 for g in range(s_scr.shape[2])]
        _fetch_bf16(wo_hbm, wo_ref, slots, w_sems, meanwhile=lambda: body(proj_a, proj_b, finish=False))

    @pl.when((step == 0) & (pl.program_id(0) > 0))
    def _():
        body(proj_a, proj_b, finish=False)

    @pl.when((step % 2 == 0) & (step > 0) & (step < last))
    def _():
        body(proj_a, proj_b)

    @pl.when((step % 2 == 1) & (step < last))
    def _():
        body(proj_b, proj_a)

    @pl.when((step % 2 == 0) & (step == last))
    def _():
        body(proj_a, proj_b, project=False)

    @pl.when((step % 2 == 1) & (step == last))
    def _():
        body(proj_b, proj_a, project=False)


def _all_sublanes(x, op):
    for shift in (4, 2, 1):
        x = op(x, pltpu.roll(x, shift, axis=0))
    return x


def _attend(has_prev, sinks_ref, q_ref, kv_ref, vt_ref, s_scr, p_scr, inv_scr, attn_t, an_scr, mask_scr, anw_ref):
    tq = q_ref.shape[0]
    nb = tq // BLOCK
    group = N_Q_HEADS // N_KV_HEADS
    group_pairs = group // 2
    key_tiles = 2 * BLOCK // SUBLANES
    nt = (((1,), (1,)), ((), ()))

    sj = lax.broadcasted_iota(jnp.int32, (2 * BLOCK, BLOCK), 0)
    qi = lax.broadcasted_iota(jnp.int32, (2 * BLOCK, BLOCK), 1)
    rel = qi + BLOCK - sj
    band = (rel >= 0) & (rel < WINDOW)
    band0 = band & ((sj >= BLOCK) | has_prev)
    mask_scr[0] = band0.astype(jnp.int32)
    mask_scr[1] = band.astype(jnp.int32)

    def scores(j):
        rows = pl.ds(j * BLOCK, BLOCK)
        for c in range(N_KV_HEADS):
            qs = jnp.concatenate(
                [q_ref[rows, pl.ds((group_pairs * c + g) * LANES, LANES)] for g in range(group_pairs)], axis=0)
            for e in range(2):
                k = kv_ref[pl.ds(j * BLOCK, 2 * BLOCK), pl.ds((2 * c + e) * LANES, LANES)]
                s = lax.dot_general(k, qs, nt, preferred_element_type=F32)
                for g in range(group_pairs):
                    s_scr[j, 2 * c + e, g] = s[:, g * BLOCK:(g + 1) * BLOCK]

    def softmax(j):
        for ce, g in [(ce, g) for ce in range(2 * N_KV_HEADS) for g in range(group_pairs)]:
            c, e = divmod(ce, 2)
            cols = pl.ds(g * BLOCK, BLOCK)
            sink = sinks_ref[group * c + 2 * g + e]
            m = None
            for r in range(key_tiles):
                rows = pl.ds(r * SUBLANES, SUBLANES)
                s = jnp.where(mask_scr[min(j, 1), rows, :] > 0, s_scr[j, ce, g, rows, :], NEG_INF)
                s_scr[j, ce, g, rows, :] = s
                m = s if m is None else jnp.maximum(m, s)
            m = jnp.maximum(_all_sublanes(m, jnp.maximum), sink)
            m2 = jnp.concatenate([m, m], axis=0)
            total = None
            for r in range(0, 2 * BLOCK, PACKED_ROWS):
                rows = pl.ds(r, PACKED_ROWS)
                pe = jnp.exp(s_scr[j, ce, g, rows, :] - m2)
                p_scr[j, ce, g, rows, :] = pe.astype(BF16)
                part = pe[:SUBLANES] + pe[SUBLANES:]
                total = part if total is None else total + part
            denom = _all_sublanes(total, jnp.add) + jnp.exp(sink - m)
            inv_scr[j, ce, :, cols] = 1.0 / denom

    def values(j):
        for c in range(N_KV_HEADS):
            vt = vt_ref[pl.ds(c * HEAD_DIM, HEAD_DIM), pl.ds(j * BLOCK, 2 * BLOCK)]
            for e in range(2):
                inv = jnp.concatenate([inv_scr[j, 2 * c + e]] * (HEAD_DIM // SUBLANES), axis=0)
                p = jnp.concatenate([p_scr[j, 2 * c + e, g] for g in range(group_pairs)], axis=1)
                o = jnp.dot(vt, p, preferred_element_type=F32) * inv
                for g in range(group_pairs):
                    attn_t[pl.ds((group * c + 2 * g + e) * HEAD_DIM, HEAD_DIM), pl.ds(j * BLOCK, BLOCK)] = (
                        o[:, g * BLOCK:(g + 1) * BLOCK])
        a = attn_t[:, pl.ds(j * BLOCK, BLOCK)]
        ms = jnp.mean(a * a, axis=0, keepdims=True)
        an_scr[pl.ds(j * BLOCK, BLOCK), :] = ((a * lax.rsqrt(ms + EPS)).T * anw_ref[...]).astype(BF16)

    for t in range(nb + 2):
        if t < nb:
            scores(t)
        if 0 <= t - 1 < nb:
            softmax(t - 1)
        if 0 <= t - 2 < nb:
            values(t - 2)


def _ffn_kernel(x_ref, nw_ref, wup_hbm, cw_ref, wd_hbm, o_ref, o_slab, up_scr, a_scr,
                wup_ref, wd_ref, w_sems):
    slabs, tm, _ = x_ref.shape
    m = tm // STRIDE
    d_ff = wd_ref.shape[0]
    nc = d_ff // MXU_DIM
    moved_start = [STRIDE * m + k * (m + HALO) for k in range(2)]

    @pl.when(_first_step())
    def _():
        _fetch_bf16(wup_hbm, wup_ref, [up_scr.at[c, pl.ds(0, tm), :] for c in range(nc)], w_sems)
        _fetch_bf16(wd_hbm, wd_ref, [up_scr.at[c, pl.ds(0, MXU_DIM), :] for c in range(nc)], w_sems)

    @pl.when(pl.program_id(1) == 0)
    def _():
        for s in moved_start:
            up_scr[:, pl.ds(s, SUBLANES), :] = jnp.zeros((nc, SUBLANES, 2 * MXU_DIM), F32)

    x = jnp.concatenate(
        [jnp.concatenate([x_ref[k, pl.ds(v, m, stride=STRIDE), :] for k in range(slabs)], axis=1)
         for v in range(STRIDE)], axis=0)
    inv_rms = lax.rsqrt(jnp.mean(x * x, axis=-1, keepdims=True) + EPS)
    h = (x * nw_ref[...]).astype(BF16)

    def pair(ref, c):
        return jnp.concatenate([ref[:, pl.ds(half + c * MXU_DIM, MXU_DIM)] for half in (0, d_ff)], axis=1)

    def up(c):
        r = jnp.dot(h, pair(wup_ref, c), preferred_element_type=F32) * inv_rms
        up_scr[c, pl.ds(0, tm), :] = r
        for k, s in enumerate(moved_start):
            v = STRIDE - 2 + k
            up_scr[c, pl.ds(s + 1, m), :] = r[v * m:(v + 1) * m]

    def gate(c, token):
        cw = pair(cw_ref, c)
        first = [v * m for v in range(STRIDE)]
        s1 = moved_start[1:] + first[:STRIDE - 1]
        s2 = moved_start + first[:STRIDE - 2]
        for v in range(STRIDE):
            for r in range(0, m, GATE_ROWS):
                z = _chained(up_scr[c, pl.ds(first[v] + r, GATE_ROWS), :], token, MXU_DIM)
                y = (cw[0:1, :] * up_scr[c, pl.ds(s2[v] + r, GATE_ROWS), :]
                     + cw[1:2, :] * up_scr[c, pl.ds(s1[v] + r, GATE_ROWS), :] + cw[2:3, :] * z)
                g, u = y[:, :MXU_DIM], y[:, MXU_DIM:]
                a = (g / (1.0 + jnp.exp2(g * -LOG2_E)) * u).astype(BF16)
                a_scr[c, pl.ds(v * m + r, GATE_ROWS), :] = a
                token = _zero_tile(a)
        return token

    token = None
    for c in range(nc):
        up(c)
        if c >= 1:
            token = gate(c - 1, token)
    gate(nc - 1, token)

    split = (nc - 1) * MXU_DIM
    a_head = jnp.concatenate([a_scr[c] for c in range(nc - 1)], axis=1)
    y = x + jnp.dot(a_head, wd_ref[pl.ds(0, split), :], preferred_element_type=F32)
    y = y + jnp.dot(a_scr[nc - 1], wd_ref[pl.ds(split, MXU_DIM), :], preferred_element_type=F32)

    for s in moved_start:
        up_scr[:, pl.ds(s, 1), :] = up_scr[:, pl.ds(s + m, 1), :]

    for k in range(slabs):
        for v in range(STRIDE):
            o_slab[k, pl.ds(v, m, stride=STRIDE), :] = y[v * m:(v + 1) * m, k * LANES:(k + 1) * LANES]
        o_ref[:, pl.ds(k * LANES, LANES)] = o_slab[k]


def _resident(shape):
    return pl.BlockSpec(shape, lambda b, i: (0,) * len(shape), pipeline_mode=pl.Buffered(1))


def _tile(tm, width):
    return pl.BlockSpec((None, tm, width), lambda b, i: (b, i, 0))


def _params(vmem_limit_bytes=VMEM_LIMIT_BYTES):
    return pltpu.CompilerParams(dimension_semantics=("arbitrary", "arbitrary"), vmem_limit_bytes=vmem_limit_bytes)


def _layer(x, attn_norm_w, w_in, q_norm_w, k_norm_w, sinks, conv_mix_w, attn_out_norm_w,
           conv_out_norm_w, w_out, ffn_norm_w, w_up, ffn_conv_w, w_down):
    bsz, seq, d = x.shape
    d_ff = w_down.shape[0]
    tm = TOKEN_TILE
    assert seq % tm == 0 and tm % BLOCK == 0 and d_ff % MXU_DIM == 0
    n_tiles = seq // tm
    nc = d_ff // MXU_DIM

    scale = HEAD_DIM ** -0.5
    qkw = jnp.concatenate([jnp.tile(q_norm_w, N_Q_HEADS) * scale, jnp.tile(k_norm_w, N_KV_HEADS)])[None, :]
    in_hbm = pl.BlockSpec(memory_space=pl.ANY)


    this_tile = pl.BlockSpec((None, tm, d), lambda b, i: (b, jnp.minimum(i, n_tiles - 1), 0))
    last_tile = pl.BlockSpec((None, tm, d), lambda b, i: (b, jnp.maximum(i - 1, 0), 0))
    last_slab_tile = pl.BlockSpec((None, d // LANES, tm, LANES), lambda b, i: (b, 0, jnp.maximum(i - 1, 0), 0))
    slab_tile = pl.BlockSpec((None, d // LANES, tm, LANES), lambda b, i: (b, 0, i, 0))
    in_w = w_in.shape[1]
    bpt = tm // BLOCK
    score_units = (bpt, 2 * N_KV_HEADS, N_Q_HEADS // N_KV_HEADS // 2)
    x = pl.pallas_call(
        _token_mixer_kernel,
        grid=(bsz, n_tiles + 1),
        in_specs=[pl.BlockSpec(memory_space=pltpu.SMEM), this_tile, last_tile, _resident((1, d)), in_hbm,
                  _resident((1, ATTN_WIDTH + KV_WIDTH)), _resident((CONV_TAPS, CONV_WIDTH)),
                  _resident((1, CONV_WIDTH)), in_hbm, _resident((1, ATTN_WIDTH))],
        out_specs=last_slab_tile,
        out_shape=jax.ShapeDtypeStruct((bsz, d // LANES, seq, LANES), x.dtype),
        scratch_shapes=[pltpu.VMEM((tm, in_w), F32), pltpu.VMEM((tm, in_w), F32),
                        pltpu.VMEM((tm + HALO, CONV_WIDTH), F32),
                        pltpu.VMEM((tm, ATTN_WIDTH), BF16),
                        pltpu.VMEM((BLOCK + tm, 2 * N_KV_HEADS * LANES), BF16),
                        pltpu.VMEM((KV_WIDTH, BLOCK + tm), BF16),
                        pltpu.VMEM((tm, CONV_WIDTH), BF16),
                        pltpu.VMEM(score_units + (2 * BLOCK, BLOCK), F32),
                        pltpu.VMEM(score_units + (2 * BLOCK, BLOCK), BF16),
                        pltpu.VMEM((bpt, 2 * N_KV_HEADS, SUBLANES, 2 * BLOCK), F32),
                        pltpu.VMEM((ATTN_WIDTH, tm), F32),
                        pltpu.VMEM((tm, ATTN_WIDTH), BF16),
                        pltpu.VMEM((2, 2 * BLOCK, BLOCK), jnp.int32),
                        pltpu.VMEM((d, in_w), BF16),
                        pltpu.VMEM((ATTN_WIDTH + CONV_WIDTH, d), BF16),
                        pltpu.SemaphoreType.DMA((bpt * 2 * N_KV_HEADS * 2,))],
        compiler_params=_params(),
        name="token_mixer",
    )(sinks, x, x, attn_norm_w[None, :], w_in, qkw, conv_mix_w, conv_out_norm_w[None, :], w_out,
      attn_out_norm_w[None, :])

    x = pl.pallas_call(
        _ffn_kernel,
        grid=(bsz, n_tiles),
        in_specs=[slab_tile, _resident((1, d)), in_hbm, _resident((CONV_TAPS, 2 * d_ff)), in_hbm],
        out_specs=_tile(tm, d),
        out_shape=jax.ShapeDtypeStruct((bsz, seq, d), x.dtype),
        scratch_shapes=[pltpu.VMEM((d // LANES, tm, LANES), F32),
                        pltpu.VMEM((nc, tm + 2 * (tm // STRIDE + HALO), 2 * MXU_DIM), F32),
                        pltpu.VMEM((nc, tm, MXU_DIM), BF16),
                        pltpu.VMEM((d, 2 * d_ff), BF16), pltpu.VMEM((d_ff, d), BF16),
                        pltpu.SemaphoreType.DMA((nc,))],
        compiler_params=_params(),
        name="conv_ffn",
    )(x, ffn_norm_w[None, :], w_up, ffn_conv_w, w_down)
    return x


def kernel(x, attn_norm_w, w_in, q_norm_w, k_norm_w, sinks, conv_mix_w, attn_out_norm_w,
           conv_out_norm_w, w_out, ffn_norm_w, w_up, ffn_conv_w, w_down):
    for l in range(attn_norm_w.shape[0]):
        x = _layer(x, attn_norm_w[l], w_in[l], q_norm_w[l], k_norm_w[l], sinks[l], conv_mix_w[l],
                   attn_out_norm_w[l], conv_out_norm_w[l], w_out[l], ffn_norm_w[l], w_up[l],
                   ffn_conv_w[l], w_down[l])
    return x
```
